```python
import math
import jax, jax.numpy as jnp
from jax import lax
import numpy as np

D_MODEL = 4096
BATCH = 4
SEQ = 2048
DEPTH = 4

N_MIXERS = 4
HEAD_DIM = 128
N_HEADS = D_MODEL // HEAD_DIM
ATTN_WIDTH = N_HEADS * HEAD_DIM
D_FF = (3 * D_MODEL) // 2
RMS_EPS = 1e-6
REL_BUCKETS = 32
REL_MAX_DIST = 2048
BAND_BLOCK = 128
NSA_KV_HEADS = 4
NSA_CMP_LEN = 32
NSA_CMP_STRIDE = 16
NSA_SEL_LEN = 64
NSA_SEL_TOPN = 16
NSA_WINDOW = 512
NSA_QCHUNK = 64
DIL_PAIRS = ((128, 1), (512, 4), (2048, 16))
DIL_KV_HEADS = 8
MOBA_BLOCK = 256
MOBA_TOPK = 3
MOBA_KV_HEADS = 8
MOBA_QCHUNK = 8
SWA_WINDOW = 128
SWA_KV_HEADS = 4
ATTN_SCALE = HEAD_DIM ** -0.5
NEG_INF = -1e30
TINY = 1e-20
FORCED_SCORE = 1e9
NSA_IN_COLS = ATTN_WIDTH + 6 * NSA_KV_HEADS * HEAD_DIM + 3 * N_HEADS
DIL_GROUP_COLS = ATTN_WIDTH + 2 * DIL_KV_HEADS * HEAD_DIM
DIL_IN_COLS = len(DIL_PAIRS) * DIL_GROUP_COLS
MOBA_IN_COLS = ATTN_WIDTH + 2 * MOBA_KV_HEADS * HEAD_DIM
SWA_IN_COLS = ATTN_WIDTH + 2 * SWA_KV_HEADS * HEAD_DIM

kernel_name = 'hybrid_interleaved_sparse_attn_macaron'


def _uses(m):
    return len(range(m, DEPTH, N_MIXERS))


def rms_norm(x, g):
    xf = x.astype(jnp.float32)
    y = xf * lax.rsqrt(jnp.mean(xf * xf, axis=-1, keepdims=True) + RMS_EPS)
    return (y * g.astype(jnp.float32)).astype(x.dtype)


def swiglu(x, w_gate, w_up, w_down):
    return (jax.nn.silu(x @ w_gate) * (x @ w_up)) @ w_down


def split_cols(y, sizes):
    return jnp.split(y, [int(c) for c in np.cumsum(sizes)[:-1]], axis=-1)


def to_chunks(x, size):
    return jnp.moveaxis(x.reshape(x.shape[0], x.shape[1] // size, size, *x.shape[2:]), 1, 0)


def from_chunks(y):
    y = jnp.moveaxis(y, 0, 1)
    return y.reshape(y.shape[0], y.shape[1] * y.shape[2], *y.shape[3:])


def to_residue(x, r):
    b, s, rest = x.shape[0], x.shape[1], x.shape[2:]
    y = x.reshape(b, s // r, r, *rest)
    return jnp.moveaxis(y, 2, 1).reshape(b * r, s // r, *rest)


def from_residue(y, r, b):
    l, rest = y.shape[1], y.shape[2:]
    y = jnp.moveaxis(y.reshape(b, r, l, *rest), 1, 2)
    return y.reshape(b, l * r, *rest)


def t5_bias(rel_table, dist):
    n = jnp.maximum(dist, 0)
    exact = REL_BUCKETS // 2
    nf = jnp.maximum(n, 1).astype(jnp.float32)
    large = exact + (jnp.log(nf / exact) * ((REL_BUCKETS - exact) / math.log(REL_MAX_DIST / exact))).astype(jnp.int32)
    bucket = jnp.where(n < exact, n, jnp.minimum(large, REL_BUCKETS - 1))
    heads = jnp.arange(rel_table.shape[1])[:, None]
    return rel_table.T[heads, bucket].astype(jnp.float32)


def softmax_parts(logits, mask, sink=None):
    logits = jnp.where(mask, logits, NEG_INF)
    m = jnp.max(logits, axis=-1, keepdims=True)
    if sink is not None:
        sink = sink.astype(jnp.float32)[:, None]
        m = jnp.maximum(m, sink)
    p = jnp.where(mask, jnp.exp(logits - m), 0.0)
    s = jnp.sum(p, axis=-1, keepdims=True)
    if sink is not None:
        s = s + jnp.exp(sink - m)
    return p, s, m


def banded_attention(q, k, v, rel_table, max_dist, dist_scale, sink=None):
    bsz, seqlen, G, R, Dh = q.shape
    H = G * R
    qb = math.gcd(seqlen, BAND_BLOCK)
    n_prev = -(-max_dist // qb)
    pad = n_prev * qb
    span = pad + qb
    nblk = seqlen // qb
    kp = jnp.pad(k, ((0, 0), (pad, 0), (0, 0), (0, 0)))
    vp = jnp.pad(v, ((0, 0), (pad, 0), (0, 0), (0, 0)))
    dist = jnp.arange(qb)[:, None] + pad - jnp.arange(span)[None, :]
    band = (dist >= 0) & (dist <= max_dist)
    bias = t5_bias(rel_table, dist[:, None, :] * dist_scale)

    def block(args):
        i, qi = args
        ks = lax.dynamic_slice_in_dim(kp, i * qb, span, axis=1)
        vs = lax.dynamic_slice_in_dim(vp, i * qb, span, axis=1)
        logits = jnp.einsum('bqgrd,bkgd->bqgrk', qi, ks, preferred_element_type=jnp.float32).reshape(bsz, qb, H, span) * ATTN_SCALE + bias
        kpos = i * qb - pad + jnp.arange(span)
        mask = (band & (kpos >= 0)[None, :])[None, :, None, :]
        p, s, m = softmax_parts(logits, mask, sink)
        o = jnp.einsum('bqgrk,bkgd->bqgrd', p.reshape(bsz, qb, G, R, span).astype(vs.dtype), vs, preferred_element_type=jnp.float32)
        s = jnp.maximum(s, TINY)
        return o / s.reshape(bsz, qb, G, R, 1), (m + jnp.log(s))[..., 0]

    o, lse = lax.map(block, (jnp.arange(nblk), to_chunks(q, qb)))
    return from_chunks(o), from_chunks(lse)


def cmp_to_sel_matrix(nc, ns):
    a, b = NSA_SEL_LEN // NSA_CMP_STRIDE, NSA_CMP_LEN // NSA_CMP_STRIDE
    w = np.zeros((nc, ns), np.float32)
    j = np.arange(ns)
    for m in range(a):
        for n in range(b):
            i = a * j + m + n - (b - 1)
            ok = (i >= 0) & (i < nc)
            np.add.at(w, (i[ok], j[ok]), 1.0)
    return jnp.asarray(w)


def nsa_mixer(u, w_in, cmp_pos, cmp_k_w1, cmp_k_w2, cmp_v_w1, cmp_v_w2, w_out, rel_table):
    bsz, seqlen, _ = u.shape
    G, R, Dh, H = NSA_KV_HEADS, N_HEADS // NSA_KV_HEADS, HEAD_DIM, N_HEADS
    kvw = G * Dh
    q, kc, vc, ks, vs, kw, vw, gates = split_cols(u @ w_in, [ATTN_WIDTH] + [kvw] * 6 + [3 * H])
    q = q.reshape(bsz, seqlen, G, R, Dh)
    kc, vc, ks, vs, kw, vw = (t.reshape(bsz, seqlen, G, Dh) for t in (kc, vc, ks, vs, kw, vw))
    pos = jnp.arange(seqlen)

    nc = (seqlen - NSA_CMP_LEN) // NSA_CMP_STRIDE + 1
    cidx = jnp.arange(nc)[:, None] * NSA_CMP_STRIDE + jnp.arange(NSA_CMP_LEN)[None, :]

    def compress(t, w1, w2):
        blk = t[:, cidx] + cmp_pos[:, None, :]
        return jax.nn.gelu(jnp.einsum('bnlgd,lde->bnge', blk, w1)) @ w2

    kcmp = compress(kc, cmp_k_w1, cmp_k_w2)
    vcmp = compress(vc, cmp_v_w1, cmp_v_w2)
    cdist = pos[:, None] - cidx[:, -1][None, :]
    lc = jnp.einsum('bsgrd,bngd->bsgrn', q, kcmp, preferred_element_type=jnp.float32).reshape(bsz, seqlen, H, nc) * ATTN_SCALE + t5_bias(rel_table, cdist[:, None, :])
    p, s, _ = softmax_parts(lc, (cdist >= 0)[:, None, :])
    p_cmp = (p / jnp.maximum(s, TINY)).reshape(bsz, seqlen, G, R, nc)
    o_cmp = jnp.einsum('bsgrn,bngd->bsgrd', p_cmp.astype(vcmp.dtype), vcmp, preferred_element_type=jnp.float32)

    ns = seqlen // NSA_SEL_LEN
    imp = jnp.sum(p_cmp, axis=3) @ cmp_to_sel_matrix(nc, ns)
    blk = jnp.arange(ns)[None, :]
    cur = (pos // NSA_SEL_LEN)[:, None]
    valid = blk * NSA_SEL_LEN <= pos[:, None]
    forced = (blk == 0) | (blk == cur) | (blk == cur - 1)
    score = jnp.where(forced[:, None, :], FORCED_SCORE, jnp.where(valid[:, None, :], imp, NEG_INF))
    n_sel = min(NSA_SEL_TOPN, ns)
    _, sel = lax.top_k(score, n_sel)
    ks_b = jnp.moveaxis(ks.reshape(bsz, ns, NSA_SEL_LEN, G, Dh), 3, 1)
    vs_b = jnp.moveaxis(vs.reshape(bsz, ns, NSA_SEL_LEN, G, Dh), 3, 1)
    b_i = jnp.arange(bsz)[:, None, None, None]
    g_i = jnp.arange(G)[None, None, :, None]
    QC = NSA_QCHUNK

    def sel_chunk(args):
        c, qc, sc = args
        tq = c * QC + jnp.arange(QC)
        kg = ks_b[b_i, g_i, sc]
        vg = vs_b[b_i, g_i, sc]
        kpos = sc[..., None] * NSA_SEL_LEN + jnp.arange(NSA_SEL_LEN)
        dist = (tq[None, :, None, None, None] - kpos).reshape(bsz, QC, G, n_sel * NSA_SEL_LEN)
        dist = jnp.repeat(dist, R, axis=2)
        logits = jnp.einsum('bqgrd,bqgnld->bqgrnl', qc, kg, preferred_element_type=jnp.float32).reshape(bsz, QC, H, n_sel * NSA_SEL_LEN) * ATTN_SCALE + t5_bias(rel_table, dist)
        p, s, _ = softmax_parts(logits, dist >= 0)
        o = jnp.einsum('bqgrnl,bqgnld->bqgrd', p.reshape(bsz, QC, G, R, n_sel, NSA_SEL_LEN).astype(vg.dtype), vg, preferred_element_type=jnp.float32)
        return o / jnp.maximum(s, TINY).reshape(bsz, QC, G, R, 1)

    o_slc = from_chunks(lax.map(sel_chunk, (jnp.arange(seqlen // QC), to_chunks(q, QC), to_chunks(sel, QC))))

    o_win, _ = banded_attention(q, kw, vw, rel_table, NSA_WINDOW - 1, 1)

    g = jax.nn.sigmoid(gates.astype(jnp.float32)).reshape(bsz, seqlen, G, R, 3)
    o = g[..., 0:1] * o_cmp + g[..., 1:2] * o_slc + g[..., 2:3] * o_win
    return o.reshape(bsz, seqlen, ATTN_WIDTH).astype(u.dtype) @ w_out


def dilated_mixer(u, w_in, w_out, rel_table):
    bsz, seqlen, _ = u.shape
    G, R, Dh, H = DIL_KV_HEADS, N_HEADS // DIL_KV_HEADS, HEAD_DIM, N_HEADS
    cols = split_cols(u @ w_in, [ATTN_WIDTH, G * Dh, G * Dh] * len(DIL_PAIRS))
    outs, lses = [], []
    for gi, (window, dil) in enumerate(DIL_PAIRS):
        q = cols[3 * gi].reshape(bsz, seqlen, G, R, Dh)
        k = cols[3 * gi + 1].reshape(bsz, seqlen, G, Dh)
        v = cols[3 * gi + 2].reshape(bsz, seqlen, G, Dh)
        o, lse = banded_attention(to_residue(q, dil), to_residue(k, dil), to_residue(v, dil), rel_table, window // dil, dil)
        outs.append(from_residue(o, dil, bsz).reshape(bsz, seqlen, H, Dh))
        lses.append(from_residue(lse, dil, bsz))
    alpha = jax.nn.softmax(jnp.stack(lses), axis=0)
    o = jnp.einsum('nbsh,nbshd->bshd', alpha, jnp.stack(outs))
    return o.reshape(bsz, seqlen, ATTN_WIDTH).astype(u.dtype) @ w_out


def moba_mixer(u, w_in, w_out, rel_table):
    bsz, seqlen, _ = u.shape
    G, R, Dh, H = MOBA_KV_HEADS, N_HEADS // MOBA_KV_HEADS, HEAD_DIM, N_HEADS
    q, k, v = split_cols(u @ w_in, [ATTN_WIDTH, G * Dh, G * Dh])
    q = q.reshape(bsz, seqlen, G, R, Dh)
    k = k.reshape(bsz, seqlen, G, Dh)
    v = v.reshape(bsz, seqlen, G, Dh)
    BL = MOBA_BLOCK
    nblk = -(-seqlen // BL)
    padlen = nblk * BL - seqlen
    kp = jnp.pad(k, ((0, 0), (0, padlen), (0, 0), (0, 0)))
    vp = jnp.pad(v, ((0, 0), (0, padlen), (0, 0), (0, 0)))
    kb = jnp.moveaxis(kp.reshape(bsz, nblk, BL, G, Dh), 3, 1)
    vb = jnp.moveaxis(vp.reshape(bsz, nblk, BL, G, Dh), 3, 1)
    kmean = jnp.mean(kb.astype(jnp.float32), axis=3)
    gate = jnp.einsum('bsgrd,bgnd->bsgrn', q, kmean.astype(q.dtype), preferred_element_type=jnp.float32).reshape(bsz, seqlen, H, nblk)
    pos = jnp.arange(seqlen)
    past = jnp.arange(nblk)[None, :] < (pos // BL)[:, None]
    n_sel = min(MOBA_TOPK, max(nblk - 1, 1))
    _, sel = lax.top_k(jnp.where(past[:, None, :], gate, NEG_INF), n_sel)
    b_i = jnp.arange(bsz)[:, None, None, None]
    g_i = (jnp.arange(H) // R)[None, None, :, None]
    QC = MOBA_QCHUNK

    def chunk(args):
        c, qc, sc = args
        tq = c * QC + jnp.arange(QC)
        own = (c * QC) // BL
        ko = lax.dynamic_slice_in_dim(kp, own * BL, BL, axis=1)
        vo = lax.dynamic_slice_in_dim(vp, own * BL, BL, axis=1)
        odist = tq[:, None] - (own * BL + jnp.arange(BL))[None, :]
        lo = jnp.einsum('bqgrd,bkgd->bqgrk', qc, ko, preferred_element_type=jnp.float32).reshape(bsz, QC, H, BL) * ATTN_SCALE + t5_bias(rel_table, odist[:, None, :])
        mo = jnp.broadcast_to((odist >= 0)[None, :, None, :], lo.shape)
        kg = kb[b_i, g_i, sc]
        vg = vb[b_i, g_i, sc]
        sdist = (tq[None, :, None, None, None] - (sc[..., None] * BL + jnp.arange(BL))).reshape(bsz, QC, H, n_sel * BL)
        ls = jnp.einsum('bqhd,bqhnld->bqhnl', qc.reshape(bsz, QC, H, Dh), kg, preferred_element_type=jnp.float32).reshape(bsz, QC, H, n_sel * BL) * ATTN_SCALE + t5_bias(rel_table, sdist)
        ms = jnp.broadcast_to((sc < own)[..., None], sc.shape + (BL,)).reshape(bsz, QC, H, n_sel * BL)
        p, s, _ = softmax_parts(jnp.concatenate([ls, lo], axis=-1), jnp.concatenate([ms, mo], axis=-1))
        ps = p[..., :n_sel * BL].reshape(bsz, QC, H, n_sel, BL).astype(vg.dtype)
        po = p[..., n_sel * BL:].reshape(bsz, QC, G, R, BL).astype(vo.dtype)
        o = jnp.einsum('bqhnl,bqhnld->bqhd', ps, vg, preferred_element_type=jnp.float32) + jnp.einsum('bqgrk,bkgd->bqgrd', po, vo, preferred_element_type=jnp.float32).reshape(bsz, QC, H, Dh)
        return o / jnp.maximum(s, TINY)

    o = from_chunks(lax.map(chunk, (jnp.arange(seqlen // QC), to_chunks(q, QC), to_chunks(sel, QC))))
    return o.reshape(bsz, seqlen, ATTN_WIDTH).astype(u.dtype) @ w_out


def swa_sink_mixer(u, w_in, sinks, w_out, rel_table):
    bsz, seqlen, _ = u.shape
    G, R, Dh = SWA_KV_HEADS, N_HEADS // SWA_KV_HEADS, HEAD_DIM
    q, k, v = split_cols(u @ w_in, [ATTN_WIDTH, G * Dh, G * Dh])
    o, _ = banded_attention(q.reshape(bsz, seqlen, G, R, Dh), k.reshape(bsz, seqlen, G, Dh), v.reshape(bsz, seqlen, G, Dh), rel_table, SWA_WINDOW - 1, 1, sinks)
    return o.reshape(bsz, seqlen, ATTN_WIDTH).astype(u.dtype) @ w_out


def setup_inputs(seed: int = 0) -> dict:
    key = jax.random.key(seed)
    keys = iter(jax.random.split(key, 32))

    def nrm(shape, scale):
        return jax.random.normal(next(keys), shape, jnp.float32) * scale

    def gain(shape):
        return 1.0 + nrm(shape, 0.02)

    nA, nB, nC, nD = (_uses(m) for m in range(N_MIXERS))
    sd, sf, sa = D_MODEL ** -0.5, D_FF ** -0.5, ATTN_WIDTH ** -0.5
    return {
        'x': nrm((BATCH, SEQ, D_MODEL), 1.0),
        'rel_table': nrm((REL_BUCKETS, N_HEADS), 0.5),
        'ffn1_norm': gain((DEPTH, D_MODEL)),
        'ffn1_w_gate': nrm((DEPTH, D_MODEL, D_FF), sd),
        'ffn1_w_up': nrm((DEPTH, D_MODEL, D_FF), sd),
        'ffn1_w_down': nrm((DEPTH, D_FF, D_MODEL), sf),
        'mix_norm': gain((DEPTH, D_MODEL)),
        'ffn2_norm': gain((DEPTH, D_MODEL)),
        'ffn2_w_gate': nrm((DEPTH, D_MODEL, D_FF), sd),
        'ffn2_w_up': nrm((DEPTH, D_MODEL, D_FF), sd),
        'ffn2_w_down': nrm((DEPTH, D_FF, D_MODEL), sf),
        'final_norm': gain((D_MODEL,)),
        'nsa_w_in': nrm((nA, D_MODEL, NSA_IN_COLS), sd),
        'nsa_cmp_pos': nrm((nA, NSA_CMP_LEN, HEAD_DIM), 0.1),
        'nsa_cmp_k_w1': nrm((nA, NSA_CMP_LEN, HEAD_DIM, HEAD_DIM), (NSA_CMP_LEN * HEAD_DIM) ** -0.5),
        'nsa_cmp_k_w2': nrm((nA, HEAD_DIM, HEAD_DIM), HEAD_DIM ** -0.5),
        'nsa_cmp_v_w1': nrm((nA, NSA_CMP_LEN, HEAD_DIM, HEAD_DIM), (NSA_CMP_LEN * HEAD_DIM) ** -0.5),
        'nsa_cmp_v_w2': nrm((nA, HEAD_DIM, HEAD_DIM), HEAD_DIM ** -0.5),
        'nsa_w_out': nrm((nA, ATTN_WIDTH, D_MODEL), sa),
        'dil_w_in': nrm((nB, D_MODEL, DIL_IN_COLS), sd),
        'dil_w_out': nrm((nB, ATTN_WIDTH, D_MODEL), sa),
        'moba_w_in': nrm((nC, D_MODEL, MOBA_IN_COLS), sd),
        'moba_w_out': nrm((nC, ATTN_WIDTH, D_MODEL), sa),
        'swa_w_in': nrm((nD, D_MODEL, SWA_IN_COLS), sd),
        'swa_sinks': nrm((nD, N_HEADS), 0.5),
        'swa_w_out': nrm((nD, ATTN_WIDTH, D_MODEL), sa),
    }


def reference(x, rel_table, ffn1_norm, ffn1_w_gate, ffn1_w_up, ffn1_w_down, mix_norm, ffn2_norm, ffn2_w_gate, ffn2_w_up, ffn2_w_down, final_norm, nsa_w_in, nsa_cmp_pos, nsa_cmp_k_w1, nsa_cmp_k_w2, nsa_cmp_v_w1, nsa_cmp_v_w2, nsa_w_out, dil_w_in, dil_w_out, moba_w_in, moba_w_out, swa_w_in, swa_sinks, swa_w_out):
    h = x
    for i in range(DEPTH):
        h = h + 0.5 * swiglu(rms_norm(h, ffn1_norm[i]), ffn1_w_gate[i], ffn1_w_up[i], ffn1_w_down[i])
        u = rms_norm(h, mix_norm[i])
        m, j = i % N_MIXERS, i // N_MIXERS
        if m == 0:
            y = nsa_mixer(u, nsa_w_in[j], nsa_cmp_pos[j], nsa_cmp_k_w1[j], nsa_cmp_k_w2[j], nsa_cmp_v_w1[j], nsa_cmp_v_w2[j], nsa_w_out[j], rel_table)
        elif m == 1:
            y = dilated_mixer(u, dil_w_in[j], dil_w_out[j], rel_table)
        elif m == 2:
            y = moba_mixer(u, moba_w_in[j], moba_w_out[j], rel_table)
        else:
            y = swa_sink_mixer(u, swa_w_in[j], swa_sinks[j], swa_w_out[j], rel_table)
        h = h + y
        h = h + 0.5 * swiglu(rms_norm(h, ffn2_norm[i]), ffn2_w_gate[i], ffn2_w_up[i], ffn2_w_down[i])
    return rms_norm(h, final_norm)
```

```python
import functools
import math

import numpy as np
import jax
import jax.numpy as jnp
from jax import lax
from jax.experimental import pallas as pl
from jax.experimental.pallas import tpu as pltpu

HEAD_DIM = 128
N_HEADS = 32
ATTN_WIDTH = N_HEADS * HEAD_DIM
RMS_EPS = 1e-6
REL_BUCKETS = 32
REL_MAX_DIST = 2048
BAND_BLOCK = 128
NSA_KV_HEADS = 4
NSA_CMP_LEN = 32
NSA_CMP_STRIDE = 16
NSA_SEL_LEN = 64
NSA_SEL_TOPN = 16
NSA_WINDOW = 512
DIL_PAIRS = ((128, 1), (512, 4), (2048, 16))
DIL_KV_HEADS = 8
MOBA_BLOCK = 256
MOBA_TOPK = 3
MOBA_KV_HEADS = 8
SWA_WINDOW = 128
SWA_KV_HEADS = 4
ATTN_SCALE = HEAD_DIM ** -0.5
NEG_INF = -1e30
MASKED_BELOW = -5e29
TINY = 1e-20
FORCED_SCORE = 1e9

LANES = 128
SEL_TILE = 256
VMEM_LIMIT = 56 * 1024 * 1024

F32 = jnp.float32
BF16 = jnp.bfloat16


def _params(semantics):
    return pltpu.CompilerParams(dimension_semantics=semantics, vmem_limit_bytes=VMEM_LIMIT)


def _dot_nt(a, b):
    return lax.dot_general(a, b, (((1,), (1,)), ((), ())), preferred_element_type=F32)


def _stack_heads(q, n_heads):
    return jnp.concatenate([q[:, r * HEAD_DIM:(r + 1) * HEAD_DIM] for r in range(n_heads)], axis=0)


def _rms_kernel(x_ref, g_ref, o_ref):
    x = x_ref[...]
    y = x * lax.rsqrt(jnp.mean(x * x, axis=-1, keepdims=True) + RMS_EPS)
    o_ref[...] = (y * g_ref[...]).astype(o_ref.dtype)


def rms_norm(x, gain, out_dtype):
    m, d = x.shape
    tm = 256
    return pl.pallas_call(
        _rms_kernel,
        grid=(m // tm,),
        in_specs=[pl.BlockSpec((tm, d), lambda i: (i, 0)), pl.BlockSpec((1, d), lambda i: (0, 0))],
        out_specs=pl.BlockSpec((tm, d), lambda i: (i, 0)),
        out_shape=jax.ShapeDtypeStruct((m, d), out_dtype),
        compiler_params=_params(("parallel",)),
        name="rms_norm",
    )(x, gain.reshape(1, d))


def _mm_kernel(a_ref, b_ref, o_ref, acc_ref):
    k = pl.program_id(2)
    prod = jnp.dot(a_ref[...], b_ref[...], preferred_element_type=F32)

    @pl.when(k == 0)
    def _():
        acc_ref[...] = prod

    @pl.when(k > 0)
    def _():
        acc_ref[...] += prod

    @pl.when(k == pl.num_programs(2) - 1)
    def _():
        o_ref[...] = acc_ref[...].astype(o_ref.dtype)


def _mm_res_kernel(a_ref, b_ref, r_ref, o_ref, acc_ref, *, scale):
    k = pl.program_id(2)
    prod = jnp.dot(a_ref[...], b_ref[...], preferred_element_type=F32)

    @pl.when(k == 0)
    def _():
        acc_ref[...] = prod

    @pl.when(k > 0)
    def _():
        acc_ref[...] += prod

    @pl.when(k == pl.num_programs(2) - 1)
    def _():
        o_ref[...] = r_ref[...] + scale * acc_ref[...]


def _mm_swiglu_kernel(a_ref, bg_ref, bu_ref, o_ref, accg_ref, accu_ref):
    k = pl.program_id(2)
    a = a_ref[...]
    pg = jnp.dot(a, bg_ref[...], preferred_element_type=F32)
    pu = jnp.dot(a, bu_ref[...], preferred_element_type=F32)

    @pl.when(k == 0)
    def _():
        accg_ref[...] = pg
        accu_ref[...] = pu

    @pl.when(k > 0)
    def _():
        accg_ref[...] += pg
        accu_ref[...] += pu

    @pl.when(k == pl.num_programs(2) - 1)
    def _():
        o_ref[...] = (jax.nn.silu(accg_ref[...]) * accu_ref[...]).astype(o_ref.dtype)


def _mm_tiles(m, n, k):
    tm = min(m, 1024)
    tn = 1024 if n % 1024 == 0 else n
    tk = 1024 if k % 1024 == 0 else k
    return tm, tn, tk


def matmul(a, b, out_dtype):
    m, k = a.shape
    n = b.shape[1]
    tm, tn, tk = _mm_tiles(m, n, k)
    return pl.pallas_call(
        _mm_kernel,
        grid=(m // tm, n // tn, k // tk),
        in_specs=[pl.BlockSpec((tm, tk), lambda i, j, kk: (i, kk)),
                  pl.BlockSpec((tk, tn), lambda i, j, kk: (kk, j))],
        out_specs=pl.BlockSpec((tm, tn), lambda i, j, kk: (i, j)),
        out_shape=jax.ShapeDtypeStruct((m, n), out_dtype),
        scratch_shapes=[pltpu.VMEM((tm, tn), F32)],
        compiler_params=_params(("parallel", "parallel", "arbitrary")),
        name="matmul",
    )(a, b)


def matmul_residual(a, b, res, scale):
    m, k = a.shape
    n = b.shape[1]
    tm, tn, tk = _mm_tiles(m, n, k)
    return pl.pallas_call(
        functools.partial(_mm_res_kernel, scale=scale),
        grid=(m // tm, n // tn, k // tk),
        in_specs=[pl.BlockSpec((tm, tk), lambda i, j, kk: (i, kk)),
                  pl.BlockSpec((tk, tn), lambda i, j, kk: (kk, j)),
                  pl.BlockSpec((tm, tn), lambda i, j, kk: (i, j))],
        out_specs=pl.BlockSpec((tm, tn), lambda i, j, kk: (i, j)),
        out_shape=jax.ShapeDtypeStruct((m, n), F32),
        scratch_shapes=[pltpu.VMEM((tm, tn), F32)],
        compiler_params=_params(("parallel", "parallel", "arbitrary")),
        name="matmul_residual",
    )(a, b, res)


def matmul_swiglu(a, bg, bu):
    m, k = a.shape
    n = bg.shape[1]
    tm, tn, tk = _mm_tiles(m, n, k)
    return pl.pallas_call(
        _mm_swiglu_kernel,
        grid=(m // tm, n // tn, k // tk),
        in_specs=[pl.BlockSpec((tm, tk), lambda i, j, kk: (i, kk)),
                  pl.BlockSpec((tk, tn), lambda i, j, kk: (kk, j)),
                  pl.BlockSpec((tk, tn), lambda i, j, kk: (kk, j))],
        out_specs=pl.BlockSpec((tm, tn), lambda i, j, kk: (i, j)),
        out_shape=jax.ShapeDtypeStruct((m, n), BF16),
        scratch_shapes=[pltpu.VMEM((tm, tn), F32), pltpu.VMEM((tm, tn), F32)],
        compiler_params=_params(("parallel", "parallel", "arbitrary")),
        name="matmul_swiglu",
    )(a, bg, bu)


def ffn_half_step(h, gain, w_gate, w_up, w_down):
    u = rms_norm(h, gain, BF16)
    act = matmul_swiglu(u, w_gate.astype(BF16), w_up.astype(BF16))
    return matmul_residual(act, w_down.astype(BF16), h, 0.5)


def _t5_bias_values(dist, tab_ref, h):
    n = jnp.maximum(dist, 0)
    exact = REL_BUCKETS // 2
    nf = jnp.maximum(n, 1).astype(F32)
    large = exact + (jnp.log(nf / exact) * ((REL_BUCKETS - exact) / math.log(REL_MAX_DIST / exact))).astype(jnp.int32)
    bucket = jnp.where(n < exact, n, jnp.minimum(large, REL_BUCKETS - 1))
    out = jnp.zeros(dist.shape, F32)
    for b in range(REL_BUCKETS):
        out = jnp.where(bucket == b, tab_ref[b, h], out)
    return out


def _bias_table_kernel(tab_ref, o_ref, *, head_axis, blk_axis, base0, base_step, col_stride,
                       max_dist, n_valid_cols, dist_scale):
    h = pl.program_id(head_axis)
    blk = pl.program_id(blk_axis)
    rows, cols = o_ref.shape[-2:]
    row = lax.broadcasted_iota(jnp.int32, (rows, cols), 0)
    col = lax.broadcasted_iota(jnp.int32, (rows, cols), 1)
    dist = base0 + blk * base_step + row - col * col_stride
    valid = (dist >= 0) & (dist <= max_dist) & (col < n_valid_cols)
    bias = _t5_bias_values(dist * dist_scale, tab_ref, h)
    o_ref[...] = jnp.where(valid, bias, NEG_INF).reshape(o_ref.shape)


def band_bias_table(rel_table, tq, span, pad, max_dist, dist_scale):
    kern = functools.partial(_bias_table_kernel, head_axis=0, blk_axis=1, base0=pad, base_step=0,
                             col_stride=1, max_dist=max_dist, n_valid_cols=span, dist_scale=dist_scale)
    return pl.pallas_call(
        kern,
        grid=(N_HEADS, 1),
        in_specs=[pl.BlockSpec(memory_space=pltpu.SMEM)],
        out_specs=pl.BlockSpec((tq, span), lambda h, j: (h, 0)),
        out_shape=jax.ShapeDtypeStruct((N_HEADS * tq, span), F32),
        compiler_params=_params(("parallel", "arbitrary")),
        name="band_bias_table",
    )(rel_table)


def causal_bias_table(rel_table, tile, n_cls):
    kern = functools.partial(_bias_table_kernel, head_axis=1, blk_axis=0, base0=0, base_step=tile,
                             col_stride=1, max_dist=2 ** 30, n_valid_cols=tile, dist_scale=1)
    return pl.pallas_call(
        kern,
        grid=(n_cls, N_HEADS),
        in_specs=[pl.BlockSpec(memory_space=pltpu.SMEM)],
        out_specs=pl.BlockSpec((1, tile, tile), lambda c, h: (c, h, 0)),
        out_shape=jax.ShapeDtypeStruct((n_cls, N_HEADS * tile, tile), F32),
        compiler_params=_params(("parallel", "arbitrary")),
        name="causal_bias_table",
    )(rel_table)


def cmp_bias_table(rel_table, seqlen, n_cmp):
    tq = LANES
    kern = functools.partial(_bias_table_kernel, head_axis=0, blk_axis=1,
                             base0=-(NSA_CMP_LEN - 1), base_step=tq, col_stride=NSA_CMP_STRIDE,
                             max_dist=2 ** 30, n_valid_cols=n_cmp, dist_scale=1)
    return pl.pallas_call(
        kern,
        grid=(N_HEADS, seqlen // tq),
        in_specs=[pl.BlockSpec(memory_space=pltpu.SMEM)],
        out_specs=pl.BlockSpec((1, 1, tq, LANES), lambda h, i: (h, i, 0, 0)),
        out_shape=jax.ShapeDtypeStruct((N_HEADS, seqlen // tq, tq, LANES), F32),
        compiler_params=_params(("parallel", "arbitrary")),
        name="cmp_bias_table",
    )(rel_table)


def _banded_kernel(*refs, n_rep, tq, n_prev, seq, has_sink, want_lse):
    q_ref, k_ref, v_ref, bias_ref = refs[:4]
    pos = 4
    sink_ref = None
    if has_sink:
        sink_ref = refs[pos]
        pos += 1
    o_ref = refs[pos]
    pos += 1
    lse_ref = None
    if want_lse:
        lse_ref = refs[pos]
        pos += 1
    kpad_ref, vpad_ref = refs[pos:pos + 2]

    g = pl.program_id(2)
    i = pl.program_id(3)
    pad = n_prev * tq
    span = pad + tq

    @pl.when(i == 0)
    def _():
        if pad:
            kpad_ref[0:pad, :] = jnp.zeros((pad, HEAD_DIM), BF16)
            vpad_ref[0:pad, :] = jnp.zeros((pad, HEAD_DIM), BF16)
        kpad_ref[pad:pad + seq, :] = k_ref[0]
        vpad_ref[pad:pad + seq, :] = v_ref[0]

    start = pl.multiple_of(i * tq, tq)
    ks = kpad_ref[pl.ds(start, span), :]
    vs = vpad_ref[pl.ds(start, span), :]
    qs = _stack_heads(q_ref[0], n_rep)
    logits = _dot_nt(qs, ks) * ATTN_SCALE + bias_ref[...]
    col = lax.broadcasted_iota(jnp.int32, (1, span), 1)
    logits = jnp.where(col >= pad - i * tq, logits, NEG_INF)
    m = jnp.max(logits, axis=-1, keepdims=True)
    sink = None
    if has_sink:
        sink = jnp.concatenate(
            [jnp.full((tq, 1), sink_ref[g * n_rep + r], F32) for r in range(n_rep)], axis=0)
        m = jnp.maximum(m, sink)
    p = jnp.where(logits > MASKED_BELOW, jnp.exp(logits - m), 0.0)
    s = jnp.sum(p, axis=-1, keepdims=True)
    if has_sink:
        s = s + jnp.exp(sink - m)
    o = jnp.dot(p.astype(BF16), vs, preferred_element_type=F32)
    s = jnp.maximum(s, TINY)
    o = o / s
    for r in range(n_rep):
        o_ref[0, :, r * HEAD_DIM:(r + 1) * HEAD_DIM] = o[r * tq:(r + 1) * tq].astype(o_ref.dtype)
    if want_lse:
        lse = m + jnp.log(s)
        lane = lax.broadcasted_iota(jnp.int32, (tq, LANES), 1)
        tile = jnp.zeros((tq, LANES), F32)
        for r in range(n_rep):
            tile = jnp.where(lane == r, lse[r * tq:(r + 1) * tq], tile)
        lse_ref[0, 0] = tile


def banded_attention(q_arr, q_off, kv_arr, k_off, v_off, bias, *, batch, seqlen, n_kv, dil,
                     max_dist, sinks=None, want_lse=False, out_dtype=F32):
    n_rep = N_HEADS // n_kv
    sub = seqlen // dil
    tq = math.gcd(sub, BAND_BLOCK)
    n_blk = sub // tq
    n_prev = min(-(-max_dist // tq), n_blk - 1)
    span = (n_prev + 1) * tq
    cq = q_arr.shape[1]
    ckv = kv_arr.shape[1]
    qw = n_rep * HEAD_DIM
    q3 = q_arr.reshape(batch, sub, dil * cq)
    kv3 = kv_arr.reshape(batch, sub, dil * ckv)
    assert cq % qw == 0 and q_off % qw == 0 and bias.shape == (N_HEADS * tq, span)

    in_specs = [
        pl.BlockSpec((1, tq, qw), lambda b, rho, g, i: (b, i, (rho * cq + q_off) // qw + g)),
        pl.BlockSpec((1, sub, HEAD_DIM), lambda b, rho, g, i: (b, 0, (rho * ckv + k_off) // HEAD_DIM + g)),
        pl.BlockSpec((1, sub, HEAD_DIM), lambda b, rho, g, i: (b, 0, (rho * ckv + v_off) // HEAD_DIM + g)),
        pl.BlockSpec((n_rep * tq, span), lambda b, rho, g, i: (g, 0)),
    ]
    args = [q3, kv3, kv3, bias]
    if sinks is not None:
        in_specs.append(pl.BlockSpec(memory_space=pltpu.SMEM))
        args.append(sinks)
    out_specs = [pl.BlockSpec((1, tq, qw), lambda b, rho, g, i: (b, i, rho * (ATTN_WIDTH // qw) + g))]
    out_shape = [jax.ShapeDtypeStruct((batch, sub, dil * ATTN_WIDTH), out_dtype)]
    if want_lse:
        out_specs.append(pl.BlockSpec((1, 1, tq, LANES), lambda b, rho, g, i: (b, g, i, rho)))
        out_shape.append(jax.ShapeDtypeStruct((batch, n_kv, sub, dil * LANES), F32))
    kern = functools.partial(_banded_kernel, n_rep=n_rep, tq=tq, n_prev=n_prev, seq=sub,
                             has_sink=sinks is not None, want_lse=want_lse)
    outs = pl.pallas_call(
        kern,
        grid=(batch, dil, n_kv, n_blk),
        in_specs=in_specs,
        out_specs=out_specs,
        out_shape=out_shape,
        scratch_shapes=[pltpu.VMEM((n_prev * tq + sub, HEAD_DIM), BF16),
                        pltpu.VMEM((n_prev * tq + sub, HEAD_DIM), BF16)],
        compiler_params=_params(("parallel", "parallel", "parallel", "arbitrary")),
        name="banded_attention",
    )(*args)
    o = outs[0].reshape(batch * seqlen, ATTN_WIDTH)
    if want_lse:
        return o, outs[1].reshape(batch, n_kv, seqlen, LANES)
    return o


def _flash_init(m_ref, s_ref, acc_ref):
    m_ref[...] = jnp.full(m_ref.shape, NEG_INF, F32)
    s_ref[...] = jnp.zeros(s_ref.shape, F32)
    acc_ref[...] = jnp.zeros(acc_ref.shape, F32)


def _flash_step(logits, v_blk, m_ref, s_ref, acc_ref):
    m_prev = m_ref[:, 0:1]
    m_new = jnp.maximum(m_prev, jnp.max(logits, axis=-1, keepdims=True))
    alpha = jnp.exp(m_prev - m_new)
    p = jnp.where(logits > MASKED_BELOW, jnp.exp(logits - m_new), 0.0)
    s_new = alpha * s_ref[:, 0:1] + jnp.sum(p, axis=-1, keepdims=True)
    acc_ref[...] = alpha * acc_ref[...] + jnp.dot(p.astype(BF16), v_blk, preferred_element_type=F32)
    m_ref[...] = jnp.broadcast_to(m_new, m_ref.shape)
    s_ref[...] = jnp.broadcast_to(s_new, s_ref.shape)


def _flash_finish(o_ref, s_ref, acc_ref, n_rep, tq):
    o = acc_ref[...] / jnp.maximum(s_ref[:, 0:1], TINY)
    for r in range(n_rep):
        o_ref[0, :, r * HEAD_DIM:(r + 1) * HEAD_DIM] = o[r * tq:(r + 1) * tq].astype(o_ref.dtype)


def _first_rank(score, n_cand):
    lane = lax.broadcasted_iota(jnp.int32, score.shape, 1)
    rank = jnp.zeros(score.shape, F32)
    for jp in range(n_cand):
        other = score[:, jp:jp + 1]
        ahead = jnp.where(other > score, 1.0, jnp.where(other == score, jnp.where(lane > jp, 1.0, 0.0), 0.0))
        rank = rank + ahead
    return rank


def _nsa_cmp_kernel(x_ref, pos_ref, w1k_ref, w2k_ref, w1v_ref, w2v_ref, ko_ref, vo_ref):
    width = 2 * NSA_KV_HEADS * HEAD_DIM
    for kv, (w1_ref, w2_ref, o_ref) in enumerate(((w1k_ref, w2k_ref, ko_ref), (w1v_ref, w2v_ref, vo_ref))):
        for g in range(NSA_KV_HEADS):
            off = kv * NSA_KV_HEADS * HEAD_DIM + g * HEAD_DIM
            chunk = jnp.concatenate(
                [x_ref[0, :, l * width + off:l * width + off + HEAD_DIM] for l in range(NSA_CMP_STRIDE)], axis=1)
            first = jnp.dot((chunk + pos_ref[0:1, :]).astype(BF16), w1_ref[0], preferred_element_type=F32)
            second = jnp.dot((chunk + pos_ref[1:2, :]).astype(BF16), w1_ref[1], preferred_element_type=F32)
            hidden = jax.nn.gelu(first + pltpu.roll(second, second.shape[0] - 1, axis=0))
            o_ref[0, g] = jnp.dot(hidden.astype(BF16), w2_ref[...], preferred_element_type=F32).astype(o_ref.dtype)


def nsa_compress(kcvc, cmp_pos, k_w1, k_w2, v_w1, v_w2, batch, seqlen):
    n_chunk = seqlen // NSA_CMP_STRIDE
    width = 2 * NSA_KV_HEADS * HEAD_DIM
    half = NSA_CMP_STRIDE * HEAD_DIM
    x = kcvc.reshape(batch, n_chunk, NSA_CMP_STRIDE * width)
    out = jax.ShapeDtypeStruct((batch, NSA_KV_HEADS, n_chunk, HEAD_DIM), BF16)
    full = lambda shape: pl.BlockSpec(shape, lambda b: (0,) * len(shape))
    return pl.pallas_call(
        _nsa_cmp_kernel,
        grid=(batch,),
        in_specs=[pl.BlockSpec((1, n_chunk, NSA_CMP_STRIDE * width), lambda b: (b, 0, 0)),
                  full((2, half)), full((2, half, HEAD_DIM)), full((HEAD_DIM, HEAD_DIM)),
                  full((2, half, HEAD_DIM)), full((HEAD_DIM, HEAD_DIM))],
        out_specs=[pl.BlockSpec((1, NSA_KV_HEADS, n_chunk, HEAD_DIM), lambda b: (b, 0, 0, 0))] * 2,
        out_shape=[out, out],
        compiler_params=_params(("parallel",)),
        name="nsa_compress",
    )(x, cmp_pos.reshape(2, half), k_w1.reshape(2, half, HEAD_DIM).astype(BF16), k_w2.astype(BF16),
      v_w1.reshape(2, half, HEAD_DIM).astype(BF16), v_w2.astype(BF16))


def _nsa_cmp_attn_kernel(q_ref, kc_ref, vc_ref, bias_ref, c2s_ref, o_ref, sel_ref, *, n_rep, tq, n_sel_blk):
    i = pl.program_id(2)
    qs = _stack_heads(q_ref[0], n_rep)
    bias = bias_ref[:, 0].reshape(n_rep * tq, LANES)
    logits = _dot_nt(qs, kc_ref[0, 0]) * ATTN_SCALE + bias
    valid = bias > MASKED_BELOW
    logits = jnp.where(valid, logits, NEG_INF)
    m = jnp.max(logits, axis=-1, keepdims=True)
    p = jnp.where(valid, jnp.exp(logits - m), 0.0)
    s = jnp.sum(p, axis=-1, keepdims=True)
    p_cmp = p / jnp.maximum(s, TINY)
    o = jnp.dot(p_cmp.astype(BF16), vc_ref[0, 0], preferred_element_type=F32)
    for r in range(n_rep):
        o_ref[0, :, r * HEAD_DIM:(r + 1) * HEAD_DIM] = o[r * tq:(r + 1) * tq]

    p_sum = p_cmp[0:tq]
    for r in range(1, n_rep):
        p_sum = p_sum + p_cmp[r * tq:(r + 1) * tq]
    imp = jnp.dot(p_sum.astype(BF16), c2s_ref[...], preferred_element_type=F32)
    blk = lax.broadcasted_iota(jnp.int32, (tq, LANES), 1)
    tpos = i * tq + lax.broadcasted_iota(jnp.int32, (tq, LANES), 0)
    cur = tpos // NSA_SEL_LEN
    forced = (blk == 0) | (blk == cur) | (blk == cur - 1)
    score = jnp.where(forced, FORCED_SCORE, jnp.where(blk * NSA_SEL_LEN <= tpos, imp, NEG_INF))
    rank = _first_rank(score, n_sel_blk)
    chosen = (rank < min(NSA_SEL_TOPN, n_sel_blk)) & (blk < n_sel_blk)
    sel_ref[0, 0] = jnp.where(chosen, 1.0, 0.0).astype(sel_ref.dtype)


def nsa_cmp_attention(q, kcmp, vcmp, bias, batch, seqlen):
    n_rep = N_HEADS // NSA_KV_HEADS
    tq = LANES
    qw = n_rep * HEAD_DIM
    n_sel_blk = seqlen // NSA_SEL_LEN
    n_cmp = (seqlen - NSA_CMP_LEN) // NSA_CMP_STRIDE + 1
    a, b = NSA_SEL_LEN // NSA_CMP_STRIDE, NSA_CMP_LEN // NSA_CMP_STRIDE
    w = np.zeros((LANES, LANES), np.float32)
    j = np.arange(n_sel_blk)
    for mm in range(a):
        for nn in range(b):
            ii = a * j + mm + nn - (b - 1)
            ok = (ii >= 0) & (ii < n_cmp)
            np.add.at(w, (ii[ok], j[ok]), 1.0)
    kern = functools.partial(_nsa_cmp_attn_kernel, n_rep=n_rep, tq=tq, n_sel_blk=n_sel_blk)
    return pl.pallas_call(
        kern,
        grid=(batch, NSA_KV_HEADS, seqlen // tq),
        in_specs=[pl.BlockSpec((1, tq, qw), lambda b_, g, i: (b_, i, g)),
                  pl.BlockSpec((1, 1, LANES, HEAD_DIM), lambda b_, g, i: (b_, g, 0, 0)),
                  pl.BlockSpec((1, 1, LANES, HEAD_DIM), lambda b_, g, i: (b_, g, 0, 0)),
                  pl.BlockSpec((n_rep, 1, tq, LANES), lambda b_, g, i: (g, i, 0, 0)),
                  pl.BlockSpec((LANES, LANES), lambda b_, g, i: (0, 0))],
        out_specs=[pl.BlockSpec((1, tq, qw), lambda b_, g, i: (b_, i, g)),
                   pl.BlockSpec((1, 1, tq, LANES), lambda b_, g, i: (b_, g, i, 0))],
        out_shape=[jax.ShapeDtypeStruct((batch, seqlen, ATTN_WIDTH), F32),
                   jax.ShapeDtypeStruct((batch, NSA_KV_HEADS, seqlen, LANES), BF16)],
        compiler_params=_params(("parallel", "parallel", "arbitrary")),
        name="nsa_cmp_attention",
    )(q.reshape(batch, seqlen, ATTN_WIDTH), kcmp, vcmp, bias, jnp.asarray(w, BF16))


def _nsa_sel_kernel(q_ref, k_ref, v_ref, sel_ref, bias_ref, exp_ref, o_ref, m_ref, s_ref, acc_ref, *, n_rep, tq):
    i = pl.program_id(2)
    qs = _stack_heads(q_ref[0], n_rep)
    sel = sel_ref[0, 0]
    _flash_init(m_ref, s_ref, acc_ref)

    def body(cc, carry):
        c = i - cc
        start = pl.multiple_of(c * tq, tq)
        kc = k_ref[0, pl.ds(start, tq), :]
        vc = v_ref[0, pl.ds(start, tq), :]
        logits = _dot_nt(qs, kc) * ATTN_SCALE + bias_ref[cc]
        key_on = jnp.dot(sel, exp_ref[c], preferred_element_type=F32)
        key_on = jnp.concatenate([key_on] * n_rep, axis=0)
        logits = jnp.where(key_on > 0.5, logits, NEG_INF)
        _flash_step(logits, vc, m_ref, s_ref, acc_ref)
        return carry

    lax.fori_loop(0, i + 1, body, 0)
    _flash_finish(o_ref, s_ref, acc_ref, n_rep, tq)


def nsa_selected_attention(q, kv, k_off, v_off, sel, causal_bias, batch, seqlen):
    n_rep = N_HEADS // NSA_KV_HEADS
    tq = SEL_TILE
    qw = n_rep * HEAD_DIM
    n_blk = seqlen // tq
    per = tq // NSA_SEL_LEN
    expand = np.zeros((n_blk, LANES, tq), np.float32)
    for c in range(n_blk):
        for kk in range(tq):
            expand[c, per * c + kk // NSA_SEL_LEN, kk] = 1.0
    ckv = kv.shape[1]
    kv3 = kv.reshape(batch, seqlen, ckv)
    kern = functools.partial(_nsa_sel_kernel, n_rep=n_rep, tq=tq)
    rows = n_rep * tq
    return pl.pallas_call(
        kern,
        grid=(batch, NSA_KV_HEADS, n_blk),
        in_specs=[pl.BlockSpec((1, tq, qw), lambda b, g, i: (b, i, g)),
                  pl.BlockSpec((1, seqlen, HEAD_DIM), lambda b, g, i: (b, 0, k_off // HEAD_DIM + g)),
                  pl.BlockSpec((1, seqlen, HEAD_DIM), lambda b, g, i: (b, 0, v_off // HEAD_DIM + g)),
                  pl.BlockSpec((1, 1, tq, LANES), lambda b, g, i: (b, g, i, 0)),
                  pl.BlockSpec((n_blk, rows, tq), lambda b, g, i: (0, g, 0), pipeline_mode=pl.Buffered(1)),
                  pl.BlockSpec((n_blk, LANES, tq), lambda b, g, i: (0, 0, 0), pipeline_mode=pl.Buffered(1))],
        out_specs=pl.BlockSpec((1, tq, qw), lambda b, g, i: (b, i, g)),
        out_shape=jax.ShapeDtypeStruct((batch, seqlen, ATTN_WIDTH), F32),
        scratch_shapes=[pltpu.VMEM((rows, LANES), F32), pltpu.VMEM((rows, LANES), F32),
                        pltpu.VMEM((rows, HEAD_DIM), F32)],
        compiler_params=_params(("parallel", "parallel", "arbitrary")),
        name="nsa_selected_attention",
    )(q.reshape(batch, seqlen, ATTN_WIDTH), kv3, kv3, sel, causal_bias, jnp.asarray(expand, BF16))


def _nsa_gate_kernel(g_ref, oc_ref, os_ref, ow_ref, o_ref):
    gate = jax.nn.sigmoid(g_ref[...])
    for h in range(N_HEADS):
        cols = slice(h * HEAD_DIM, (h + 1) * HEAD_DIM)
        mix = (gate[:, 3 * h:3 * h + 1] * oc_ref[:, cols] + gate[:, 3 * h + 1:3 * h + 2] * os_ref[:, cols]
               + gate[:, 3 * h + 2:3 * h + 3] * ow_ref[:, cols])
        o_ref[:, cols] = mix.astype(o_ref.dtype)


def nsa_gate_combine(gates, o_cmp, o_slc, o_win):
    m = gates.shape[0]
    tm = 128
    wide = pl.BlockSpec((tm, ATTN_WIDTH), lambda i: (i, 0))
    return pl.pallas_call(
        _nsa_gate_kernel,
        grid=(m // tm,),
        in_specs=[pl.BlockSpec((tm, gates.shape[1]), lambda i: (i, 0)), wide, wide, wide],
        out_specs=wide,
        out_shape=jax.ShapeDtypeStruct((m, ATTN_WIDTH), BF16),
        compiler_params=_params(("parallel",)),
        name="nsa_gate_combine",
    )(gates, o_cmp, o_slc, o_win)


def _moba_kernel(q_ref, k_ref, v_ref, bias_ref, o_ref, kb_ref, vb_ref, km_ref, m_ref, s_ref, acc_ref,
                 *, n_rep, tq, n_blk):
    i = pl.program_id(2)

    @pl.when(i == 0)
    def _():
        k = k_ref[0]
        kb_ref[...] = k.astype(BF16)
        vb_ref[...] = v_ref[0].astype(BF16)
        row = lax.broadcasted_iota(jnp.int32, (LANES, HEAD_DIM), 0)
        means = jnp.zeros((LANES, HEAD_DIM), F32)
        for j in range(n_blk):
            means = jnp.where(row == j, jnp.mean(k[j * tq:(j + 1) * tq], axis=0, keepdims=True), means)
        km_ref[...] = means.astype(BF16)

    qs = _stack_heads(q_ref[0], n_rep)
    rows = n_rep * tq
    gate = _dot_nt(qs, km_ref[...])
    blk = lax.broadcasted_iota(jnp.int32, (rows, LANES), 1)
    score = jnp.where(blk < i, gate, NEG_INF)
    rank = _first_rank(score, n_blk)
    chosen = jnp.where((rank < min(MOBA_TOPK, max(n_blk - 1, 1))) & (blk < i), 1.0, 0.0)

    _flash_init(m_ref, s_ref, acc_ref)
    own = pl.multiple_of(i * tq, tq)
    logits = _dot_nt(qs, kb_ref[pl.ds(own, tq), :]) * ATTN_SCALE + bias_ref[0]
    _flash_step(logits, vb_ref[pl.ds(own, tq), :], m_ref, s_ref, acc_ref)

    def body(cc, carry):
        c = i - cc
        start = pl.multiple_of(c * tq, tq)
        logits = _dot_nt(qs, kb_ref[pl.ds(start, tq), :]) * ATTN_SCALE + bias_ref[cc]
        row_on = jnp.sum(jnp.where(blk == c, chosen, 0.0), axis=-1, keepdims=True)
        logits = jnp.where(row_on > 0.5, logits, NEG_INF)
        _flash_step(logits, vb_ref[pl.ds(start, tq), :], m_ref, s_ref, acc_ref)
        return carry

    lax.fori_loop(1, i + 1, body, 0)
    _flash_finish(o_ref, s_ref, acc_ref, n_rep, tq)


def moba_attention(q, kv, causal_bias, batch, seqlen):
    n_rep = N_HEADS // MOBA_KV_HEADS
    tq = MOBA_BLOCK
    qw = n_rep * HEAD_DIM
    n_blk = seqlen // tq
    rows = n_rep * tq
    kv3 = kv.reshape(batch, seqlen, kv.shape[1])
    kern = functools.partial(_moba_kernel, n_rep=n_rep, tq=tq, n_blk=n_blk)
    return pl.pallas_call(
        kern,
        grid=(batch, MOBA_KV_HEADS, n_blk),
        in_specs=[pl.BlockSpec((1, tq, qw), lambda b, g, i: (b, i, g)),
                  pl.BlockSpec((1, seqlen, HEAD_DIM), lambda b, g, i: (b, 0, g)),
                  pl.BlockSpec((1, seqlen, HEAD_DIM), lambda b, g, i: (b, 0, MOBA_KV_HEADS + g)),
                  pl.BlockSpec((n_blk, rows, tq), lambda b, g, i: (0, g, 0), pipeline_mode=pl.Buffered(1))],
        out_specs=pl.BlockSpec((1, tq, qw), lambda b, g, i: (b, i, g)),
        out_shape=jax.ShapeDtypeStruct((batch, seqlen, ATTN_WIDTH), BF16),
        scratch_shapes=[pltpu.VMEM((seqlen, HEAD_DIM), BF16), pltpu.VMEM((seqlen, HEAD_DIM), BF16),
                        pltpu.VMEM((LANES, HEAD_DIM), BF16),
                        pltpu.VMEM((rows, LANES), F32), pltpu.VMEM((rows, LANES), F32),
                        pltpu.VMEM((rows, HEAD_DIM), F32)],
        compiler_params=_params(("parallel", "parallel", "arbitrary")),
        name="moba_attention",
    )(q.reshape(batch, seqlen, ATTN_WIDTH), kv3, kv3, causal_bias)


def _dil_combine_kernel(o0_ref, o1_ref, o2_ref, l0_ref, l1_ref, l2_ref, o_ref, *, n_rep):
    l0, l1, l2 = l0_ref[0, 0], l1_ref[0, 0], l2_ref[0, 0]
    top = jnp.maximum(jnp.maximum(l0, l1), l2)
    e0, e1, e2 = jnp.exp(l0 - top), jnp.exp(l1 - top), jnp.exp(l2 - top)
    den = e0 + e1 + e2
    a0, a1, a2 = e0 / den, e1 / den, e2 / den
    for r in range(n_rep):
        cols = slice(r * HEAD_DIM, (r + 1) * HEAD_DIM)
        mix = (a0[:, r:r + 1] * o0_ref[0, :, cols] + a1[:, r:r + 1] * o1_ref[0, :, cols]
               + a2[:, r:r + 1] * o2_ref[0, :, cols])
        o_ref[0, :, cols] = mix.astype(o_ref.dtype)


def dilated_combine(outs, lses, batch, seqlen):
    n_rep = N_HEADS // DIL_KV_HEADS
    tq = LANES
    qw = n_rep * HEAD_DIM
    o_spec = pl.BlockSpec((1, tq, qw), lambda b, g, i: (b, i, g))
    l_spec = pl.BlockSpec((1, 1, tq, LANES), lambda b, g, i: (b, g, i, 0))
    return pl.pallas_call(
        functools.partial(_dil_combine_kernel, n_rep=n_rep),
        grid=(batch, DIL_KV_HEADS, seqlen // tq),
        in_specs=[o_spec] * 3 + [l_spec] * 3,
        out_specs=o_spec,
        out_shape=jax.ShapeDtypeStruct((batch, seqlen, ATTN_WIDTH), BF16),
        compiler_params=_params(("parallel", "parallel", "parallel")),
        name="dilated_combine",
    )(*[o.reshape(batch, seqlen, ATTN_WIDTH) for o in outs], *lses)


def nsa_mixer(u, h, w_in, cmp_pos, k_w1, k_w2, v_w1, v_w2, w_out, tables, batch, seqlen):
    kvw = NSA_KV_HEADS * HEAD_DIM
    c0 = ATTN_WIDTH
    q = matmul(u, w_in[:, :c0].astype(BF16), BF16)
    kcvc = matmul(u, w_in[:, c0:c0 + 2 * kvw].astype(BF16), F32)
    kvsw = matmul(u, w_in[:, c0 + 2 * kvw:c0 + 6 * kvw].astype(BF16), BF16)
    gates = matmul(u, w_in[:, c0 + 6 * kvw:].astype(BF16), F32)
    kcmp, vcmp = nsa_compress(kcvc, cmp_pos, k_w1, k_w2, v_w1, v_w2, batch, seqlen)
    o_cmp, sel = nsa_cmp_attention(q, kcmp, vcmp, tables["cmp"], batch, seqlen)
    o_slc = nsa_selected_attention(q, kvsw, 0, kvw, sel, tables["causal"], batch, seqlen)
    o_win = banded_attention(q, 0, kvsw, 2 * kvw, 3 * kvw, tables["nsa_win"], batch=batch, seqlen=seqlen,
                             n_kv=NSA_KV_HEADS, dil=1, max_dist=NSA_WINDOW - 1)
    o = nsa_gate_combine(gates, o_cmp.reshape(batch * seqlen, ATTN_WIDTH),
                         o_slc.reshape(batch * seqlen, ATTN_WIDTH), o_win)
    return matmul_residual(o, w_out.astype(BF16), h, 1.0)


def dilated_mixer(u, h, w_in, w_out, tables, batch, seqlen):
    kvw = DIL_KV_HEADS * HEAD_DIM
    group = ATTN_WIDTH + 2 * kvw
    proj = matmul(u, w_in.astype(BF16), BF16)
    outs, lses = [], []
    for gi, (window, dil) in enumerate(DIL_PAIRS):
        off = gi * group
        o, lse = banded_attention(proj, off, proj, off + ATTN_WIDTH, off + ATTN_WIDTH + kvw,
                                  tables["dil%d" % dil], batch=batch, seqlen=seqlen, n_kv=DIL_KV_HEADS,
                                  dil=dil, max_dist=window // dil, want_lse=True)
        outs.append(o)
        lses.append(lse)
    o = dilated_combine(outs, lses, batch, seqlen)
    return matmul_residual(o.reshape(batch * seqlen, ATTN_WIDTH), w_out.astype(BF16), h, 1.0)


def moba_mixer(u, h, w_in, w_out, tables, batch, seqlen):
    q = matmul(u, w_in[:, :ATTN_WIDTH].astype(BF16), BF16)
    kv = matmul(u, w_in[:, ATTN_WIDTH:].astype(BF16), F32)
    o = moba_attention(q, kv, tables["causal"], batch, seqlen)
    return matmul_residual(o.reshape(batch * seqlen, ATTN_WIDTH), w_out.astype(BF16), h, 1.0)


def swa_mixer(u, h, w_in, sinks, w_out, tables, batch, seqlen):
    kvw = SWA_KV_HEADS * HEAD_DIM
    proj = matmul(u, w_in.astype(BF16), BF16)
    o = banded_attention(proj, 0, proj, ATTN_WIDTH, ATTN_WIDTH + kvw, tables["swa"], batch=batch,
                         seqlen=seqlen, n_kv=SWA_KV_HEADS, dil=1, max_dist=SWA_WINDOW - 1, sinks=sinks,
                         out_dtype=BF16)
    return matmul_residual(o, w_out.astype(BF16), h, 1.0)


def _band_table_for(rel_table, seqlen, dil, max_dist):
    sub = seqlen // dil
    tq = math.gcd(sub, BAND_BLOCK)
    n_prev = min(-(-max_dist // tq), sub // tq - 1)
    return band_bias_table(rel_table, tq, (n_prev + 1) * tq, n_prev * tq, max_dist, dil)


def kernel(x, rel_table, ffn1_norm, ffn1_w_gate, ffn1_w_up, ffn1_w_down, mix_norm, ffn2_norm, ffn2_w_gate, ffn2_w_up, ffn2_w_down, final_norm, nsa_w_in, nsa_cmp_pos, nsa_cmp_k_w1, nsa_cmp_k_w2, nsa_cmp_v_w1, nsa_cmp_v_w2, nsa_w_out, dil_w_in, dil_w_out, moba_w_in, moba_w_out, swa_w_in, swa_sinks, swa_w_out):
    batch, seqlen, d_model = x.shape
    depth = ffn1_norm.shape[0]
    n_mixers = 4
    h = x.reshape(batch * seqlen, d_model)

    tables = {
        "causal": causal_bias_table(rel_table, SEL_TILE, seqlen // SEL_TILE),
        "cmp": cmp_bias_table(rel_table, seqlen, (seqlen - NSA_CMP_LEN) // NSA_CMP_STRIDE + 1),
        "nsa_win": _band_table_for(rel_table, seqlen, 1, NSA_WINDOW - 1),
        "swa": _band_table_for(rel_table, seqlen, 1, SWA_WINDOW - 1),
    }
    for window, dil in DIL_PAIRS:
        tables["dil%d" % dil] = _band_table_for(rel_table, seqlen, dil, window // dil)

    for i in range(depth):
        h = ffn_half_step(h, ffn1_norm[i], ffn1_w_gate[i], ffn1_w_up[i], ffn1_w_down[i])
        u = rms_norm(h, mix_norm[i], BF16)
        m, j = i % n_mixers, i // n_mixers
        if m == 0:
            h = nsa_mixer(u, h, nsa_w_in[j], nsa_cmp_pos[j], nsa_cmp_k_w1[j], nsa_cmp_k_w2[j],
                          nsa_cmp_v_w1[j], nsa_cmp_v_w2[j], nsa_w_out[j], tables, batch, seqlen)
        elif m == 1:
            h = dilated_mixer(u, h, dil_w_in[j], dil_w_out[j], tables, batch, seqlen)
        elif m == 2:
            h = moba_mixer(u, h, moba_w_in[j], moba_w_out[j], tables, batch, seqlen)
        else:
            h = swa_mixer(u, h, swa_w_in[j], swa_sinks[j], swa_w_out[j], tables, batch, seqlen)
        h = ffn_half_step(h, ffn2_norm[i], ffn2_w_gate[i], ffn2_w_up[i], ffn2_w_down[i])
    return rms_norm(h, final_norm, x.dtype).reshape(batch, seqlen, d_model)
```

```python
import functools
import math

import numpy as np
import jax
import jax.numpy as jnp
from jax import lax
from jax.experimental import pallas as pl
from jax.experimental.pallas import tpu as pltpu

HEAD_DIM = 128
N_HEADS = 32
ATTN_WIDTH = N_HEADS * HEAD_DIM
RMS_EPS = 1e-6
REL_BUCKETS = 32
REL_MAX_DIST = 2048
BAND_BLOCK = 128
NSA_KV_HEADS = 4
NSA_CMP_LEN = 32
NSA_CMP_STRIDE = 16
NSA_SEL_LEN = 64
NSA_SEL_TOPN = 16
NSA_WINDOW = 512
DIL_PAIRS = ((128, 1), (512, 4), (2048, 16))
DIL_KV_HEADS = 8
MOBA_BLOCK = 256
MOBA_TOPK = 3
MOBA_KV_HEADS = 8
SWA_WINDOW = 128
SWA_KV_HEADS = 4
ATTN_SCALE = HEAD_DIM ** -0.5
NEG_INF = -1e30
MASKED_BELOW = -5e29
TINY = 1e-20
FORCED_SCORE = 1e9

LANES = 128
SUBLANES = 8
BF16_ROWS = 16
SEL_TILE = 256
VMEM_LIMIT = 56 * 1024 * 1024

F32 = jnp.float32
BF16 = jnp.bfloat16


def _params(semantics):
    return pltpu.CompilerParams(dimension_semantics=semantics, vmem_limit_bytes=VMEM_LIMIT)


def _dot_nt(a, b):
    return lax.dot_general(a, b, (((1,), (1,)), ((), ())), preferred_element_type=F32)


def _dot_tn(a, b):
    return lax.dot_general(a, b, (((0,), (0,)), ((), ())), preferred_element_type=F32)


def _stack_heads(q, n_heads):
    return jnp.concatenate([q[:, r * HEAD_DIM:(r + 1) * HEAD_DIM] for r in range(n_heads)], axis=0)


def _store_heads(o_ref, o_t, n_heads, tq):
    for r in range(n_heads):
        o_ref[0, :, r * HEAD_DIM:(r + 1) * HEAD_DIM] = o_t[:, r * tq:(r + 1) * tq].T.astype(o_ref.dtype)


def _rms_kernel(x_ref, g_ref, o_ref):
    x = x_ref[...]
    y = x * lax.rsqrt(jnp.mean(x * x, axis=-1, keepdims=True) + RMS_EPS)
    o_ref[...] = (y * g_ref[...]).astype(o_ref.dtype)


def rms_norm(x, gain, out_dtype):
    m, d = x.shape
    tm = 256
    return pl.pallas_call(
        _rms_kernel,
        grid=(m // tm,),
        in_specs=[pl.BlockSpec((tm, d), lambda i: (i, 0)), pl.BlockSpec((1, d), lambda i: (0, 0))],
        out_specs=pl.BlockSpec((tm, d), lambda i: (i, 0)),
        out_shape=jax.ShapeDtypeStruct((m, d), out_dtype),
        compiler_params=_params(("parallel",)),
        name="rms_norm",
    )(x, gain.reshape(1, d))


def _mm_kernel(a_ref, b_ref, o_ref):
    o_ref[...] = jnp.dot(a_ref[...], b_ref[...], preferred_element_type=F32).astype(o_ref.dtype)


def _mm_res_kernel(a_ref, b_ref, r_ref, o_ref, *, scale):
    o_ref[...] = r_ref[...] + scale * jnp.dot(a_ref[...], b_ref[...], preferred_element_type=F32)


def _mm_swiglu_kernel(a_ref, bg_ref, bu_ref, o_ref):
    a = a_ref[...]
    gate = jnp.dot(a, bg_ref[...], preferred_element_type=F32)
    up = jnp.dot(a, bu_ref[...], preferred_element_type=F32)
    o_ref[...] = (jax.nn.silu(gate) * up).astype(o_ref.dtype)


def _mm_tiles(m, n):
    tm = min(m, 1024)
    tn = 512 if n % 512 == 0 else n
    return tm, tn


def matmul(a, b, out_dtype):
    m, k = a.shape
    n = b.shape[1]
    tm, tn = _mm_tiles(m, n)
    return pl.pallas_call(
        _mm_kernel,
        grid=(m // tm, n // tn),
        in_specs=[pl.BlockSpec((tm, k), lambda i, j: (i, 0)),
                  pl.BlockSpec((k, tn), lambda i, j: (0, j))],
        out_specs=pl.BlockSpec((tm, tn), lambda i, j: (i, j)),
        out_shape=jax.ShapeDtypeStruct((m, n), out_dtype),
        compiler_params=_params(("parallel", "arbitrary")),
        name="matmul",
    )(a, b)


def matmul_residual(a, b, res, scale):
    m, k = a.shape
    n = b.shape[1]
    tm, tn = _mm_tiles(m, n)
    return pl.pallas_call(
        functools.partial(_mm_res_kernel, scale=scale),
        grid=(m // tm, n // tn),
        in_specs=[pl.BlockSpec((tm, k), lambda i, j: (i, 0)),
                  pl.BlockSpec((k, tn), lambda i, j: (0, j)),
                  pl.BlockSpec((tm, tn), lambda i, j: (i, j))],
        out_specs=pl.BlockSpec((tm, tn), lambda i, j: (i, j)),
        out_shape=jax.ShapeDtypeStruct((m, n), F32),
        compiler_params=_params(("parallel", "arbitrary")),
        name="matmul_residual",
    )(a, b, res)


def matmul_swiglu(a, bg, bu):
    m, k = a.shape
    n = bg.shape[1]
    tm, tn = _mm_tiles(m, n)
    return pl.pallas_call(
        _mm_swiglu_kernel,
        grid=(m // tm, n // tn),
        in_specs=[pl.BlockSpec((tm, k), lambda i, j: (i, 0)),
                  pl.BlockSpec((k, tn), lambda i, j: (0, j)),
                  pl.BlockSpec((k, tn), lambda i, j: (0, j))],
        out_specs=pl.BlockSpec((tm, tn), lambda i, j: (i, j)),
        out_shape=jax.ShapeDtypeStruct((m, n), BF16),
        compiler_params=_params(("parallel", "arbitrary")),
        name="matmul_swiglu",
    )(a, bg, bu)


def ffn_half_step(h, gain, w_gate, w_up, w_down):
    u = rms_norm(h, gain, BF16)
    act = matmul_swiglu(u, w_gate.astype(BF16), w_up.astype(BF16))
    return matmul_residual(act, w_down.astype(BF16), h, 0.5)


def _t5_bias_values(dist, tab_ref, h):
    n = jnp.maximum(dist, 0)
    exact = REL_BUCKETS // 2
    nf = jnp.maximum(n, 1).astype(F32)
    large = exact + (jnp.log(nf / exact) * ((REL_BUCKETS - exact) / math.log(REL_MAX_DIST / exact))).astype(jnp.int32)
    bucket = jnp.where(n < exact, n, jnp.minimum(large, REL_BUCKETS - 1))
    out = jnp.zeros(dist.shape, F32)
    for b in range(REL_BUCKETS):
        out = jnp.where(bucket == b, tab_ref[b, h], out)
    return out


def _bias_table_kernel(tab_ref, o_ref, *, head_axis, blk_axis, base0, base_step, key_stride,
                       max_dist, n_valid_keys, dist_scale):
    h = pl.program_id(head_axis)
    blk = pl.program_id(blk_axis)
    n_keys, n_qry = o_ref.shape[-2:]
    key = lax.broadcasted_iota(jnp.int32, (n_keys, n_qry), 0)
    qry = lax.broadcasted_iota(jnp.int32, (n_keys, n_qry), 1)
    dist = base0 + blk * base_step + qry - key * key_stride
    valid = (dist >= 0) & (dist <= max_dist) & (key < n_valid_keys)
    bias = _t5_bias_values(dist * dist_scale, tab_ref, h)
    o_ref[...] = jnp.where(valid, bias, NEG_INF).reshape(o_ref.shape)


def band_bias_table(rel_table, tq, span, pad, max_dist, dist_scale):
    kern = functools.partial(_bias_table_kernel, head_axis=0, blk_axis=1, base0=pad, base_step=0,
                             key_stride=1, max_dist=max_dist, n_valid_keys=span, dist_scale=dist_scale)
    return pl.pallas_call(
        kern,
        grid=(N_HEADS, 1),
        in_specs=[pl.BlockSpec(memory_space=pltpu.SMEM)],
        out_specs=pl.BlockSpec((span, tq), lambda h, j: (0, h)),
        out_shape=jax.ShapeDtypeStruct((span, N_HEADS * tq), F32),
        compiler_params=_params(("parallel", "arbitrary")),
        name="band_bias_table",
    )(rel_table)


def causal_bias_table(rel_table, tile, n_cls):
    kern = functools.partial(_bias_table_kernel, head_axis=1, blk_axis=0, base0=0, base_step=tile,
                             key_stride=1, max_dist=2 ** 30, n_valid_keys=tile, dist_scale=1)
    return pl.pallas_call(
        kern,
        grid=(n_cls, N_HEADS),
        in_specs=[pl.BlockSpec(memory_space=pltpu.SMEM)],
        out_specs=pl.BlockSpec((1, tile, tile), lambda c, h: (c, 0, h)),
        out_shape=jax.ShapeDtypeStruct((n_cls, tile, N_HEADS * tile), F32),
        compiler_params=_params(("parallel", "arbitrary")),
        name="causal_bias_table",
    )(rel_table)


def cmp_bias_table(rel_table, seqlen, n_cmp):
    tq = LANES
    kern = functools.partial(_bias_table_kernel, head_axis=1, blk_axis=0,
                             base0=-(NSA_CMP_LEN - 1), base_step=tq, key_stride=NSA_CMP_STRIDE,
                             max_dist=2 ** 30, n_valid_keys=n_cmp, dist_scale=1)
    return pl.pallas_call(
        kern,
        grid=(seqlen // tq, N_HEADS),
        in_specs=[pl.BlockSpec(memory_space=pltpu.SMEM)],
        out_specs=pl.BlockSpec((1, LANES, tq), lambda i, h: (i, 0, h)),
        out_shape=jax.ShapeDtypeStruct((seqlen // tq, LANES, N_HEADS * tq), F32),
        compiler_params=_params(("parallel", "arbitrary")),
        name="cmp_bias_table",
    )(rel_table)


def _banded_kernel(*refs, n_rep, tq, n_prev, seq, has_sink, want_lse):
    q_ref, k_ref, v_ref, bias_ref = refs[:4]
    pos = 4
    sink_ref = None
    if has_sink:
        sink_ref = refs[pos]
        pos += 1
    o_ref = refs[pos]
    pos += 1
    lse_ref = None
    if want_lse:
        lse_ref = refs[pos]
        pos += 1
    kpad_ref, vpad_ref = refs[pos:pos + 2]

    g = pl.program_id(2)
    i = pl.program_id(3)
    pad = n_prev * tq
    span = pad + tq
    rows = n_rep * tq

    @pl.when(i == 0)
    def _():
        if pad:
            kpad_ref[0:pad, :] = jnp.zeros((pad, HEAD_DIM), BF16)
            vpad_ref[0:pad, :] = jnp.zeros((pad, HEAD_DIM), BF16)
        kpad_ref[pad:pad + seq, :] = k_ref[0]
        vpad_ref[pad:pad + seq, :] = v_ref[0]

    start = pl.multiple_of(i * tq, tq)
    ks = kpad_ref[pl.ds(start, span), :]
    vs = vpad_ref[pl.ds(start, span), :]
    qs = _stack_heads(q_ref[0], n_rep)
    lt = _dot_nt(ks, qs) * ATTN_SCALE + bias_ref[...]
    if pad:
        key = lax.broadcasted_iota(jnp.int32, (span, rows), 0)
        lt = jnp.where(key >= pad - i * tq, lt, NEG_INF)
    m = jnp.max(lt, axis=0, keepdims=True)
    sink = None
    if has_sink:
        sink = jnp.concatenate(
            [jnp.full((1, tq), sink_ref[g * n_rep + r], F32) for r in range(n_rep)], axis=1)
        m = jnp.maximum(m, sink)
    p = jnp.exp(lt - m)
    s = jnp.sum(p, axis=0, keepdims=True)
    if has_sink:
        s = s + jnp.exp(sink - m)
    o_t = _dot_tn(vs, p.astype(BF16))
    s = jnp.maximum(s, TINY)
    _store_heads(o_ref, o_t / s, n_rep, tq)
    if want_lse:
        lse = m + jnp.log(s)
        head = lax.broadcasted_iota(jnp.int32, (LANES, tq), 0)
        tile = jnp.zeros((LANES, tq), F32)
        for r in range(n_rep):
            tile = jnp.where(head == r, lse[:, r * tq:(r + 1) * tq], tile)
        lse_ref[0, 0] = tile.T


def banded_attention(q_arr, q_off, kv_arr, k_off, v_off, bias, *, batch, seqlen, n_kv, dil,
                     max_dist, sinks=None, want_lse=False, out_dtype=F32):
    n_rep = N_HEADS // n_kv
    sub = seqlen // dil
    tq = math.gcd(sub, BAND_BLOCK)
    n_blk = sub // tq
    n_prev = min(-(-max_dist // tq), n_blk - 1)
    span = (n_prev + 1) * tq
    cq = q_arr.shape[1]
    ckv = kv_arr.shape[1]
    qw = n_rep * HEAD_DIM
    q3 = q_arr.reshape(batch, sub, dil * cq)
    kv3 = kv_arr.reshape(batch, sub, dil * ckv)
    assert cq % qw == 0 and q_off % qw == 0 and bias.shape == (span, N_HEADS * tq)

    in_specs = [
        pl.BlockSpec((1, tq, qw), lambda b, rho, g, i: (b, i, (rho * cq + q_off) // qw + g)),
        pl.BlockSpec((1, sub, HEAD_DIM), lambda b, rho, g, i: (b, 0, (rho * ckv + k_off) // HEAD_DIM + g)),
        pl.BlockSpec((1, sub, HEAD_DIM), lambda b, rho, g, i: (b, 0, (rho * ckv + v_off) // HEAD_DIM + g)),
        pl.BlockSpec((span, n_rep * tq), lambda b, rho, g, i: (0, g)),
    ]
    args = [q3, kv3, kv3, bias]
    if sinks is not None:
        in_specs.append(pl.BlockSpec(memory_space=pltpu.SMEM))
        args.append(sinks)
    out_specs = [pl.BlockSpec((1, tq, qw), lambda b, rho, g, i: (b, i, rho * (ATTN_WIDTH // qw) + g))]
    out_shape = [jax.ShapeDtypeStruct((batch, sub, dil * ATTN_WIDTH), out_dtype)]
    if want_lse:
        out_specs.append(pl.BlockSpec((1, 1, tq, LANES), lambda b, rho, g, i: (b, g, i, rho)))
        out_shape.append(jax.ShapeDtypeStruct((batch, n_kv, sub, dil * LANES), F32))
    kern = functools.partial(_banded_kernel, n_rep=n_rep, tq=tq, n_prev=n_prev, seq=sub,
                             has_sink=sinks is not None, want_lse=want_lse)
    outs = pl.pallas_call(
        kern,
        grid=(batch, dil, n_kv, n_blk),
        in_specs=in_specs,
        out_specs=out_specs,
        out_shape=out_shape,
        scratch_shapes=[pltpu.VMEM((n_prev * tq + sub, HEAD_DIM), BF16),
                        pltpu.VMEM((n_prev * tq + sub, HEAD_DIM), BF16)],
        compiler_params=_params(("parallel", "parallel", "parallel", "arbitrary")),
        name="banded_attention",
    )(*args)
    o = outs[0].reshape(batch * seqlen, ATTN_WIDTH)
    if want_lse:
        return o, outs[1].reshape(batch, n_kv, seqlen, LANES)
    return o


def _flash_init(m_ref, s_ref, acc_ref):
    m_ref[...] = jnp.full(m_ref.shape, NEG_INF, F32)
    s_ref[...] = jnp.zeros(s_ref.shape, F32)
    acc_ref[...] = jnp.zeros(acc_ref.shape, F32)


def _flash_step(lt, v_blk, m_ref, s_ref, acc_ref):
    m_prev = m_ref[...]
    m_new = jnp.maximum(m_prev, jnp.max(lt, axis=0, keepdims=True))
    alpha = jnp.exp(m_prev - m_new)
    p = jnp.exp(lt - m_new)
    s_ref[...] = alpha * s_ref[...] + jnp.sum(p, axis=0, keepdims=True)
    acc_ref[...] = alpha * acc_ref[...] + _dot_tn(v_blk, p.astype(BF16))
    m_ref[...] = m_new


def _flash_finish(o_ref, s_ref, acc_ref, n_rep, tq):
    _store_heads(o_ref, acc_ref[...] / jnp.maximum(s_ref[...], TINY), n_rep, tq)


def _first_rank(score, n_cand):
    idx = lax.broadcasted_iota(jnp.int32, score.shape, 0)
    rank = jnp.zeros(score.shape, F32)
    for jp in range(n_cand):
        other = score[jp:jp + 1, :]
        ahead = jnp.where(other > score, 1.0, jnp.where(other == score, jnp.where(idx > jp, 1.0, 0.0), 0.0))
        rank = rank + ahead
    return rank


def _nsa_cmp_kernel(x_ref, pos_ref, w1k_ref, w2k_ref, w1v_ref, w2v_ref, ko_ref, vo_ref):
    width = 2 * NSA_KV_HEADS * HEAD_DIM
    for kv, (w1_ref, w2_ref, o_ref) in enumerate(((w1k_ref, w2k_ref, ko_ref), (w1v_ref, w2v_ref, vo_ref))):
        for g in range(NSA_KV_HEADS):
            off = kv * NSA_KV_HEADS * HEAD_DIM + g * HEAD_DIM
            chunk = jnp.concatenate(
                [x_ref[0, :, l * width + off:l * width + off + HEAD_DIM] for l in range(NSA_CMP_STRIDE)], axis=1)
            first = jnp.dot((chunk + pos_ref[0:1, :]).astype(BF16), w1_ref[0], preferred_element_type=F32)
            second = jnp.dot((chunk + pos_ref[1:2, :]).astype(BF16), w1_ref[1], preferred_element_type=F32)
            hidden = jax.nn.gelu(first + pltpu.roll(second, second.shape[0] - 1, axis=0))
            o_ref[0, g] = jnp.dot(hidden.astype(BF16), w2_ref[...], preferred_element_type=F32).astype(o_ref.dtype)


def nsa_compress(kcvc, cmp_pos, k_w1, k_w2, v_w1, v_w2, batch, seqlen):
    n_chunk = seqlen // NSA_CMP_STRIDE
    width = 2 * NSA_KV_HEADS * HEAD_DIM
    half = NSA_CMP_STRIDE * HEAD_DIM
    x = kcvc.reshape(batch, n_chunk, NSA_CMP_STRIDE * width)
    out = jax.ShapeDtypeStruct((batch, NSA_KV_HEADS, n_chunk, HEAD_DIM), BF16)
    full = lambda shape: pl.BlockSpec(shape, lambda b: (0,) * len(shape))
    return pl.pallas_call(
        _nsa_cmp_kernel,
        grid=(batch,),
        in_specs=[pl.BlockSpec((1, n_chunk, NSA_CMP_STRIDE * width), lambda b: (b, 0, 0)),
                  full((2, half)), full((2, half, HEAD_DIM)), full((HEAD_DIM, HEAD_DIM)),
                  full((2, half, HEAD_DIM)), full((HEAD_DIM, HEAD_DIM))],
        out_specs=[pl.BlockSpec((1, NSA_KV_HEADS, n_chunk, HEAD_DIM), lambda b: (b, 0, 0, 0))] * 2,
        out_shape=[out, out],
        compiler_params=_params(("parallel",)),
        name="nsa_compress",
    )(x, cmp_pos.reshape(2, half), k_w1.reshape(2, half, HEAD_DIM).astype(BF16), k_w2.astype(BF16),
      v_w1.reshape(2, half, HEAD_DIM).astype(BF16), v_w2.astype(BF16))


def _nsa_cmp_attn_kernel(q_ref, kc_ref, vc_ref, bias_ref, c2s_ref, o_ref, sel_ref, *, n_rep, tq, n_sel_blk):
    i = pl.program_id(2)
    qs = _stack_heads(q_ref[0], n_rep)
    bias = bias_ref[0]
    valid = bias > MASKED_BELOW
    lt = jnp.where(valid, _dot_nt(kc_ref[0, 0], qs) * ATTN_SCALE + bias, NEG_INF)
    m = jnp.max(lt, axis=0, keepdims=True)
    p = jnp.where(valid, jnp.exp(lt - m), 0.0)
    s = jnp.sum(p, axis=0, keepdims=True)
    p_cmp = p / jnp.maximum(s, TINY)
    _store_heads(o_ref, _dot_tn(vc_ref[0, 0], p_cmp.astype(BF16)), n_rep, tq)

    p_sum = p_cmp[:, 0:tq]
    for r in range(1, n_rep):
        p_sum = p_sum + p_cmp[:, r * tq:(r + 1) * tq]
    imp = jnp.dot(c2s_ref[...], p_sum.astype(BF16), preferred_element_type=F32)[0:n_sel_blk]
    blk = lax.broadcasted_iota(jnp.int32, (n_sel_blk, tq), 0)
    tpos = i * tq + lax.broadcasted_iota(jnp.int32, (n_sel_blk, tq), 1)
    cur = tpos // NSA_SEL_LEN
    forced = (blk == 0) | (blk == cur) | (blk == cur - 1)
    score = jnp.where(forced, FORCED_SCORE, jnp.where(blk * NSA_SEL_LEN <= tpos, imp, NEG_INF))
    chosen = jnp.where(_first_rank(score, n_sel_blk) < min(NSA_SEL_TOPN, n_sel_blk), 1.0, 0.0)
    sel_ref[0, 0] = jnp.concatenate([chosen, jnp.zeros((LANES - n_sel_blk, tq), F32)], axis=0).astype(sel_ref.dtype)


def nsa_cmp_attention(q, kcmp, vcmp, bias, batch, seqlen):
    n_rep = N_HEADS // NSA_KV_HEADS
    tq = LANES
    qw = n_rep * HEAD_DIM
    n_sel_blk = seqlen // NSA_SEL_LEN
    n_cmp = (seqlen - NSA_CMP_LEN) // NSA_CMP_STRIDE + 1
    a, b = NSA_SEL_LEN // NSA_CMP_STRIDE, NSA_CMP_LEN // NSA_CMP_STRIDE
    w = np.zeros((LANES, LANES), np.float32)
    j = np.arange(n_sel_blk)
    for mm in range(a):
        for nn in range(b):
            ii = a * j + mm + nn - (b - 1)
            ok = (ii >= 0) & (ii < n_cmp)
            np.add.at(w, (j[ok], ii[ok]), 1.0)
    kern = functools.partial(_nsa_cmp_attn_kernel, n_rep=n_rep, tq=tq, n_sel_blk=n_sel_blk)
    return pl.pallas_call(
        kern,
        grid=(batch, NSA_KV_HEADS, seqlen // tq),
        in_specs=[pl.BlockSpec((1, tq, qw), lambda b_, g, i: (b_, i, g)),
                  pl.BlockSpec((1, 1, LANES, HEAD_DIM), lambda b_, g, i: (b_, g, 0, 0)),
                  pl.BlockSpec((1, 1, LANES, HEAD_DIM), lambda b_, g, i: (b_, g, 0, 0)),
                  pl.BlockSpec((1, LANES, n_rep * tq), lambda b_, g, i: (i, 0, g)),
                  pl.BlockSpec((LANES, LANES), lambda b_, g, i: (0, 0))],
        out_specs=[pl.BlockSpec((1, tq, qw), lambda b_, g, i: (b_, i, g)),
                   pl.BlockSpec((1, 1, LANES, tq), lambda b_, g, i: (b_, g, 0, i))],
        out_shape=[jax.ShapeDtypeStruct((batch, seqlen, ATTN_WIDTH), F32),
                   jax.ShapeDtypeStruct((batch, NSA_KV_HEADS, LANES, seqlen), BF16)],
        compiler_params=_params(("parallel", "parallel", "arbitrary")),
        name="nsa_cmp_attention",
    )(q.reshape(batch, seqlen, ATTN_WIDTH), kcmp, vcmp, bias, jnp.asarray(w, BF16))


def _nsa_sel_kernel(q_ref, k_ref, v_ref, sel_ref, bias_ref, exp_ref, o_ref, m_ref, s_ref, acc_ref, *, n_rep, tq):
    i = pl.program_id(2)
    qs = _stack_heads(q_ref[0], n_rep)
    sel = jnp.concatenate([sel_ref[0, 0]] * n_rep, axis=1)
    _flash_init(m_ref, s_ref, acc_ref)

    def body(cc, carry):
        c = i - cc
        start = pl.multiple_of(c * tq, tq)
        lt = _dot_nt(k_ref[0, pl.ds(start, tq), :], qs) * ATTN_SCALE + bias_ref[cc]
        key_on = jnp.dot(exp_ref[c], sel, preferred_element_type=F32)
        lt = jnp.where(key_on > 0.5, lt, NEG_INF)
        _flash_step(lt, v_ref[0, pl.ds(start, tq), :], m_ref, s_ref, acc_ref)
        return carry

    lax.fori_loop(0, i + 1, body, 0)
    _flash_finish(o_ref, s_ref, acc_ref, n_rep, tq)


def nsa_selected_attention(q, kv, k_off, v_off, sel, causal_bias, batch, seqlen):
    n_rep = N_HEADS // NSA_KV_HEADS
    tq = SEL_TILE
    qw = n_rep * HEAD_DIM
    n_blk = seqlen // tq
    per = tq // NSA_SEL_LEN
    expand = np.zeros((n_blk, tq, LANES), np.float32)
    for c in range(n_blk):
        for kk in range(tq):
            expand[c, kk, per * c + kk // NSA_SEL_LEN] = 1.0
    ckv = kv.shape[1]
    kv3 = kv.reshape(batch, seqlen, ckv)
    kern = functools.partial(_nsa_sel_kernel, n_rep=n_rep, tq=tq)
    rows = n_rep * tq
    return pl.pallas_call(
        kern,
        grid=(batch, NSA_KV_HEADS, n_blk),
        in_specs=[pl.BlockSpec((1, tq, qw), lambda b, g, i: (b, i, g)),
                  pl.BlockSpec((1, seqlen, HEAD_DIM), lambda b, g, i: (b, 0, k_off // HEAD_DIM + g)),
                  pl.BlockSpec((1, seqlen, HEAD_DIM), lambda b, g, i: (b, 0, v_off // HEAD_DIM + g)),
                  pl.BlockSpec((1, 1, LANES, tq), lambda b, g, i: (b, g, 0, i)),
                  pl.BlockSpec((n_blk, tq, rows), lambda b, g, i: (0, 0, g), pipeline_mode=pl.Buffered(1)),
                  pl.BlockSpec((n_blk, tq, LANES), lambda b, g, i: (0, 0, 0), pipeline_mode=pl.Buffered(1))],
        out_specs=pl.BlockSpec((1, tq, qw), lambda b, g, i: (b, i, g)),
        out_shape=jax.ShapeDtypeStruct((batch, seqlen, ATTN_WIDTH), F32),
        scratch_shapes=[pltpu.VMEM((1, rows), F32), pltpu.VMEM((1, rows), F32),
                        pltpu.VMEM((HEAD_DIM, rows), F32)],
        compiler_params=_params(("parallel", "parallel", "arbitrary")),
        name="nsa_selected_attention",
    )(q.reshape(batch, seqlen, ATTN_WIDTH), kv3, kv3, sel, causal_bias, jnp.asarray(expand, BF16))


def _nsa_gate_kernel(g_ref, oc_ref, os_ref, ow_ref, o_ref):
    gate = jax.nn.sigmoid(g_ref[...])
    for h in range(N_HEADS):
        cols = slice(h * HEAD_DIM, (h + 1) * HEAD_DIM)
        mix = (gate[:, 3 * h:3 * h + 1] * oc_ref[:, cols] + gate[:, 3 * h + 1:3 * h + 2] * os_ref[:, cols]
               + gate[:, 3 * h + 2:3 * h + 3] * ow_ref[:, cols])
        o_ref[:, cols] = mix.astype(o_ref.dtype)


def nsa_gate_combine(gates, o_cmp, o_slc, o_win):
    m = gates.shape[0]
    tm = 128
    wide = pl.BlockSpec((tm, ATTN_WIDTH), lambda i: (i, 0))
    return pl.pallas_call(
        _nsa_gate_kernel,
        grid=(m // tm,),
        in_specs=[pl.BlockSpec((tm, gates.shape[1]), lambda i: (i, 0)), wide, wide, wide],
        out_specs=wide,
        out_shape=jax.ShapeDtypeStruct((m, ATTN_WIDTH), BF16),
        compiler_params=_params(("parallel",)),
        name="nsa_gate_combine",
    )(gates, o_cmp, o_slc, o_win)


def _moba_kernel(q_ref, k_ref, v_ref, bias_ref, o_ref, kb_ref, vb_ref, km_ref, m_ref, s_ref, acc_ref,
                 *, n_rep, tq, n_blk):
    i = pl.program_id(2)
    rows = n_rep * tq

    @pl.when(i == 0)
    def _():
        k = k_ref[0]
        kb_ref[...] = k.astype(BF16)
        vb_ref[...] = v_ref[0].astype(BF16)
        slot = lax.broadcasted_iota(jnp.int32, (BF16_ROWS, HEAD_DIM), 0)
        means = jnp.zeros((BF16_ROWS, HEAD_DIM), F32)
        for j in range(n_blk):
            means = jnp.where(slot == j, jnp.mean(k[j * tq:(j + 1) * tq], axis=0, keepdims=True), means)
        km_ref[...] = means.astype(BF16)

    qs = _stack_heads(q_ref[0], n_rep)
    gate = _dot_nt(km_ref[...], qs)[0:n_blk]
    blk = lax.broadcasted_iota(jnp.int32, (n_blk, rows), 0)
    past = blk < i
    rank = _first_rank(jnp.where(past, gate, NEG_INF), n_blk)
    chosen = jnp.where(past, jnp.where(rank < min(MOBA_TOPK, max(n_blk - 1, 1)), 1.0, 0.0), 0.0)

    _flash_init(m_ref, s_ref, acc_ref)
    own = pl.multiple_of(i * tq, tq)
    lt = _dot_nt(kb_ref[pl.ds(own, tq), :], qs) * ATTN_SCALE + bias_ref[0]
    _flash_step(lt, vb_ref[pl.ds(own, tq), :], m_ref, s_ref, acc_ref)

    def body(cc, carry):
        c = i - cc
        start = pl.multiple_of(c * tq, tq)
        lt = _dot_nt(kb_ref[pl.ds(start, tq), :], qs) * ATTN_SCALE + bias_ref[cc]
        row_on = jnp.sum(jnp.where(blk == c, chosen, 0.0), axis=0, keepdims=True)
        lt = jnp.where(row_on > 0.5, lt, NEG_INF)
        _flash_step(lt, vb_ref[pl.ds(start, tq), :], m_ref, s_ref, acc_ref)
        return carry

    lax.fori_loop(1, i + 1, body, 0)
    _flash_finish(o_ref, s_ref, acc_ref, n_rep, tq)


def moba_attention(q, kv, causal_bias, batch, seqlen):
    n_rep = N_HEADS // MOBA_KV_HEADS
    tq = MOBA_BLOCK
    qw = n_rep * HEAD_DIM
    n_blk = seqlen // tq
    assert n_blk <= SUBLANES
    rows = n_rep * tq
    kv3 = kv.reshape(batch, seqlen, kv.shape[1])
    kern = functools.partial(_moba_kernel, n_rep=n_rep, tq=tq, n_blk=n_blk)
    return pl.pallas_call(
        kern,
        grid=(batch, MOBA_KV_HEADS, n_blk),
        in_specs=[pl.BlockSpec((1, tq, qw), lambda b, g, i: (b, i, g)),
                  pl.BlockSpec((1, seqlen, HEAD_DIM), lambda b, g, i: (b, 0, g)),
                  pl.BlockSpec((1, seqlen, HEAD_DIM), lambda b, g, i: (b, 0, MOBA_KV_HEADS + g)),
                  pl.BlockSpec((n_blk, tq, rows), lambda b, g, i: (0, 0, g), pipeline_mode=pl.Buffered(1))],
        out_specs=pl.BlockSpec((1, tq, qw), lambda b, g, i: (b, i, g)),
        out_shape=jax.ShapeDtypeStruct((batch, seqlen, ATTN_WIDTH), BF16),
        scratch_shapes=[pltpu.VMEM((seqlen, HEAD_DIM), BF16), pltpu.VMEM((seqlen, HEAD_DIM), BF16),
                        pltpu.VMEM((BF16_ROWS, HEAD_DIM), BF16),
                        pltpu.VMEM((1, rows), F32), pltpu.VMEM((1, rows), F32),
                        pltpu.VMEM((HEAD_DIM, rows), F32)],
        compiler_params=_params(("parallel", "parallel", "arbitrary")),
        name="moba_attention",
    )(q.reshape(batch, seqlen, ATTN_WIDTH), kv3, kv3, causal_bias)


def _dil_combine_kernel(o0_ref, o1_ref, o2_ref, l0_ref, l1_ref, l2_ref, o_ref, *, n_rep):
    l0, l1, l2 = l0_ref[0, 0], l1_ref[0, 0], l2_ref[0, 0]
    top = jnp.maximum(jnp.maximum(l0, l1), l2)
    e0, e1, e2 = jnp.exp(l0 - top), jnp.exp(l1 - top), jnp.exp(l2 - top)
    den = e0 + e1 + e2
    a0, a1, a2 = e0 / den, e1 / den, e2 / den
    for r in range(n_rep):
        cols = slice(r * HEAD_DIM, (r + 1) * HEAD_DIM)
        mix = (a0[:, r:r + 1] * o0_ref[0, :, cols] + a1[:, r:r + 1] * o1_ref[0, :, cols]
               + a2[:, r:r + 1] * o2_ref[0, :, cols])
        o_ref[0, :, cols] = mix.astype(o_ref.dtype)


def dilated_combine(outs, lses, batch, seqlen):
    n_rep = N_HEADS // DIL_KV_HEADS
    tq = LANES
    qw = n_rep * HEAD_DIM
    o_spec = pl.BlockSpec((1, tq, qw), lambda b, g, i: (b, i, g))
    l_spec = pl.BlockSpec((1, 1, tq, LANES), lambda b, g, i: (b, g, i, 0))
    return pl.pallas_call(
        functools.partial(_dil_combine_kernel, n_rep=n_rep),
        grid=(batch, DIL_KV_HEADS, seqlen // tq),
        in_specs=[o_spec] * 3 + [l_spec] * 3,
        out_specs=o_spec,
        out_shape=jax.ShapeDtypeStruct((batch, seqlen, ATTN_WIDTH), BF16),
        compiler_params=_params(("parallel", "parallel", "parallel")),
        name="dilated_combine",
    )(*[o.reshape(batch, seqlen, ATTN_WIDTH) for o in outs], *lses)


def nsa_mixer(u, h, w_in, cmp_pos, k_w1, k_w2, v_w1, v_w2, w_out, tables, batch, seqlen):
    kvw = NSA_KV_HEADS * HEAD_DIM
    c0 = ATTN_WIDTH
    q = matmul(u, w_in[:, :c0].astype(BF16), BF16)
    kcvc = matmul(u, w_in[:, c0:c0 + 2 * kvw].astype(BF16), F32)
    kvsw = matmul(u, w_in[:, c0 + 2 * kvw:c0 + 6 * kvw].astype(BF16), BF16)
    gates = matmul(u, w_in[:, c0 + 6 * kvw:].astype(BF16), F32)
    kcmp, vcmp = nsa_compress(kcvc, cmp_pos, k_w1, k_w2, v_w1, v_w2, batch, seqlen)
    o_cmp, sel = nsa_cmp_attention(q, kcmp, vcmp, tables["cmp"], batch, seqlen)
    o_slc = nsa_selected_attention(q, kvsw, 0, kvw, sel, tables["causal"], batch, seqlen)
    o_win = banded_attention(q, 0, kvsw, 2 * kvw, 3 * kvw, tables["nsa_win"], batch=batch, seqlen=seqlen,
                             n_kv=NSA_KV_HEADS, dil=1, max_dist=NSA_WINDOW - 1)
    o = nsa_gate_combine(gates, o_cmp.reshape(batch * seqlen, ATTN_WIDTH),
                         o_slc.reshape(batch * seqlen, ATTN_WIDTH), o_win)
    return matmul_residual(o, w_out.astype(BF16), h, 1.0)


def dilated_mixer(u, h, w_in, w_out, tables, batch, seqlen):
    kvw = DIL_KV_HEADS * HEAD_DIM
    group = ATTN_WIDTH + 2 * kvw
    proj = matmul(u, w_in.astype(BF16), BF16)
    outs, lses = [], []
    for gi, (window, dil) in enumerate(DIL_PAIRS):
        off = gi * group
        o, lse = banded_attention(proj, off, proj, off + ATTN_WIDTH, off + ATTN_WIDTH + kvw,
                                  tables["dil%d" % dil], batch=batch, seqlen=seqlen, n_kv=DIL_KV_HEADS,
                                  dil=dil, max_dist=window // dil, want_lse=True)
        outs.append(o)
        lses.append(lse)
    o = dilated_combine(outs, lses, batch, seqlen)
    return matmul_residual(o.reshape(batch * seqlen, ATTN_WIDTH), w_out.astype(BF16), h, 1.0)


def moba_mixer(u, h, w_in, w_out, tables, batch, seqlen):
    q = matmul(u, w_in[:, :ATTN_WIDTH].astype(BF16), BF16)
    kv = matmul(u, w_in[:, ATTN_WIDTH:].astype(BF16), F32)
    o = moba_attention(q, kv, tables["causal"], batch, seqlen)
    return matmul_residual(o.reshape(batch * seqlen, ATTN_WIDTH), w_out.astype(BF16), h, 1.0)


def swa_mixer(u, h, w_in, sinks, w_out, tables, batch, seqlen):
    kvw = SWA_KV_HEADS * HEAD_DIM
    proj = matmul(u, w_in.astype(BF16), BF16)
    o = banded_attention(proj, 0, proj, ATTN_WIDTH, ATTN_WIDTH + kvw, tables["swa"], batch=batch,
                         seqlen=seqlen, n_kv=SWA_KV_HEADS, dil=1, max_dist=SWA_WINDOW - 1, sinks=sinks,
                         out_dtype=BF16)
    return matmul_residual(o, w_out.astype(BF16), h, 1.0)


def _band_table_for(rel_table, seqlen, dil, max_dist):
    sub = seqlen // dil
    tq = math.gcd(sub, BAND_BLOCK)
    n_prev = min(-(-max_dist // tq), sub // tq - 1)
    return band_bias_table(rel_table, tq, (n_prev + 1) * tq, n_prev * tq, max_dist, dil)


def kernel(x, rel_table, ffn1_norm, ffn1_w_gate, ffn1_w_up, ffn1_w_down, mix_norm, ffn2_norm, ffn2_w_gate, ffn2_w_up, ffn2_w_down, final_norm, nsa_w_in, nsa_cmp_pos, nsa_cmp_k_w1, nsa_cmp_k_w2, nsa_cmp_v_w1, nsa_cmp_v_w2, nsa_w_out, dil_w_in, dil_w_out, moba_w_in, moba_w_out, swa_w_in, swa_sinks, swa_w_out):
    batch, seqlen, d_model = x.shape
    depth = ffn1_norm.shape[0]
    n_mixers = 4
    h = x.reshape(batch * seqlen, d_model)

    tables = {
        "causal": causal_bias_table(rel_table, SEL_TILE, seqlen // SEL_TILE),
        "cmp": cmp_bias_table(rel_table, seqlen, (seqlen - NSA_CMP_LEN) // NSA_CMP_STRIDE + 1),
        "nsa_win": _band_table_for(rel_table, seqlen, 1, NSA_WINDOW - 1),
        "swa": _band_table_for(rel_table, seqlen, 1, SWA_WINDOW - 1),
    }
    for window, dil in DIL_PAIRS:
        tables["dil%d" % dil] = _band_table_for(rel_table, seqlen, dil, window // dil)

    for i in range(depth):
        h = ffn_half_step(h, ffn1_norm[i], ffn1_w_gate[i], ffn1_w_up[i], ffn1_w_down[i])
        u = rms_norm(h, mix_norm[i], BF16)
        m, j = i % n_mixers, i // n_mixers
        if m == 0:
            h = nsa_mixer(u, h, nsa_w_in[j], nsa_cmp_pos[j], nsa_cmp_k_w1[j], nsa_cmp_k_w2[j],
                          nsa_cmp_v_w1[j], nsa_cmp_v_w2[j], nsa_w_out[j], tables, batch, seqlen)
        elif m == 1:
            h = dilated_mixer(u, h, dil_w_in[j], dil_w_out[j], tables, batch, seqlen)
        elif m == 2:
            h = moba_mixer(u, h, moba_w_in[j], moba_w_out[j], tables, batch, seqlen)
        else:
            h = swa_mixer(u, h, swa_w_in[j], swa_sinks[j], swa_w_out[j], tables, batch, seqlen)
        h = ffn_half_step(h, ffn2_norm[i], ffn2_w_gate[i], ffn2_w_up[i], ffn2_w_down[i])
    return rms_norm(h, final_norm, x.dtype).reshape(batch, seqlen, d_model)
```

```python
import functools
import math

import numpy as np
import jax
import jax.numpy as jnp
from jax import lax
from jax.experimental import pallas as pl
from jax.experimental.pallas import tpu as pltpu

HEAD_DIM = 128
N_HEADS = 32
ATTN_WIDTH = N_HEADS * HEAD_DIM
RMS_EPS = 1e-6
REL_BUCKETS = 32
REL_MAX_DIST = 2048
BAND_BLOCK = 128
NSA_KV_HEADS = 4
NSA_CMP_LEN = 32
NSA_CMP_STRIDE = 16
NSA_SEL_LEN = 64
NSA_SEL_TOPN = 16
NSA_WINDOW = 512
DIL_PAIRS = ((128, 1), (512, 4), (2048, 16))
DIL_KV_HEADS = 8
MOBA_BLOCK = 256
MOBA_TOPK = 3
MOBA_KV_HEADS = 8
SWA_WINDOW = 128
SWA_KV_HEADS = 4
ATTN_SCALE = HEAD_DIM ** -0.5
NEG_INF = -1e30
MASKED_BELOW = -5e29
TINY = 1e-20
FORCED_SCORE = 1e9

LANES = 128
SUBLANES = 8
BF16_ROWS = 16
SEL_TILE = 256
VMEM_LIMIT = 56 * 1024 * 1024

F32 = jnp.float32
BF16 = jnp.bfloat16


def _params(semantics):
    return pltpu.CompilerParams(dimension_semantics=semantics, vmem_limit_bytes=VMEM_LIMIT)


def _dot_nt(a, b):
    return lax.dot_general(a, b, (((1,), (1,)), ((), ())), preferred_element_type=F32)


def _dot_tn(a, b):
    return lax.dot_general(a, b, (((0,), (0,)), ((), ())), preferred_element_type=F32)


def _stack_heads(q, n_heads):
    return jnp.concatenate([q[:, r * HEAD_DIM:(r + 1) * HEAD_DIM] for r in range(n_heads)], axis=0)


def _store_heads(o_ref, o_t, n_heads, tq):
    for r in range(n_heads):
        o_ref[0, :, r * HEAD_DIM:(r + 1) * HEAD_DIM] = o_t[:, r * tq:(r + 1) * tq].T.astype(o_ref.dtype)


def _rms_kernel(x_ref, g_ref, o_ref):
    x = x_ref[...]
    y = x * lax.rsqrt(jnp.mean(x * x, axis=-1, keepdims=True) + RMS_EPS)
    o_ref[...] = (y * g_ref[...]).astype(o_ref.dtype)


def rms_norm(x, gain, out_dtype):
    m, d = x.shape
    tm = 256
    return pl.pallas_call(
        _rms_kernel,
        grid=(m // tm,),
        in_specs=[pl.BlockSpec((tm, d), lambda i: (i, 0)), pl.BlockSpec((1, d), lambda i: (0, 0))],
        out_specs=pl.BlockSpec((tm, d), lambda i: (i, 0)),
        out_shape=jax.ShapeDtypeStruct((m, d), out_dtype),
        compiler_params=_params(("parallel",)),
        name="rms_norm",
    )(x, gain.reshape(1, d))


MM_VMEM_BUDGET = 46 * 1024 * 1024


def _mm_kernel(a_ref, b_ref, o_ref, *scratch, dil):
    res = jnp.dot(a_ref[...], b_ref[...].astype(BF16), preferred_element_type=F32)
    if dil == 1:
        o_ref[...] = res.astype(o_ref.dtype)
        return
    scr_ref, = scratch
    tm, tn = res.shape
    for s in range(tn // LANES):
        scr_ref[s] = res[:, s * LANES:(s + 1) * LANES]
    for rho in range(dil):
        for s in range(tn // LANES):
            o_ref[rho, :, s * LANES:(s + 1) * LANES] = (
                scr_ref[s, pl.ds(rho, tm // dil, stride=dil), :].astype(o_ref.dtype))


def _mm_res_kernel(a_ref, b_ref, r_ref, o_ref, *, scale):
    o_ref[...] = r_ref[...] + scale * jnp.dot(a_ref[...], b_ref[...].astype(BF16), preferred_element_type=F32)


def _mm_swiglu_kernel(a_ref, bg_ref, bu_ref, o_ref):
    a = a_ref[...]
    gate = jnp.dot(a, bg_ref[...].astype(BF16), preferred_element_type=F32)
    up = jnp.dot(a, bu_ref[...].astype(BF16), preferred_element_type=F32)
    o_ref[...] = (jax.nn.silu(gate) * up).astype(o_ref.dtype)


def _mm_tiles(m, k, n, n_weights, io_bytes):
    tm = min(m, 1024)
    if n < LANES:
        return tm, n
    for tn in (512, 256, 128):
        need = (2 * tm * k * 2 + n_weights * (2 * k * tn * 4 + k * tn * 2) + 2 * tm * tn * io_bytes
                + n_weights * tm * tn * 4)
        if n % tn == 0 and need <= MM_VMEM_BUDGET:
            return tm, tn
    raise ValueError("no matmul tile fits VMEM")


def _weight_spec(w, layer, col_off, tn):
    assert col_off % tn == 0
    return pl.BlockSpec((None, w.shape[1], tn), lambda i, j: (layer, 0, col_off // tn + j))


def matmul(a, w, layer, col_off, n, out_dtype, *, dil=1, batch=1):
    m, k = a.shape
    tm, tn = _mm_tiles(m, k, n, 1, jnp.dtype(out_dtype).itemsize)
    in_specs = [pl.BlockSpec((tm, k), lambda i, j: (i, 0)), _weight_spec(w, layer, col_off, tn)]
    if dil == 1:
        out_specs = pl.BlockSpec((tm, tn), lambda i, j: (i, j))
        out_shape = jax.ShapeDtypeStruct((m, n), out_dtype)
        scratch = []
    else:
        per_batch = m // batch // tm
        assert m % (batch * tm) == 0 and tm % (dil * BF16_ROWS) == 0 and tn % LANES == 0
        out_specs = pl.BlockSpec((None, dil, tm // dil, tn), lambda i, j: (i // per_batch, 0, i % per_batch, j))
        out_shape = jax.ShapeDtypeStruct((batch, dil, m // batch // dil, n), out_dtype)
        scratch = [pltpu.VMEM((tn // LANES, tm, LANES), F32)]
    return pl.pallas_call(
        functools.partial(_mm_kernel, dil=dil),
        grid=(m // tm, n // tn),
        in_specs=in_specs,
        out_specs=out_specs,
        out_shape=out_shape,
        scratch_shapes=scratch,
        compiler_params=_params(("parallel", "arbitrary")),
        name="matmul",
    )(a, w)


def matmul_residual(a, w, layer, res, scale):
    m, k = a.shape
    n = w.shape[2]
    tm, tn = _mm_tiles(m, k, n, 1, 8)
    return pl.pallas_call(
        functools.partial(_mm_res_kernel, scale=scale),
        grid=(m // tm, n // tn),
        in_specs=[pl.BlockSpec((tm, k), lambda i, j: (i, 0)),
                  _weight_spec(w, layer, 0, tn),
                  pl.BlockSpec((tm, tn), lambda i, j: (i, j))],
        out_specs=pl.BlockSpec((tm, tn), lambda i, j: (i, j)),
        out_shape=jax.ShapeDtypeStruct((m, n), F32),
        compiler_params=_params(("parallel", "arbitrary")),
        name="matmul_residual",
    )(a, w, res)


def matmul_swiglu(a, wg, wu, layer):
    m, k = a.shape
    n = wg.shape[2]
    tm, tn = _mm_tiles(m, k, n, 2, 2)
    return pl.pallas_call(
        _mm_swiglu_kernel,
        grid=(m // tm, n // tn),
        in_specs=[pl.BlockSpec((tm, k), lambda i, j: (i, 0)),
                  _weight_spec(wg, layer, 0, tn),
                  _weight_spec(wu, layer, 0, tn)],
        out_specs=pl.BlockSpec((tm, tn), lambda i, j: (i, j)),
        out_shape=jax.ShapeDtypeStruct((m, n), BF16),
        compiler_params=_params(("parallel", "arbitrary")),
        name="matmul_swiglu",
    )(a, wg, wu)


def ffn_half_step(h, gain, w_gate, w_up, w_down, layer):
    u = rms_norm(h, gain, BF16)
    act = matmul_swiglu(u, w_gate, w_up, layer)
    return matmul_residual(act, w_down, layer, h, 0.5)


def _t5_bias_values(dist, tab_ref, h):
    n = jnp.maximum(dist, 0)
    exact = REL_BUCKETS // 2
    nf = jnp.maximum(n, 1).astype(F32)
    large = exact + (jnp.log(nf / exact) * ((REL_BUCKETS - exact) / math.log(REL_MAX_DIST / exact))).astype(jnp.int32)
    bucket = jnp.where(n < exact, n, jnp.minimum(large, REL_BUCKETS - 1))
    level = [tab_ref[b, h] for b in range(REL_BUCKETS)]
    bit = 1
    while len(level) > 1:
        odd = (bucket & bit) != 0
        level = [jnp.where(odd, level[2 * t + 1], level[2 * t]) for t in range(len(level) // 2)]
        bit *= 2
    return level[0]


def _bias_table_kernel(tab_ref, o_ref, *, head_axis, blk_axis, base0, base_step, key_stride,
                       max_dist, n_valid_keys, dist_scale):
    h = pl.program_id(head_axis)
    blk = pl.program_id(blk_axis)
    n_keys, n_qry = o_ref.shape[-2:]
    key = lax.broadcasted_iota(jnp.int32, (n_keys, n_qry), 0)
    qry = lax.broadcasted_iota(jnp.int32, (n_keys, n_qry), 1)
    dist = base0 + blk * base_step + qry - key * key_stride
    valid = (dist >= 0) & (dist <= max_dist) & (key < n_valid_keys)
    bias = _t5_bias_values(dist * dist_scale, tab_ref, h)
    o_ref[...] = jnp.where(valid, bias, NEG_INF).reshape(o_ref.shape)


def band_bias_table(rel_table, tq, span, pad, max_dist, dist_scale):
    kern = functools.partial(_bias_table_kernel, head_axis=0, blk_axis=1, base0=pad, base_step=0,
                             key_stride=1, max_dist=max_dist, n_valid_keys=span, dist_scale=dist_scale)
    return pl.pallas_call(
        kern,
        grid=(N_HEADS, 1),
        in_specs=[pl.BlockSpec(memory_space=pltpu.SMEM)],
        out_specs=pl.BlockSpec((span, tq), lambda h, j: (0, h)),
        out_shape=jax.ShapeDtypeStruct((span, N_HEADS * tq), F32),
        compiler_params=_params(("parallel", "arbitrary")),
        name="band_bias_table",
    )(rel_table)


def causal_bias_table(rel_table, tile, n_cls):
    kern = functools.partial(_bias_table_kernel, head_axis=1, blk_axis=0, base0=0, base_step=tile,
                             key_stride=1, max_dist=2 ** 30, n_valid_keys=tile, dist_scale=1)
    return pl.pallas_call(
        kern,
        grid=(n_cls, N_HEADS),
        in_specs=[pl.BlockSpec(memory_space=pltpu.SMEM)],
        out_specs=pl.BlockSpec((1, tile, tile), lambda c, h: (c, 0, h)),
        out_shape=jax.ShapeDtypeStruct((n_cls, tile, N_HEADS * tile), F32),
        compiler_params=_params(("parallel", "arbitrary")),
        name="causal_bias_table",
    )(rel_table)


def cmp_bias_table(rel_table, seqlen, n_cmp):
    tq = LANES
    kern = functools.partial(_bias_table_kernel, head_axis=1, blk_axis=0,
                             base0=-(NSA_CMP_LEN - 1), base_step=tq, key_stride=NSA_CMP_STRIDE,
                             max_dist=2 ** 30, n_valid_keys=n_cmp, dist_scale=1)
    return pl.pallas_call(
        kern,
        grid=(seqlen // tq, N_HEADS),
        in_specs=[pl.BlockSpec(memory_space=pltpu.SMEM)],
        out_specs=pl.BlockSpec((1, LANES, tq), lambda i, h: (i, 0, h)),
        out_shape=jax.ShapeDtypeStruct((seqlen // tq, LANES, N_HEADS * tq), F32),
        compiler_params=_params(("parallel", "arbitrary")),
        name="cmp_bias_table",
    )(rel_table)


def _banded_kernel(*refs, n_rep, tq, n_prev, seq, has_sink, want_lse):
    q_ref, k_ref, v_ref, bias_ref = refs[:4]
    pos = 4
    sink_ref = None
    if has_sink:
        sink_ref = refs[pos]
        pos += 1
    o_ref = refs[pos]
    pos += 1
    lse_ref = None
    if want_lse:
        lse_ref = refs[pos]
        pos += 1
    kpad_ref, vpad_ref = refs[pos:pos + 2]

    g = pl.program_id(2)
    i = pl.program_id(3)
    pad = n_prev * tq
    span = pad + tq
    rows = n_rep * tq

    @pl.when(i == 0)
    def _():
        if pad:
            kpad_ref[0:pad, :] = jnp.zeros((pad, HEAD_DIM), BF16)
            vpad_ref[0:pad, :] = jnp.zeros((pad, HEAD_DIM), BF16)
        kpad_ref[pad:pad + seq, :] = k_ref[0]
        vpad_ref[pad:pad + seq, :] = v_ref[0]

    start = pl.multiple_of(i * tq, tq)
    ks = kpad_ref[pl.ds(start, span), :]
    vs = vpad_ref[pl.ds(start, span), :]
    qs = _stack_heads(q_ref[0], n_rep)
    lt = _dot_nt(ks, qs) * ATTN_SCALE + bias_ref[...]
    if pad:
        key = lax.broadcasted_iota(jnp.int32, (span, rows), 0)
        lt = jnp.where(key >= pad - i * tq, lt, NEG_INF)
    m = jnp.max(lt, axis=0, keepdims=True)
    sink = None
    if has_sink:
        sink = jnp.concatenate(
            [jnp.full((1, tq), sink_ref[g * n_rep + r], F32) for r in range(n_rep)], axis=1)
        m = jnp.maximum(m, sink)
    p = jnp.exp(lt - m)
    s = jnp.sum(p, axis=0, keepdims=True)
    if has_sink:
        s = s + jnp.exp(sink - m)
    o_t = _dot_tn(vs, p.astype(BF16))
    s = jnp.maximum(s, TINY)
    _store_heads(o_ref, o_t / s, n_rep, tq)
    if want_lse:
        lse = m + jnp.log(s)
        head = lax.broadcasted_iota(jnp.int32, (LANES, tq), 0)
        tile = jnp.zeros((LANES, tq), F32)
        for r in range(n_rep):
            tile = jnp.where(head == r, lse[:, r * tq:(r + 1) * tq], tile)
        lse_ref[0, 0] = tile.T


def banded_attention(q_arr, q_off, kv_arr, k_off, v_off, bias, *, batch, seqlen, n_kv, dil,
                     max_dist, sinks=None, want_lse=False, out_dtype=F32):
    n_rep = N_HEADS // n_kv
    sub = seqlen // dil
    tq = math.gcd(sub, BAND_BLOCK)
    n_blk = sub // tq
    n_prev = min(-(-max_dist // tq), n_blk - 1)
    span = (n_prev + 1) * tq
    qw = n_rep * HEAD_DIM
    assert q_arr.shape[:3] == kv_arr.shape[:3] == (batch, dil, sub)
    assert q_off % qw == 0 and bias.shape == (span, N_HEADS * tq)

    in_specs = [
        pl.BlockSpec((None, 1, tq, qw), lambda b, rho, g, i: (b, rho, i, q_off // qw + g)),
        pl.BlockSpec((None, 1, sub, HEAD_DIM), lambda b, rho, g, i: (b, rho, 0, k_off // HEAD_DIM + g)),
        pl.BlockSpec((None, 1, sub, HEAD_DIM), lambda b, rho, g, i: (b, rho, 0, v_off // HEAD_DIM + g)),
        pl.BlockSpec((span, n_rep * tq), lambda b, rho, g, i: (0, g)),
    ]
    args = [q_arr, kv_arr, kv_arr, bias]
    if sinks is not None:
        in_specs.append(pl.BlockSpec(memory_space=pltpu.SMEM))
        args.append(sinks)
    out_specs = [pl.BlockSpec((None, 1, tq, qw), lambda b, rho, g, i: (b, rho, i, g))]
    out_shape = [jax.ShapeDtypeStruct((batch, dil, sub, ATTN_WIDTH), out_dtype)]
    if want_lse:
        out_specs.append(pl.BlockSpec((None, 1, 1, tq, LANES), lambda b, rho, g, i: (b, g, rho, i, 0)))
        out_shape.append(jax.ShapeDtypeStruct((batch, n_kv, dil, sub, LANES), F32))
    kern = functools.partial(_banded_kernel, n_rep=n_rep, tq=tq, n_prev=n_prev, seq=sub,
                             has_sink=sinks is not None, want_lse=want_lse)
    outs = pl.pallas_call(
        kern,
        grid=(batch, dil, n_kv, n_blk),
        in_specs=in_specs,
        out_specs=out_specs,
        out_shape=out_shape,
        scratch_shapes=[pltpu.VMEM((n_prev * tq + sub, HEAD_DIM), BF16),
                        pltpu.VMEM((n_prev * tq + sub, HEAD_DIM), BF16)],
        compiler_params=_params(("parallel", "parallel", "parallel", "arbitrary")),
        name="banded_attention",
    )(*args)
    return tuple(outs) if want_lse else outs[0]


def _flash_init(m_ref, s_ref, acc_ref):
    m_ref[...] = jnp.full(m_ref.shape, NEG_INF, F32)
    s_ref[...] = jnp.zeros(s_ref.shape, F32)
    acc_ref[...] = jnp.zeros(acc_ref.shape, F32)


def _flash_step(lt, v_blk, m_ref, s_ref, acc_ref):
    m_prev = m_ref[...]
    m_new = jnp.maximum(m_prev, jnp.max(lt, axis=0, keepdims=True))
    alpha = jnp.exp(m_prev - m_new)
    p = jnp.exp(lt - m_new)
    s_ref[...] = alpha * s_ref[...] + jnp.sum(p, axis=0, keepdims=True)
    acc_ref[...] = alpha * acc_ref[...] + _dot_tn(v_blk, p.astype(BF16))
    m_ref[...] = m_new


def _flash_finish(o_ref, s_ref, acc_ref, n_rep, tq):
    _store_heads(o_ref, acc_ref[...] / jnp.maximum(s_ref[...], TINY), n_rep, tq)


def _first_rank(score, n_cand):
    idx = lax.broadcasted_iota(jnp.int32, score.shape, 0)
    rank = jnp.zeros(score.shape, F32)
    for jp in range(n_cand):
        other = score[jp:jp + 1, :]
        ahead = jnp.where(other > score, 1.0, jnp.where(other == score, jnp.where(idx > jp, 1.0, 0.0), 0.0))
        rank = rank + ahead
    return rank


def _nsa_cmp_kernel(x_ref, pos_ref, w1k_ref, w2k_ref, w1v_ref, w2v_ref, ko_ref, vo_ref):
    width = 2 * NSA_KV_HEADS * HEAD_DIM
    for kv, (w1_ref, w2_ref, o_ref) in enumerate(((w1k_ref, w2k_ref, ko_ref), (w1v_ref, w2v_ref, vo_ref))):
        for g in range(NSA_KV_HEADS):
            off = kv * NSA_KV_HEADS * HEAD_DIM + g * HEAD_DIM
            chunk = jnp.concatenate(
                [x_ref[0, :, l * width + off:l * width + off + HEAD_DIM] for l in range(NSA_CMP_STRIDE)], axis=1)
            first = jnp.dot((chunk + pos_ref[0:1, :]).astype(BF16), w1_ref[0], preferred_element_type=F32)
            second = jnp.dot((chunk + pos_ref[1:2, :]).astype(BF16), w1_ref[1], preferred_element_type=F32)
            hidden = jax.nn.gelu(first + pltpu.roll(second, second.shape[0] - 1, axis=0))
            o_ref[0, g] = jnp.dot(hidden.astype(BF16), w2_ref[...], preferred_element_type=F32).astype(o_ref.dtype)


def nsa_compress(kcvc, cmp_pos, k_w1, k_w2, v_w1, v_w2, batch, seqlen):
    n_chunk = seqlen // NSA_CMP_STRIDE
    width = 2 * NSA_KV_HEADS * HEAD_DIM
    half = NSA_CMP_STRIDE * HEAD_DIM
    x = kcvc.reshape(batch, n_chunk, NSA_CMP_STRIDE * width)
    out = jax.ShapeDtypeStruct((batch, NSA_KV_HEADS, n_chunk, HEAD_DIM), BF16)
    full = lambda shape: pl.BlockSpec(shape, lambda b: (0,) * len(shape))
    return pl.pallas_call(
        _nsa_cmp_kernel,
        grid=(batch,),
        in_specs=[pl.BlockSpec((1, n_chunk, NSA_CMP_STRIDE * width), lambda b: (b, 0, 0)),
                  full((2, half)), full((2, half, HEAD_DIM)), full((HEAD_DIM, HEAD_DIM)),
                  full((2, half, HEAD_DIM)), full((HEAD_DIM, HEAD_DIM))],
        out_specs=[pl.BlockSpec((1, NSA_KV_HEADS, n_chunk, HEAD_DIM), lambda b: (b, 0, 0, 0))] * 2,
        out_shape=[out, out],
        compiler_params=_params(("parallel",)),
        name="nsa_compress",
    )(x, cmp_pos.reshape(2, half), k_w1.reshape(2, half, HEAD_DIM).astype(BF16), k_w2.astype(BF16),
      v_w1.reshape(2, half, HEAD_DIM).astype(BF16), v_w2.astype(BF16))


def _nsa_cmp_attn_kernel(q_ref, kc_ref, vc_ref, bias_ref, c2s_ref, o_ref, sel_ref, *, n_rep, tq, n_sel_blk):
    i = pl.program_id(2)
    qs = _stack_heads(q_ref[0], n_rep)
    bias = bias_ref[0]
    valid = bias > MASKED_BELOW
    lt = jnp.where(valid, _dot_nt(kc_ref[0, 0], qs) * ATTN_SCALE + bias, NEG_INF)
    m = jnp.max(lt, axis=0, keepdims=True)
    p = jnp.where(valid, jnp.exp(lt - m), 0.0)
    s = jnp.sum(p, axis=0, keepdims=True)
    p_cmp = p / jnp.maximum(s, TINY)
    _store_heads(o_ref, _dot_tn(vc_ref[0, 0], p_cmp.astype(BF16)), n_rep, tq)

    p_sum = p_cmp[:, 0:tq]
    for r in range(1, n_rep):
        p_sum = p_sum + p_cmp[:, r * tq:(r + 1) * tq]
    imp = jnp.dot(c2s_ref[...], p_sum.astype(BF16), preferred_element_type=F32)[0:n_sel_blk]
    blk = lax.broadcasted_iota(jnp.int32, (n_sel_blk, tq), 0)
    tpos = i * tq + lax.broadcasted_iota(jnp.int32, (n_sel_blk, tq), 1)
    cur = tpos // NSA_SEL_LEN
    forced = (blk == 0) | (blk == cur) | (blk == cur - 1)
    score = jnp.where(forced, FORCED_SCORE, jnp.where(blk * NSA_SEL_LEN <= tpos, imp, NEG_INF))
    chosen = jnp.where(_first_rank(score, n_sel_blk) < min(NSA_SEL_TOPN, n_sel_blk), 1.0, 0.0)
    sel_ref[0, 0] = jnp.concatenate([chosen, jnp.zeros((LANES - n_sel_blk, tq), F32)], axis=0).astype(sel_ref.dtype)


def nsa_cmp_attention(q, kcmp, vcmp, bias, batch, seqlen):
    n_rep = N_HEADS // NSA_KV_HEADS
    tq = LANES
    qw = n_rep * HEAD_DIM
    n_sel_blk = seqlen // NSA_SEL_LEN
    n_cmp = (seqlen - NSA_CMP_LEN) // NSA_CMP_STRIDE + 1
    a, b = NSA_SEL_LEN // NSA_CMP_STRIDE, NSA_CMP_LEN // NSA_CMP_STRIDE
    w = np.zeros((LANES, LANES), np.float32)
    j = np.arange(n_sel_blk)
    for mm in range(a):
        for nn in range(b):
            ii = a * j + mm + nn - (b - 1)
            ok = (ii >= 0) & (ii < n_cmp)
            np.add.at(w, (j[ok], ii[ok]), 1.0)
    kern = functools.partial(_nsa_cmp_attn_kernel, n_rep=n_rep, tq=tq, n_sel_blk=n_sel_blk)
    return pl.pallas_call(
        kern,
        grid=(batch, NSA_KV_HEADS, seqlen // tq),
        in_specs=[pl.BlockSpec((1, tq, qw), lambda b_, g, i: (b_, i, g)),
                  pl.BlockSpec((1, 1, LANES, HEAD_DIM), lambda b_, g, i: (b_, g, 0, 0)),
                  pl.BlockSpec((1, 1, LANES, HEAD_DIM), lambda b_, g, i: (b_, g, 0, 0)),
                  pl.BlockSpec((1, LANES, n_rep * tq), lambda b_, g, i: (i, 0, g)),
                  pl.BlockSpec((LANES, LANES), lambda b_, g, i: (0, 0))],
        out_specs=[pl.BlockSpec((1, tq, qw), lambda b_, g, i: (b_, i, g)),
                   pl.BlockSpec((1, 1, LANES, tq), lambda b_, g, i: (b_, g, 0, i))],
        out_shape=[jax.ShapeDtypeStruct((batch, seqlen, ATTN_WIDTH), F32),
                   jax.ShapeDtypeStruct((batch, NSA_KV_HEADS, LANES, seqlen), BF16)],
        compiler_params=_params(("parallel", "parallel", "arbitrary")),
        name="nsa_cmp_attention",
    )(q.reshape(batch, seqlen, ATTN_WIDTH), kcmp, vcmp, bias, jnp.asarray(w, BF16))


def _nsa_sel_kernel(q_ref, k_ref, v_ref, sel_ref, bias_ref, exp_ref, o_ref, m_ref, s_ref, acc_ref, *, n_rep, tq):
    i = pl.program_id(2)
    qs = _stack_heads(q_ref[0], n_rep)
    sel = jnp.concatenate([sel_ref[0, 0]] * n_rep, axis=1)
    _flash_init(m_ref, s_ref, acc_ref)

    def body(cc, carry):
        c = i - cc
        start = pl.multiple_of(c * tq, tq)
        lt = _dot_nt(k_ref[0, pl.ds(start, tq), :], qs) * ATTN_SCALE + bias_ref[cc]
        key_on = jnp.dot(exp_ref[c], sel, preferred_element_type=F32)
        lt = jnp.where(key_on > 0.5, lt, NEG_INF)
        _flash_step(lt, v_ref[0, pl.ds(start, tq), :], m_ref, s_ref, acc_ref)
        return carry

    lax.fori_loop(0, i + 1, body, 0)
    _flash_finish(o_ref, s_ref, acc_ref, n_rep, tq)


def nsa_selected_attention(q, kv, k_off, v_off, sel, causal_bias, batch, seqlen):
    n_rep = N_HEADS // NSA_KV_HEADS
    tq = SEL_TILE
    qw = n_rep * HEAD_DIM
    n_blk = seqlen // tq
    per = tq // NSA_SEL_LEN
    expand = np.zeros((n_blk, tq, LANES), np.float32)
    for c in range(n_blk):
        for kk in range(tq):
            expand[c, kk, per * c + kk // NSA_SEL_LEN] = 1.0
    ckv = kv.shape[1]
    kv3 = kv.reshape(batch, seqlen, ckv)
    kern = functools.partial(_nsa_sel_kernel, n_rep=n_rep, tq=tq)
    rows = n_rep * tq
    return pl.pallas_call(
        kern,
        grid=(batch, NSA_KV_HEADS, n_blk),
        in_specs=[pl.BlockSpec((1, tq, qw), lambda b, g, i: (b, i, g)),
                  pl.BlockSpec((1, seqlen, HEAD_DIM), lambda b, g, i: (b, 0, k_off // HEAD_DIM + g)),
                  pl.BlockSpec((1, seqlen, HEAD_DIM), lambda b, g, i: (b, 0, v_off // HEAD_DIM + g)),
                  pl.BlockSpec((1, 1, LANES, tq), lambda b, g, i: (b, g, 0, i)),
                  pl.BlockSpec((n_blk, tq, rows), lambda b, g, i: (0, 0, g), pipeline_mode=pl.Buffered(1)),
                  pl.BlockSpec((n_blk, tq, LANES), lambda b, g, i: (0, 0, 0), pipeline_mode=pl.Buffered(1))],
        out_specs=pl.BlockSpec((1, tq, qw), lambda b, g, i: (b, i, g)),
        out_shape=jax.ShapeDtypeStruct((batch, seqlen, ATTN_WIDTH), F32),
        scratch_shapes=[pltpu.VMEM((1, rows), F32), pltpu.VMEM((1, rows), F32),
                        pltpu.VMEM((HEAD_DIM, rows), F32)],
        compiler_params=_params(("parallel", "parallel", "arbitrary")),
        name="nsa_selected_attention",
    )(q.reshape(batch, seqlen, ATTN_WIDTH), kv3, kv3, sel, causal_bias, jnp.asarray(expand, BF16))


def _nsa_gate_kernel(g_ref, oc_ref, os_ref, ow_ref, o_ref):
    gate = jax.nn.sigmoid(g_ref[...])
    for h in range(N_HEADS):
        cols = slice(h * HEAD_DIM, (h + 1) * HEAD_DIM)
        mix = (gate[:, 3 * h:3 * h + 1] * oc_ref[:, cols] + gate[:, 3 * h + 1:3 * h + 2] * os_ref[:, cols]
               + gate[:, 3 * h + 2:3 * h + 3] * ow_ref[:, cols])
        o_ref[:, cols] = mix.astype(o_ref.dtype)


def nsa_gate_combine(gates, o_cmp, o_slc, o_win):
    m = gates.shape[0]
    tm = 128
    wide = pl.BlockSpec((tm, ATTN_WIDTH), lambda i: (i, 0))
    return pl.pallas_call(
        _nsa_gate_kernel,
        grid=(m // tm,),
        in_specs=[pl.BlockSpec((tm, gates.shape[1]), lambda i: (i, 0)), wide, wide, wide],
        out_specs=wide,
        out_shape=jax.ShapeDtypeStruct((m, ATTN_WIDTH), BF16),
        compiler_params=_params(("parallel",)),
        name="nsa_gate_combine",
    )(gates, o_cmp, o_slc, o_win)


def _moba_kernel(q_ref, k_ref, v_ref, bias_ref, o_ref, kb_ref, vb_ref, km_ref, m_ref, s_ref, acc_ref,
                 *, n_rep, tq, n_blk):
    i = pl.program_id(2)
    rows = n_rep * tq

    @pl.when(i == 0)
    def _():
        k = k_ref[0]
        kb_ref[...] = k.astype(BF16)
        vb_ref[...] = v_ref[0].astype(BF16)
        slot = lax.broadcasted_iota(jnp.int32, (BF16_ROWS, HEAD_DIM), 0)
        means = jnp.zeros((BF16_ROWS, HEAD_DIM), F32)
        for j in range(n_blk):
            means = jnp.where(slot == j, jnp.mean(k[j * tq:(j + 1) * tq], axis=0, keepdims=True), means)
        km_ref[...] = means.astype(BF16)

    qs = _stack_heads(q_ref[0], n_rep)
    gate = _dot_nt(km_ref[...], qs)[0:n_blk]
    blk = lax.broadcasted_iota(jnp.int32, (n_blk, rows), 0)
    past = blk < i
    rank = _first_rank(jnp.where(past, gate, NEG_INF), n_blk)
    chosen = jnp.where(past, jnp.where(rank < min(MOBA_TOPK, max(n_blk - 1, 1)), 1.0, 0.0), 0.0)

    _flash_init(m_ref, s_ref, acc_ref)
    own = pl.multiple_of(i * tq, tq)
    lt = _dot_nt(kb_ref[pl.ds(own, tq), :], qs) * ATTN_SCALE + bias_ref[0]
    _flash_step(lt, vb_ref[pl.ds(own, tq), :], m_ref, s_ref, acc_ref)

    def body(cc, carry):
        c = i - cc
        start = pl.multiple_of(c * tq, tq)
        lt = _dot_nt(kb_ref[pl.ds(start, tq), :], qs) * ATTN_SCALE + bias_ref[cc]
        row_on = jnp.sum(jnp.where(blk == c, chosen, 0.0), axis=0, keepdims=True)
        lt = jnp.where(row_on > 0.5, lt, NEG_INF)
        _flash_step(lt, vb_ref[pl.ds(start, tq), :], m_ref, s_ref, acc_ref)
        return carry

    lax.fori_loop(1, i + 1, body, 0)
    _flash_finish(o_ref, s_ref, acc_ref, n_rep, tq)


def moba_attention(q, kv, causal_bias, batch, seqlen):
    n_rep = N_HEADS // MOBA_KV_HEADS
    tq = MOBA_BLOCK
    qw = n_rep * HEAD_DIM
    n_blk = seqlen // tq
    assert n_blk <= SUBLANES
    rows = n_rep * tq
    kv3 = kv.reshape(batch, seqlen, kv.shape[1])
    kern = functools.partial(_moba_kernel, n_rep=n_rep, tq=tq, n_blk=n_blk)
    return pl.pallas_call(
        kern,
        grid=(batch, MOBA_KV_HEADS, n_blk),
        in_specs=[pl.BlockSpec((1, tq, qw), lambda b, g, i: (b, i, g)),
                  pl.BlockSpec((1, seqlen, HEAD_DIM), lambda b, g, i: (b, 0, g)),
                  pl.BlockSpec((1, seqlen, HEAD_DIM), lambda b, g, i: (b, 0, MOBA_KV_HEADS + g)),
                  pl.BlockSpec((n_blk, tq, rows), lambda b, g, i: (0, 0, g), pipeline_mode=pl.Buffered(1))],
        out_specs=pl.BlockSpec((1, tq, qw), lambda b, g, i: (b, i, g)),
        out_shape=jax.ShapeDtypeStruct((batch, seqlen, ATTN_WIDTH), BF16),
        scratch_shapes=[pltpu.VMEM((seqlen, HEAD_DIM), BF16), pltpu.VMEM((seqlen, HEAD_DIM), BF16),
                        pltpu.VMEM((BF16_ROWS, HEAD_DIM), BF16),
                        pltpu.VMEM((1, rows), F32), pltpu.VMEM((1, rows), F32),
                        pltpu.VMEM((HEAD_DIM, rows), F32)],
        compiler_params=_params(("parallel", "parallel", "arbitrary")),
        name="moba_attention",
    )(q.reshape(batch, seqlen, ATTN_WIDTH), kv3, kv3, causal_bias)


def _dil_combine_kernel(*refs, n_rep, dils, tile):
    n_grp = len(dils)
    o_refs, l_refs, o_ref = refs[:n_grp], refs[n_grp:2 * n_grp], refs[2 * n_grp]
    scratch = list(refs[2 * n_grp + 1:])
    outs, lses = [], []
    for o_g, l_g, dil in zip(o_refs, l_refs, dils):
        if dil == 1:
            outs.append([o_g[0, :, r * HEAD_DIM:(r + 1) * HEAD_DIM] for r in range(n_rep)])
            lses.append(l_g[0])
            continue
        nat_o, nat_l = scratch.pop(0), scratch.pop(0)
        per = tile // dil
        for rho in range(dil):
            for r in range(n_rep):
                nat_o[r, pl.ds(rho, per, stride=dil), :] = o_g[rho, :, r * HEAD_DIM:(r + 1) * HEAD_DIM]
            nat_l[pl.ds(rho, per, stride=dil), :] = l_g[rho]
        outs.append([nat_o[r] for r in range(n_rep)])
        lses.append(nat_l[...])
    top = functools.reduce(jnp.maximum, lses)
    weights = [jnp.exp(l - top) for l in lses]
    den = functools.reduce(lambda x, y: x + y, weights)
    weights = [w / den for w in weights]
    for r in range(n_rep):
        mix = weights[0][:, r:r + 1] * outs[0][r]
        for w, o in zip(weights[1:], outs[1:]):
            mix = mix + w[:, r:r + 1] * o[r]
        o_ref[0, :, r * HEAD_DIM:(r + 1) * HEAD_DIM] = mix.astype(o_ref.dtype)


def dilated_combine(outs, lses, dils, batch, seqlen):
    n_rep = N_HEADS // DIL_KV_HEADS
    tile = SEL_TILE
    qw = n_rep * HEAD_DIM
    in_specs, scratch = [], []
    for dil in dils:
        in_specs.append(pl.BlockSpec((None, dil, tile // dil, qw), lambda b, g, i: (b, 0, i, g)))
    for dil in dils:
        in_specs.append(pl.BlockSpec((None, None, dil, tile // dil, LANES), lambda b, g, i: (b, g, 0, i, 0)))
        if dil > 1:
            scratch += [pltpu.VMEM((n_rep, tile, HEAD_DIM), F32), pltpu.VMEM((tile, LANES), F32)]
    return pl.pallas_call(
        functools.partial(_dil_combine_kernel, n_rep=n_rep, dils=tuple(dils), tile=tile),
        grid=(batch, DIL_KV_HEADS, seqlen // tile),
        in_specs=in_specs,
        out_specs=pl.BlockSpec((1, tile, qw), lambda b, g, i: (b, i, g)),
        out_shape=jax.ShapeDtypeStruct((batch, seqlen, ATTN_WIDTH), BF16),
        scratch_shapes=scratch,
        compiler_params=_params(("parallel", "parallel", "parallel")),
        name="dilated_combine",
    )(*outs, *lses)


def nsa_mixer(u, h, w_in, j, cmp_pos, k_w1, k_w2, v_w1, v_w2, w_out, tables, batch, seqlen):
    kvw = NSA_KV_HEADS * HEAD_DIM
    c0 = ATTN_WIDTH
    tokens = batch * seqlen
    q = matmul(u, w_in, j, 0, c0, BF16)
    kcvc = matmul(u, w_in, j, c0, 2 * kvw, F32)
    kvsw = matmul(u, w_in, j, c0 + 2 * kvw, 4 * kvw, BF16)
    gates = matmul(u, w_in[:, :, c0 + 6 * kvw:], j, 0, 3 * N_HEADS, F32)
    kcmp, vcmp = nsa_compress(kcvc, cmp_pos, k_w1, k_w2, v_w1, v_w2, batch, seqlen)
    o_cmp, sel = nsa_cmp_attention(q, kcmp, vcmp, tables["cmp"], batch, seqlen)
    o_slc = nsa_selected_attention(q, kvsw, 0, kvw, sel, tables["causal"], batch, seqlen)
    o_win = banded_attention(q.reshape(batch, 1, seqlen, c0), 0, kvsw.reshape(batch, 1, seqlen, 4 * kvw),
                             2 * kvw, 3 * kvw, tables["nsa_win"], batch=batch, seqlen=seqlen,
                             n_kv=NSA_KV_HEADS, dil=1, max_dist=NSA_WINDOW - 1)
    o = nsa_gate_combine(gates, o_cmp.reshape(tokens, ATTN_WIDTH), o_slc.reshape(tokens, ATTN_WIDTH),
                         o_win.reshape(tokens, ATTN_WIDTH))
    return matmul_residual(o, w_out, j, h, 1.0)


def dilated_mixer(u, h, w_in, j, w_out, tables, batch, seqlen):
    kvw = DIL_KV_HEADS * HEAD_DIM
    group = ATTN_WIDTH + 2 * kvw
    outs, lses, dils = [], [], []
    for gi, (window, dil) in enumerate(DIL_PAIRS):
        proj = matmul(u, w_in, j, gi * group, group, BF16, dil=dil, batch=batch)
        proj = proj.reshape(batch, dil, seqlen // dil, group)
        o, lse = banded_attention(proj, 0, proj, ATTN_WIDTH, ATTN_WIDTH + kvw, tables["dil%d" % dil],
                                  batch=batch, seqlen=seqlen, n_kv=DIL_KV_HEADS, dil=dil,
                                  max_dist=window // dil, want_lse=True)
        outs.append(o)
        lses.append(lse)
        dils.append(dil)
    o = dilated_combine(outs, lses, dils, batch, seqlen)
    return matmul_residual(o.reshape(batch * seqlen, ATTN_WIDTH), w_out, j, h, 1.0)


def moba_mixer(u, h, w_in, j, w_out, tables, batch, seqlen):
    q = matmul(u, w_in, j, 0, ATTN_WIDTH, BF16)
    kv = matmul(u, w_in, j, ATTN_WIDTH, 2 * MOBA_KV_HEADS * HEAD_DIM, F32)
    o = moba_attention(q, kv, tables["causal"], batch, seqlen)
    return matmul_residual(o.reshape(batch * seqlen, ATTN_WIDTH), w_out, j, h, 1.0)


def swa_mixer(u, h, w_in, j, sinks, w_out, tables, batch, seqlen):
    kvw = SWA_KV_HEADS * HEAD_DIM
    width = ATTN_WIDTH + 2 * kvw
    proj = matmul(u, w_in, j, 0, width, BF16).reshape(batch, 1, seqlen, width)
    o = banded_attention(proj, 0, proj, ATTN_WIDTH, ATTN_WIDTH + kvw, tables["swa"], batch=batch,
                         seqlen=seqlen, n_kv=SWA_KV_HEADS, dil=1, max_dist=SWA_WINDOW - 1, sinks=sinks,
                         out_dtype=BF16)
    return matmul_residual(o.reshape(batch * seqlen, ATTN_WIDTH), w_out, j, h, 1.0)


def _band_table_for(rel_table, seqlen, dil, max_dist):
    sub = seqlen // dil
    tq = math.gcd(sub, BAND_BLOCK)
    n_prev = min(-(-max_dist // tq), sub // tq - 1)
    return band_bias_table(rel_table, tq, (n_prev + 1) * tq, n_prev * tq, max_dist, dil)


def kernel(x, rel_table, ffn1_norm, ffn1_w_gate, ffn1_w_up, ffn1_w_down, mix_norm, ffn2_norm, ffn2_w_gate, ffn2_w_up, ffn2_w_down, final_norm, nsa_w_in, nsa_cmp_pos, nsa_cmp_k_w1, nsa_cmp_k_w2, nsa_cmp_v_w1, nsa_cmp_v_w2, nsa_w_out, dil_w_in, dil_w_out, moba_w_in, moba_w_out, swa_w_in, swa_sinks, swa_w_out):
    batch, seqlen, d_model = x.shape
    depth = ffn1_norm.shape[0]
    n_mixers = 4
    h = x.reshape(batch * seqlen, d_model)

    tables = {
        "causal": causal_bias_table(rel_table, SEL_TILE, seqlen // SEL_TILE),
        "cmp": cmp_bias_table(rel_table, seqlen, (seqlen - NSA_CMP_LEN) // NSA_CMP_STRIDE + 1),
        "nsa_win": _band_table_for(rel_table, seqlen, 1, NSA_WINDOW - 1),
        "swa": _band_table_for(rel_table, seqlen, 1, SWA_WINDOW - 1),
    }
    for window, dil in DIL_PAIRS:
        tables["dil%d" % dil] = _band_table_for(rel_table, seqlen, dil, window // dil)

    for i in range(depth):
        h = ffn_half_step(h, ffn1_norm[i], ffn1_w_gate, ffn1_w_up, ffn1_w_down, i)
        u = rms_norm(h, mix_norm[i], BF16)
        m, j = i % n_mixers, i // n_mixers
        if m == 0:
            h = nsa_mixer(u, h, nsa_w_in, j, nsa_cmp_pos[j], nsa_cmp_k_w1[j], nsa_cmp_k_w2[j],
                          nsa_cmp_v_w1[j], nsa_cmp_v_w2[j], nsa_w_out, tables, batch, seqlen)
        elif m == 1:
            h = dilated_mixer(u, h, dil_w_in, j, dil_w_out, tables, batch, seqlen)
        elif m == 2:
            h = moba_mixer(u, h, moba_w_in, j, moba_w_out, tables, batch, seqlen)
        else:
            h = swa_mixer(u, h, swa_w_in, j, swa_sinks[j], swa_w_out, tables, batch, seqlen)
        h = ffn_half_step(h, ffn2_norm[i], ffn2_w_gate, ffn2_w_up, ffn2_w_down, i)
    return rms_norm(h, final_norm, x.dtype).reshape(batch, seqlen, d_model)
```

```python
import functools
import math

import numpy as np
import jax
import jax.numpy as jnp
from jax import lax
from jax.experimental import pallas as pl
from jax.experimental.pallas import tpu as pltpu

HEAD_DIM = 128
N_HEADS = 32
ATTN_WIDTH = N_HEADS * HEAD_DIM
RMS_EPS = 1e-6
REL_BUCKETS = 32
REL_MAX_DIST = 2048
BAND_BLOCK = 128
NSA_KV_HEADS = 4
NSA_CMP_LEN = 32
NSA_CMP_STRIDE = 16
NSA_SEL_LEN = 64
NSA_SEL_TOPN = 16
NSA_WINDOW = 512
DIL_PAIRS = ((128, 1), (512, 4), (2048, 16))
DIL_KV_HEADS = 8
MOBA_BLOCK = 256
MOBA_TOPK = 3
MOBA_KV_HEADS = 8
SWA_WINDOW = 128
SWA_KV_HEADS = 4
ATTN_SCALE = HEAD_DIM ** -0.5
NEG_INF = -1e30
MASKED_BELOW = -5e29
TINY = 1e-20
FORCED_SCORE = 1e9

LANES = 128
SUBLANES = 8
BF16_ROWS = 16
SEL_TILE = 256
VMEM_LIMIT = 56 * 1024 * 1024

F32 = jnp.float32
BF16 = jnp.bfloat16


def _params(semantics):
    return pltpu.CompilerParams(dimension_semantics=semantics, vmem_limit_bytes=VMEM_LIMIT)


def _dot_nt(a, b):
    return lax.dot_general(a, b, (((1,), (1,)), ((), ())), preferred_element_type=F32)


def _dot_tn(a, b):
    return lax.dot_general(a, b, (((0,), (0,)), ((), ())), preferred_element_type=F32)


def _stack_heads(q, n_heads):
    return jnp.concatenate([q[:, r * HEAD_DIM:(r + 1) * HEAD_DIM] for r in range(n_heads)], axis=0)


def _store_heads(o_ref, o_t, n_heads, tq, col0=0):
    for r in range(n_heads):
        cols = slice(col0 + r * HEAD_DIM, col0 + (r + 1) * HEAD_DIM)
        o_ref[0, :, cols] = o_t[:, r * tq:(r + 1) * tq].T.astype(o_ref.dtype)


def _rms_kernel(x_ref, g_ref, o_ref):
    x = x_ref[...]
    y = x * lax.rsqrt(jnp.mean(x * x, axis=-1, keepdims=True) + RMS_EPS)
    o_ref[...] = (y * g_ref[...]).astype(o_ref.dtype)


def rms_norm(x, gain, out_dtype):
    m, d = x.shape
    tm = 256
    return pl.pallas_call(
        _rms_kernel,
        grid=(m // tm,),
        in_specs=[pl.BlockSpec((tm, d), lambda i: (i, 0)), pl.BlockSpec((1, d), lambda i: (0, 0))],
        out_specs=pl.BlockSpec((tm, d), lambda i: (i, 0)),
        out_shape=jax.ShapeDtypeStruct((m, d), out_dtype),
        compiler_params=_params(("parallel",)),
        name="rms_norm",
    )(x, gain.reshape(1, d))


MM_VMEM_BUDGET = 46 * 1024 * 1024


def _mm_kernel(a_ref, b_ref, o_ref, *scratch, dil):
    res = jnp.dot(a_ref[...], b_ref[...].astype(BF16), preferred_element_type=F32)
    if dil == 1:
        o_ref[...] = res.astype(o_ref.dtype)
        return
    scr_ref, = scratch
    tm, tn = res.shape
    for s in range(tn // LANES):
        scr_ref[s] = res[:, s * LANES:(s + 1) * LANES]
    for rho in range(dil):
        for s in range(tn // LANES):
            o_ref[rho, :, s * LANES:(s + 1) * LANES] = (
                scr_ref[s, pl.ds(rho, tm // dil, stride=dil), :].astype(o_ref.dtype))


def _mm_res_kernel(a_ref, b_ref, r_ref, o_ref, *, scale):
    o_ref[...] = r_ref[...] + scale * jnp.dot(a_ref[...], b_ref[...].astype(BF16), preferred_element_type=F32)


def _mm_swiglu_kernel(a_ref, bg_ref, bu_ref, o_ref):
    a = a_ref[...]
    gate = jnp.dot(a, bg_ref[...].astype(BF16), preferred_element_type=F32)
    up = jnp.dot(a, bu_ref[...].astype(BF16), preferred_element_type=F32)
    o_ref[...] = (jax.nn.silu(gate) * up).astype(o_ref.dtype)


def _mm_tiles(m, k, n, n_weights, io_bytes):
    tm = min(m, 1024)
    if n < LANES:
        return tm, n
    for tn in (512, 256, 128):
        need = (2 * tm * k * 2 + n_weights * (2 * k * tn * 4 + k * tn * 2) + 2 * tm * tn * io_bytes
                + n_weights * tm * tn * 4)
        if n % tn == 0 and need <= MM_VMEM_BUDGET:
            return tm, tn
    raise ValueError("no matmul tile fits VMEM")


def _weight_spec(w, layer, col_off, tn):
    assert col_off % tn == 0
    return pl.BlockSpec((None, w.shape[1], tn), lambda i, j: (layer, 0, col_off // tn + j))


def matmul(a, w, layer, col_off, n, out_dtype, *, dil=1, batch=1):
    m, k = a.shape
    tm, tn = _mm_tiles(m, k, n, 1, jnp.dtype(out_dtype).itemsize)
    in_specs = [pl.BlockSpec((tm, k), lambda i, j: (i, 0)), _weight_spec(w, layer, col_off, tn)]
    if dil == 1:
        out_specs = pl.BlockSpec((tm, tn), lambda i, j: (i, j))
        out_shape = jax.ShapeDtypeStruct((m, n), out_dtype)
        scratch = []
    else:
        per_batch = m // batch // tm
        assert m % (batch * tm) == 0 and tm % (dil * BF16_ROWS) == 0 and tn % LANES == 0
        out_specs = pl.BlockSpec((None, dil, tm // dil, tn), lambda i, j: (i // per_batch, 0, i % per_batch, j))
        out_shape = jax.ShapeDtypeStruct((batch, dil, m // batch // dil, n), out_dtype)
        scratch = [pltpu.VMEM((tn // LANES, tm, LANES), F32)]
    return pl.pallas_call(
        functools.partial(_mm_kernel, dil=dil),
        grid=(m // tm, n // tn),
        in_specs=in_specs,
        out_specs=out_specs,
        out_shape=out_shape,
        scratch_shapes=scratch,
        compiler_params=_params(("parallel", "arbitrary")),
        name="matmul",
    )(a, w)


def matmul_residual(a, w, layer, res, scale):
    m, k = a.shape
    n = w.shape[2]
    tm, tn = _mm_tiles(m, k, n, 1, 8)
    return pl.pallas_call(
        functools.partial(_mm_res_kernel, scale=scale),
        grid=(m // tm, n // tn),
        in_specs=[pl.BlockSpec((tm, k), lambda i, j: (i, 0)),
                  _weight_spec(w, layer, 0, tn),
                  pl.BlockSpec((tm, tn), lambda i, j: (i, j))],
        out_specs=pl.BlockSpec((tm, tn), lambda i, j: (i, j)),
        out_shape=jax.ShapeDtypeStruct((m, n), F32),
        compiler_params=_params(("parallel", "arbitrary")),
        name="matmul_residual",
    )(a, w, res)


def matmul_swiglu(a, wg, wu, layer):
    m, k = a.shape
    n = wg.shape[2]
    tm, tn = _mm_tiles(m, k, n, 2, 2)
    return pl.pallas_call(
        _mm_swiglu_kernel,
        grid=(m // tm, n // tn),
        in_specs=[pl.BlockSpec((tm, k), lambda i, j: (i, 0)),
                  _weight_spec(wg, layer, 0, tn),
                  _weight_spec(wu, layer, 0, tn)],
        out_specs=pl.BlockSpec((tm, tn), lambda i, j: (i, j)),
        out_shape=jax.ShapeDtypeStruct((m, n), BF16),
        compiler_params=_params(("parallel", "arbitrary")),
        name="matmul_swiglu",
    )(a, wg, wu)


def ffn_half_step(h, gain, w_gate, w_up, w_down, layer):
    u = rms_norm(h, gain, BF16)
    act = matmul_swiglu(u, w_gate, w_up, layer)
    return matmul_residual(act, w_down, layer, h, 0.5)


def _t5_bias_values(dist, tab_ref, h):
    n = jnp.maximum(dist, 0)
    exact = REL_BUCKETS // 2
    nf = jnp.maximum(n, 1).astype(F32)
    large = exact + (jnp.log(nf / exact) * ((REL_BUCKETS - exact) / math.log(REL_MAX_DIST / exact))).astype(jnp.int32)
    bucket = jnp.where(n < exact, n, jnp.minimum(large, REL_BUCKETS - 1))
    level = [tab_ref[b, h] for b in range(REL_BUCKETS)]
    bit = 1
    while len(level) > 1:
        odd = (bucket & bit) != 0
        level = [jnp.where(odd, level[2 * t + 1], level[2 * t]) for t in range(len(level) // 2)]
        bit *= 2
    return level[0]


def _bias_table_kernel(tab_ref, o_ref, *, head_axis, blk_axis, base0, base_step, key_stride,
                       max_dist, n_valid_keys, dist_scale):
    h = pl.program_id(head_axis)
    blk = pl.program_id(blk_axis)
    n_keys, n_qry = o_ref.shape[-2:]
    key = lax.broadcasted_iota(jnp.int32, (n_keys, n_qry), 0)
    qry = lax.broadcasted_iota(jnp.int32, (n_keys, n_qry), 1)
    dist = base0 + blk * base_step + qry - key * key_stride
    valid = (dist >= 0) & (dist <= max_dist) & (key < n_valid_keys)
    bias = _t5_bias_values(dist * dist_scale, tab_ref, h)
    o_ref[...] = jnp.where(valid, bias, NEG_INF).reshape(o_ref.shape)


def band_bias_table(rel_table, tq, span, pad, max_dist, dist_scale):
    kern = functools.partial(_bias_table_kernel, head_axis=0, blk_axis=1, base0=pad, base_step=0,
                             key_stride=1, max_dist=max_dist, n_valid_keys=span, dist_scale=dist_scale)
    return pl.pallas_call(
        kern,
        grid=(N_HEADS, 1),
        in_specs=[pl.BlockSpec(memory_space=pltpu.SMEM)],
        out_specs=pl.BlockSpec((span, tq), lambda h, j: (0, h)),
        out_shape=jax.ShapeDtypeStruct((span, N_HEADS * tq), F32),
        compiler_params=_params(("parallel", "arbitrary")),
        name="band_bias_table",
    )(rel_table)


def causal_bias_table(rel_table, tile, n_cls):
    kern = functools.partial(_bias_table_kernel, head_axis=1, blk_axis=0, base0=0, base_step=tile,
                             key_stride=1, max_dist=2 ** 30, n_valid_keys=tile, dist_scale=1)
    return pl.pallas_call(
        kern,
        grid=(n_cls, N_HEADS),
        in_specs=[pl.BlockSpec(memory_space=pltpu.SMEM)],
        out_specs=pl.BlockSpec((1, tile, tile), lambda c, h: (c, 0, h)),
        out_shape=jax.ShapeDtypeStruct((n_cls, tile, N_HEADS * tile), F32),
        compiler_params=_params(("parallel", "arbitrary")),
        name="causal_bias_table",
    )(rel_table)


def cmp_bias_table(rel_table, seqlen, n_cmp):
    tq = LANES
    kern = functools.partial(_bias_table_kernel, head_axis=1, blk_axis=0,
                             base0=-(NSA_CMP_LEN - 1), base_step=tq, key_stride=NSA_CMP_STRIDE,
                             max_dist=2 ** 30, n_valid_keys=n_cmp, dist_scale=1)
    return pl.pallas_call(
        kern,
        grid=(seqlen // tq, N_HEADS),
        in_specs=[pl.BlockSpec(memory_space=pltpu.SMEM)],
        out_specs=pl.BlockSpec((1, LANES, tq), lambda i, h: (i, 0, h)),
        out_shape=jax.ShapeDtypeStruct((seqlen // tq, LANES, N_HEADS * tq), F32),
        compiler_params=_params(("parallel", "arbitrary")),
        name="cmp_bias_table",
    )(rel_table)


def _banded_kernel(*refs, n_rep, n_grp, tq, n_prev, seq, has_sink, want_lse):
    q_ref, k_ref, v_ref, bias_ref = refs[:4]
    pos = 4
    sink_ref = None
    if has_sink:
        sink_ref = refs[pos]
        pos += 1
    o_ref = refs[pos]
    pos += 1
    lse_ref = None
    if want_lse:
        lse_ref = refs[pos]
        pos += 1
    kpad_ref, vpad_ref = refs[pos:pos + 2]

    i = pl.program_id(3)
    pad = n_prev * tq
    span = pad + tq
    rows = n_rep * tq
    qw = n_rep * HEAD_DIM

    @pl.when(i == 0)
    def _():
        if pad:
            kpad_ref[0:pad, :] = jnp.zeros((pad, n_grp * HEAD_DIM), BF16)
            vpad_ref[0:pad, :] = jnp.zeros((pad, n_grp * HEAD_DIM), BF16)
        kpad_ref[pad:pad + seq, :] = k_ref[0]
        vpad_ref[pad:pad + seq, :] = v_ref[0]

    start = pl.multiple_of(i * tq, tq)
    for gg in range(n_grp):
        kv_cols = slice(gg * HEAD_DIM, (gg + 1) * HEAD_DIM)
        row_cols = slice(gg * rows, (gg + 1) * rows)
        ks = kpad_ref[pl.ds(start, span), kv_cols]
        vs = vpad_ref[pl.ds(start, span), kv_cols]
        qs = _stack_heads(q_ref[0, :, gg * qw:(gg + 1) * qw], n_rep)
        lt = _dot_nt(ks, qs) * ATTN_SCALE + bias_ref[:, row_cols]
        if pad:
            key = lax.broadcasted_iota(jnp.int32, (span, rows), 0)
            lt = jnp.where(key >= pad - i * tq, lt, NEG_INF)
        m = jnp.max(lt, axis=0, keepdims=True)
        sink = None
        if has_sink:
            sink = sink_ref[:, row_cols]
            m = jnp.maximum(m, sink)
        p = jnp.exp(lt - m)
        s = jnp.sum(p, axis=0, keepdims=True)
        if has_sink:
            s = s + jnp.exp(sink - m)
        o_t = _dot_tn(vs, p.astype(BF16))
        s = jnp.maximum(s, TINY)
        _store_heads(o_ref, o_t / s, n_rep, tq, gg * qw)
        if want_lse:
            lse = m + jnp.log(s)
            head = lax.broadcasted_iota(jnp.int32, (LANES, tq), 0)
            tile = jnp.zeros((LANES, tq), F32)
            for r in range(n_rep):
                tile = jnp.where(head == r, lse[:, r * tq:(r + 1) * tq], tile)
            lse_ref[gg, 0] = tile.T


def banded_attention(q_arr, q_off, kv_arr, k_off, v_off, bias, *, batch, seqlen, n_kv, dil,
                     max_dist, sink_row=None, want_lse=False, out_dtype=F32):
    n_rep = N_HEADS // n_kv
    sub = seqlen // dil
    tq = math.gcd(sub, BAND_BLOCK)
    n_blk = sub // tq
    n_prev = min(-(-max_dist // tq), n_blk - 1)
    span = (n_prev + 1) * tq
    qw = n_rep * HEAD_DIM
    rows = n_rep * tq
    n_grp = 2 if rows <= 512 and n_kv % 2 == 0 else 1
    gqw, gkw = n_grp * qw, n_grp * HEAD_DIM
    assert q_arr.shape[:3] == kv_arr.shape[:3] == (batch, dil, sub)
    assert q_off % gqw == 0 and k_off % gkw == 0 and v_off % gkw == 0 and bias.shape == (span, N_HEADS * tq)

    in_specs = [
        pl.BlockSpec((None, 1, tq, gqw), lambda b, rho, g, i: (b, rho, i, q_off // gqw + g)),
        pl.BlockSpec((None, 1, sub, gkw), lambda b, rho, g, i: (b, rho, 0, k_off // gkw + g)),
        pl.BlockSpec((None, 1, sub, gkw), lambda b, rho, g, i: (b, rho, 0, v_off // gkw + g)),
        pl.BlockSpec((span, n_grp * rows), lambda b, rho, g, i: (0, g)),
    ]
    args = [q_arr, kv_arr, kv_arr, bias]
    if sink_row is not None:
        in_specs.append(pl.BlockSpec((1, n_grp * rows), lambda b, rho, g, i: (0, g)))
        args.append(sink_row)
    out_specs = [pl.BlockSpec((None, 1, tq, gqw), lambda b, rho, g, i: (b, rho, i, g))]
    out_shape = [jax.ShapeDtypeStruct((batch, dil, sub, ATTN_WIDTH), out_dtype)]
    if want_lse:
        out_specs.append(pl.BlockSpec((None, n_grp, 1, tq, LANES), lambda b, rho, g, i: (b, g, rho, i, 0)))
        out_shape.append(jax.ShapeDtypeStruct((batch, n_kv, dil, sub, LANES), F32))
    kern = functools.partial(_banded_kernel, n_rep=n_rep, n_grp=n_grp, tq=tq, n_prev=n_prev, seq=sub,
                             has_sink=sink_row is not None, want_lse=want_lse)
    outs = pl.pallas_call(
        kern,
        grid=(batch, dil, n_kv // n_grp, n_blk),
        in_specs=in_specs,
        out_specs=out_specs,
        out_shape=out_shape,
        scratch_shapes=[pltpu.VMEM((n_prev * tq + sub, gkw), BF16),
                        pltpu.VMEM((n_prev * tq + sub, gkw), BF16)],
        compiler_params=_params(("parallel", "parallel", "parallel", "arbitrary")),
        name="banded_attention",
    )(*args)
    return tuple(outs) if want_lse else outs[0]


def _flash_init(m_ref, s_ref, acc_ref):
    m_ref[...] = jnp.full(m_ref.shape, NEG_INF, F32)
    s_ref[...] = jnp.zeros(s_ref.shape, F32)
    acc_ref[...] = jnp.zeros(acc_ref.shape, F32)


def _flash_step(lt, v_blk, m_ref, s_ref, acc_ref):
    m_prev = m_ref[...]
    m_new = jnp.maximum(m_prev, jnp.max(lt, axis=0, keepdims=True))
    alpha = jnp.exp(m_prev - m_new)
    p = jnp.exp(lt - m_new)
    s_ref[...] = alpha * s_ref[...] + jnp.sum(p, axis=0, keepdims=True)
    acc_ref[...] = alpha * acc_ref[...] + _dot_tn(v_blk, p.astype(BF16))
    m_ref[...] = m_new


def _flash_finish(o_ref, s_ref, acc_ref, n_rep, tq):
    _store_heads(o_ref, acc_ref[...] / jnp.maximum(s_ref[...], TINY), n_rep, tq)


def _first_rank(score, n_cand):
    idx = lax.broadcasted_iota(jnp.int32, score.shape, 0)
    rank = jnp.zeros(score.shape, F32)
    for jp in range(n_cand):
        other = score[jp:jp + 1, :]
        ahead = jnp.where(other > score, 1.0, jnp.where(other == score, jnp.where(idx > jp, 1.0, 0.0), 0.0))
        rank = rank + ahead
    return rank


def _nsa_cmp_kernel(x_ref, pos_ref, w1k_ref, w2k_ref, w1v_ref, w2v_ref, ko_ref, vo_ref):
    width = 2 * NSA_KV_HEADS * HEAD_DIM
    for kv, (w1_ref, w2_ref, o_ref) in enumerate(((w1k_ref, w2k_ref, ko_ref), (w1v_ref, w2v_ref, vo_ref))):
        for g in range(NSA_KV_HEADS):
            off = kv * NSA_KV_HEADS * HEAD_DIM + g * HEAD_DIM
            chunk = jnp.concatenate(
                [x_ref[0, :, l * width + off:l * width + off + HEAD_DIM] for l in range(NSA_CMP_STRIDE)], axis=1)
            first = jnp.dot((chunk + pos_ref[0:1, :]).astype(BF16), w1_ref[0], preferred_element_type=F32)
            second = jnp.dot((chunk + pos_ref[1:2, :]).astype(BF16), w1_ref[1], preferred_element_type=F32)
            hidden = jax.nn.gelu(first + pltpu.roll(second, second.shape[0] - 1, axis=0))
            o_ref[0, g] = jnp.dot(hidden.astype(BF16), w2_ref[...], preferred_element_type=F32).astype(o_ref.dtype)


def nsa_compress(kcvc, cmp_pos, k_w1, k_w2, v_w1, v_w2, batch, seqlen):
    n_chunk = seqlen // NSA_CMP_STRIDE
    width = 2 * NSA_KV_HEADS * HEAD_DIM
    half = NSA_CMP_STRIDE * HEAD_DIM
    x = kcvc.reshape(batch, n_chunk, NSA_CMP_STRIDE * width)
    out = jax.ShapeDtypeStruct((batch, NSA_KV_HEADS, n_chunk, HEAD_DIM), BF16)
    full = lambda shape: pl.BlockSpec(shape, lambda b: (0,) * len(shape))
    return pl.pallas_call(
        _nsa_cmp_kernel,
        grid=(batch,),
        in_specs=[pl.BlockSpec((1, n_chunk, NSA_CMP_STRIDE * width), lambda b: (b, 0, 0)),
                  full((2, half)), full((2, half, HEAD_DIM)), full((HEAD_DIM, HEAD_DIM)),
                  full((2, half, HEAD_DIM)), full((HEAD_DIM, HEAD_DIM))],
        out_specs=[pl.BlockSpec((1, NSA_KV_HEADS, n_chunk, HEAD_DIM), lambda b: (b, 0, 0, 0))] * 2,
        out_shape=[out, out],
        compiler_params=_params(("parallel",)),
        name="nsa_compress",
    )(x, cmp_pos.reshape(2, half), k_w1.reshape(2, half, HEAD_DIM).astype(BF16), k_w2.astype(BF16),
      v_w1.reshape(2, half, HEAD_DIM).astype(BF16), v_w2.astype(BF16))


def _nsa_cmp_attn_kernel(q_ref, kc_ref, vc_ref, bias_ref, c2s_ref, o_ref, sel_ref, *, n_rep, tq, n_sel_blk):
    i = pl.program_id(2)
    qs = _stack_heads(q_ref[0], n_rep)
    bias = bias_ref[0]
    valid = bias > MASKED_BELOW
    lt = jnp.where(valid, _dot_nt(kc_ref[0, 0], qs) * ATTN_SCALE + bias, NEG_INF)
    m = jnp.max(lt, axis=0, keepdims=True)
    p = jnp.where(valid, jnp.exp(lt - m), 0.0)
    s = jnp.sum(p, axis=0, keepdims=True)
    p_cmp = p / jnp.maximum(s, TINY)
    _store_heads(o_ref, _dot_tn(vc_ref[0, 0], p_cmp.astype(BF16)), n_rep, tq)

    p_sum = p_cmp[:, 0:tq]
    for r in range(1, n_rep):
        p_sum = p_sum + p_cmp[:, r * tq:(r + 1) * tq]
    imp = jnp.dot(c2s_ref[...], p_sum.astype(BF16), preferred_element_type=F32)[0:n_sel_blk]
    blk = lax.broadcasted_iota(jnp.int32, (n_sel_blk, tq), 0)
    tpos = i * tq + lax.broadcasted_iota(jnp.int32, (n_sel_blk, tq), 1)
    cur = tpos // NSA_SEL_LEN
    forced = (blk == 0) | (blk == cur) | (blk == cur - 1)
    score = jnp.where(forced, FORCED_SCORE, jnp.where(blk * NSA_SEL_LEN <= tpos, imp, NEG_INF))
    sel_ref[0, 0] = jnp.where(_first_rank(score, n_sel_blk) < min(NSA_SEL_TOPN, n_sel_blk), 1.0, 0.0)


def nsa_cmp_attention(q, kcmp, vcmp, bias, batch, seqlen):
    n_rep = N_HEADS // NSA_KV_HEADS
    tq = LANES
    qw = n_rep * HEAD_DIM
    n_sel_blk = seqlen // NSA_SEL_LEN
    n_cmp = (seqlen - NSA_CMP_LEN) // NSA_CMP_STRIDE + 1
    a, b = NSA_SEL_LEN // NSA_CMP_STRIDE, NSA_CMP_LEN // NSA_CMP_STRIDE
    w = np.zeros((LANES, LANES), np.float32)
    j = np.arange(n_sel_blk)
    for mm in range(a):
        for nn in range(b):
            ii = a * j + mm + nn - (b - 1)
            ok = (ii >= 0) & (ii < n_cmp)
            np.add.at(w, (j[ok], ii[ok]), 1.0)
    kern = functools.partial(_nsa_cmp_attn_kernel, n_rep=n_rep, tq=tq, n_sel_blk=n_sel_blk)
    return pl.pallas_call(
        kern,
        grid=(batch, NSA_KV_HEADS, seqlen // tq),
        in_specs=[pl.BlockSpec((1, tq, qw), lambda b_, g, i: (b_, i, g)),
                  pl.BlockSpec((1, 1, LANES, HEAD_DIM), lambda b_, g, i: (b_, g, 0, 0)),
                  pl.BlockSpec((1, 1, LANES, HEAD_DIM), lambda b_, g, i: (b_, g, 0, 0)),
                  pl.BlockSpec((1, LANES, n_rep * tq), lambda b_, g, i: (i, 0, g)),
                  pl.BlockSpec((LANES, LANES), lambda b_, g, i: (0, 0))],
        out_specs=[pl.BlockSpec((1, tq, qw), lambda b_, g, i: (b_, i, g)),
                   pl.BlockSpec((1, 1, n_sel_blk, tq), lambda b_, g, i: (b_, g, 0, i))],
        out_shape=[jax.ShapeDtypeStruct((batch, seqlen, ATTN_WIDTH), F32),
                   jax.ShapeDtypeStruct((batch, NSA_KV_HEADS, n_sel_blk, seqlen), F32)],
        compiler_params=_params(("parallel", "parallel", "arbitrary")),
        name="nsa_cmp_attention",
    )(q.reshape(batch, seqlen, ATTN_WIDTH), kcmp, vcmp, bias, jnp.asarray(w, BF16))


def _visit_past_chunks(i, visit):
    def pair(pp, carry):
        visit(i - 2 - 2 * pp, 2)
        return carry

    lax.fori_loop(0, i // 2, pair, 0)

    @pl.when((i & 1) == 1)
    def _():
        visit(0, 1)


def _chunk_bias(bias_ref, i, c_lo, n):
    if n == 1:
        return bias_ref[i - c_lo]
    return jnp.concatenate([bias_ref[i - c_lo - t] for t in range(n)], axis=0)


def _nsa_sel_kernel(q_ref, k_ref, v_ref, sel_ref, bias_ref, o_ref, m_ref, s_ref, acc_ref, *, n_rep, tq):
    i = pl.program_id(2)
    qs = _stack_heads(q_ref[0], n_rep)
    per = tq // NSA_SEL_LEN
    _flash_init(m_ref, s_ref, acc_ref)

    def visit(c_lo, n):
        start = pl.multiple_of(c_lo * tq, tq)
        lt = _dot_nt(k_ref[0, pl.ds(start, n * tq), :], qs) * ATTN_SCALE + _chunk_bias(bias_ref, i, c_lo, n)
        slabs = []
        for b in range(n * per):
            on = sel_ref[0, 0, pl.ds(c_lo * per + b, 1), :]
            on = jnp.concatenate([on] * n_rep, axis=1)
            slabs.append(jnp.where(on > 0.5, lt[b * NSA_SEL_LEN:(b + 1) * NSA_SEL_LEN], NEG_INF))
        _flash_step(jnp.concatenate(slabs, axis=0), v_ref[0, pl.ds(start, n * tq), :], m_ref, s_ref, acc_ref)

    visit(i, 1)
    _visit_past_chunks(i, visit)
    _flash_finish(o_ref, s_ref, acc_ref, n_rep, tq)


def nsa_selected_attention(q, kv, k_off, v_off, sel, causal_bias, batch, seqlen):
    n_rep = N_HEADS // NSA_KV_HEADS
    tq = SEL_TILE
    qw = n_rep * HEAD_DIM
    n_blk = seqlen // tq
    n_sel_blk = seqlen // NSA_SEL_LEN
    ckv = kv.shape[1]
    kv3 = kv.reshape(batch, seqlen, ckv)
    kern = functools.partial(_nsa_sel_kernel, n_rep=n_rep, tq=tq)
    rows = n_rep * tq
    return pl.pallas_call(
        kern,
        grid=(batch, NSA_KV_HEADS, n_blk),
        in_specs=[pl.BlockSpec((1, tq, qw), lambda b, g, i: (b, i, g)),
                  pl.BlockSpec((1, seqlen, HEAD_DIM), lambda b, g, i: (b, 0, k_off // HEAD_DIM + g)),
                  pl.BlockSpec((1, seqlen, HEAD_DIM), lambda b, g, i: (b, 0, v_off // HEAD_DIM + g)),
                  pl.BlockSpec((1, 1, n_sel_blk, tq), lambda b, g, i: (b, g, 0, i)),
                  pl.BlockSpec((n_blk, tq, rows), lambda b, g, i: (0, 0, g), pipeline_mode=pl.Buffered(1))],
        out_specs=pl.BlockSpec((1, tq, qw), lambda b, g, i: (b, i, g)),
        out_shape=jax.ShapeDtypeStruct((batch, seqlen, ATTN_WIDTH), F32),
        scratch_shapes=[pltpu.VMEM((1, rows), F32), pltpu.VMEM((1, rows), F32),
                        pltpu.VMEM((HEAD_DIM, rows), F32)],
        compiler_params=_params(("parallel", "parallel", "arbitrary")),
        name="nsa_selected_attention",
    )(q.reshape(batch, seqlen, ATTN_WIDTH), kv3, kv3, sel, causal_bias)


def _nsa_gate_kernel(g_ref, oc_ref, os_ref, ow_ref, o_ref):
    gate = jax.nn.sigmoid(g_ref[...])
    for h in range(N_HEADS):
        cols = slice(h * HEAD_DIM, (h + 1) * HEAD_DIM)
        mix = (gate[:, 3 * h:3 * h + 1] * oc_ref[:, cols] + gate[:, 3 * h + 1:3 * h + 2] * os_ref[:, cols]
               + gate[:, 3 * h + 2:3 * h + 3] * ow_ref[:, cols])
        o_ref[:, cols] = mix.astype(o_ref.dtype)


def nsa_gate_combine(gates, o_cmp, o_slc, o_win):
    m = gates.shape[0]
    tm = 128
    wide = pl.BlockSpec((tm, ATTN_WIDTH), lambda i: (i, 0))
    return pl.pallas_call(
        _nsa_gate_kernel,
        grid=(m // tm,),
        in_specs=[pl.BlockSpec((tm, gates.shape[1]), lambda i: (i, 0)), wide, wide, wide],
        out_specs=wide,
        out_shape=jax.ShapeDtypeStruct((m, ATTN_WIDTH), BF16),
        compiler_params=_params(("parallel",)),
        name="nsa_gate_combine",
    )(gates, o_cmp, o_slc, o_win)


def _moba_kernel(q_ref, k_ref, v_ref, bias_ref, o_ref, kb_ref, vb_ref, km_ref, m_ref, s_ref, acc_ref,
                 *, n_rep, tq, n_blk):
    i = pl.program_id(2)
    rows = n_rep * tq

    @pl.when(i == 0)
    def _():
        k = k_ref[0]
        kb_ref[...] = k.astype(BF16)
        vb_ref[...] = v_ref[0].astype(BF16)
        slot = lax.broadcasted_iota(jnp.int32, (BF16_ROWS, HEAD_DIM), 0)
        means = jnp.zeros((BF16_ROWS, HEAD_DIM), F32)
        for j in range(n_blk):
            means = jnp.where(slot == j, jnp.mean(k[j * tq:(j + 1) * tq], axis=0, keepdims=True), means)
        km_ref[...] = means.astype(BF16)

    qs = _stack_heads(q_ref[0], n_rep)
    gate = _dot_nt(km_ref[...], qs)[0:n_blk]
    blk = lax.broadcasted_iota(jnp.int32, (n_blk, rows), 0)
    past = blk < i
    rank = _first_rank(jnp.where(past, gate, NEG_INF), n_blk)
    chosen = jnp.where(past, jnp.where(rank < min(MOBA_TOPK, max(n_blk - 1, 1)), 1.0, 0.0), 0.0)

    _flash_init(m_ref, s_ref, acc_ref)
    own = pl.multiple_of(i * tq, tq)
    lt = _dot_nt(kb_ref[pl.ds(own, tq), :], qs) * ATTN_SCALE + bias_ref[0]
    _flash_step(lt, vb_ref[pl.ds(own, tq), :], m_ref, s_ref, acc_ref)

    def visit(c_lo, n):
        start = pl.multiple_of(c_lo * tq, tq)
        lt = _dot_nt(kb_ref[pl.ds(start, n * tq), :], qs) * ATTN_SCALE + _chunk_bias(bias_ref, i, c_lo, n)
        parts = []
        for t in range(n):
            row_on = jnp.sum(jnp.where(blk == c_lo + t, chosen, 0.0), axis=0, keepdims=True)
            parts.append(jnp.where(row_on > 0.5, lt[t * tq:(t + 1) * tq], NEG_INF))
        _flash_step(jnp.concatenate(parts, axis=0), vb_ref[pl.ds(start, n * tq), :], m_ref, s_ref, acc_ref)

    _visit_past_chunks(i, visit)
    _flash_finish(o_ref, s_ref, acc_ref, n_rep, tq)


def moba_attention(q, kv, causal_bias, batch, seqlen):
    n_rep = N_HEADS // MOBA_KV_HEADS
    tq = MOBA_BLOCK
    qw = n_rep * HEAD_DIM
    n_blk = seqlen // tq
    assert n_blk <= SUBLANES
    rows = n_rep * tq
    kv3 = kv.reshape(batch, seqlen, kv.shape[1])
    kern = functools.partial(_moba_kernel, n_rep=n_rep, tq=tq, n_blk=n_blk)
    return pl.pallas_call(
        kern,
        grid=(batch, MOBA_KV_HEADS, n_blk),
        in_specs=[pl.BlockSpec((1, tq, qw), lambda b, g, i: (b, i, g)),
                  pl.BlockSpec((1, seqlen, HEAD_DIM), lambda b, g, i: (b, 0, g)),
                  pl.BlockSpec((1, seqlen, HEAD_DIM), lambda b, g, i: (b, 0, MOBA_KV_HEADS + g)),
                  pl.BlockSpec((n_blk, tq, rows), lambda b, g, i: (0, 0, g), pipeline_mode=pl.Buffered(1))],
        out_specs=pl.BlockSpec((1, tq, qw), lambda b, g, i: (b, i, g)),
        out_shape=jax.ShapeDtypeStruct((batch, seqlen, ATTN_WIDTH), BF16),
        scratch_shapes=[pltpu.VMEM((seqlen, HEAD_DIM), BF16), pltpu.VMEM((seqlen, HEAD_DIM), BF16),
                        pltpu.VMEM((BF16_ROWS, HEAD_DIM), BF16),
                        pltpu.VMEM((1, rows), F32), pltpu.VMEM((1, rows), F32),
                        pltpu.VMEM((HEAD_DIM, rows), F32)],
        compiler_params=_params(("parallel", "parallel", "arbitrary")),
        name="moba_attention",
    )(q.reshape(batch, seqlen, ATTN_WIDTH), kv3, kv3, causal_bias)


def _dil_combine_kernel(*refs, n_rep, dils, tile):
    n_grp = len(dils)
    o_refs, l_refs, o_ref = refs[:n_grp], refs[n_grp:2 * n_grp], refs[2 * n_grp]
    scratch = list(refs[2 * n_grp + 1:])
    outs, lses = [], []
    for o_g, l_g, dil in zip(o_refs, l_refs, dils):
        if dil == 1:
            outs.append([o_g[0, :, r * HEAD_DIM:(r + 1) * HEAD_DIM] for r in range(n_rep)])
            lses.append(l_g[0])
            continue
        nat_o, nat_l = scratch.pop(0), scratch.pop(0)
        per = tile // dil
        for rho in range(dil):
            for r in range(n_rep):
                nat_o[r, pl.ds(rho, per, stride=dil), :] = o_g[rho, :, r * HEAD_DIM:(r + 1) * HEAD_DIM]
            nat_l[pl.ds(rho, per, stride=dil), :] = l_g[rho]
        outs.append([nat_o[r] for r in range(n_rep)])
        lses.append(nat_l[...])
    top = functools.reduce(jnp.maximum, lses)
    weights = [jnp.exp(l - top) for l in lses]
    den = functools.reduce(lambda x, y: x + y, weights)
    weights = [w / den for w in weights]
    for r in range(n_rep):
        mix = weights[0][:, r:r + 1] * outs[0][r]
        for w, o in zip(weights[1:], outs[1:]):
            mix = mix + w[:, r:r + 1] * o[r]
        o_ref[0, :, r * HEAD_DIM:(r + 1) * HEAD_DIM] = mix.astype(o_ref.dtype)


def dilated_combine(outs, lses, dils, batch, seqlen):
    n_rep = N_HEADS // DIL_KV_HEADS
    tile = SEL_TILE
    qw = n_rep * HEAD_DIM
    in_specs, scratch = [], []
    for dil in dils:
        in_specs.append(pl.BlockSpec((None, dil, tile // dil, qw), lambda b, g, i: (b, 0, i, g)))
    for dil in dils:
        in_specs.append(pl.BlockSpec((None, None, dil, tile // dil, LANES), lambda b, g, i: (b, g, 0, i, 0)))
        if dil > 1:
            scratch += [pltpu.VMEM((n_rep, tile, HEAD_DIM), F32), pltpu.VMEM((tile, LANES), F32)]
    return pl.pallas_call(
        functools.partial(_dil_combine_kernel, n_rep=n_rep, dils=tuple(dils), tile=tile),
        grid=(batch, DIL_KV_HEADS, seqlen // tile),
        in_specs=in_specs,
        out_specs=pl.BlockSpec((1, tile, qw), lambda b, g, i: (b, i, g)),
        out_shape=jax.ShapeDtypeStruct((batch, seqlen, ATTN_WIDTH), BF16),
        scratch_shapes=scratch,
        compiler_params=_params(("parallel", "parallel", "parallel")),
        name="dilated_combine",
    )(*outs, *lses)


def nsa_mixer(u, h, w_in, j, cmp_pos, k_w1, k_w2, v_w1, v_w2, w_out, tables, batch, seqlen):
    kvw = NSA_KV_HEADS * HEAD_DIM
    c0 = ATTN_WIDTH
    tokens = batch * seqlen
    q = matmul(u, w_in, j, 0, c0, BF16)
    kcvc = matmul(u, w_in, j, c0, 2 * kvw, F32)
    kvsw = matmul(u, w_in, j, c0 + 2 * kvw, 4 * kvw, BF16)
    gates = matmul(u, w_in[:, :, c0 + 6 * kvw:], j, 0, 3 * N_HEADS, F32)
    kcmp, vcmp = nsa_compress(kcvc, cmp_pos, k_w1, k_w2, v_w1, v_w2, batch, seqlen)
    o_cmp, sel = nsa_cmp_attention(q, kcmp, vcmp, tables["cmp"], batch, seqlen)
    o_slc = nsa_selected_attention(q, kvsw, 0, kvw, sel, tables["causal"], batch, seqlen)
    o_win = banded_attention(q.reshape(batch, 1, seqlen, c0), 0, kvsw.reshape(batch, 1, seqlen, 4 * kvw),
                             2 * kvw, 3 * kvw, tables["nsa_win"], batch=batch, seqlen=seqlen,
                             n_kv=NSA_KV_HEADS, dil=1, max_dist=NSA_WINDOW - 1)
    o = nsa_gate_combine(gates, o_cmp.reshape(tokens, ATTN_WIDTH), o_slc.reshape(tokens, ATTN_WIDTH),
                         o_win.reshape(tokens, ATTN_WIDTH))
    return matmul_residual(o, w_out, j, h, 1.0)


def dilated_mixer(u, h, w_in, j, w_out, tables, batch, seqlen):
    kvw = DIL_KV_HEADS * HEAD_DIM
    group = ATTN_WIDTH + 2 * kvw
    outs, lses, dils = [], [], []
    for gi, (window, dil) in enumerate(DIL_PAIRS):
        proj = matmul(u, w_in, j, gi * group, group, BF16, dil=dil, batch=batch)
        proj = proj.reshape(batch, dil, seqlen // dil, group)
        o, lse = banded_attention(proj, 0, proj, ATTN_WIDTH, ATTN_WIDTH + kvw, tables["dil%d" % dil],
                                  batch=batch, seqlen=seqlen, n_kv=DIL_KV_HEADS, dil=dil,
                                  max_dist=window // dil, want_lse=True)
        outs.append(o)
        lses.append(lse)
        dils.append(dil)
    o = dilated_combine(outs, lses, dils, batch, seqlen)
    return matmul_residual(o.reshape(batch * seqlen, ATTN_WIDTH), w_out, j, h, 1.0)


def moba_mixer(u, h, w_in, j, w_out, tables, batch, seqlen):
    q = matmul(u, w_in, j, 0, ATTN_WIDTH, BF16)
    kv = matmul(u, w_in, j, ATTN_WIDTH, 2 * MOBA_KV_HEADS * HEAD_DIM, F32)
    o = moba_attention(q, kv, tables["causal"], batch, seqlen)
    return matmul_residual(o.reshape(batch * seqlen, ATTN_WIDTH), w_out, j, h, 1.0)


def swa_mixer(u, h, w_in, j, sinks, w_out, tables, batch, seqlen):
    kvw = SWA_KV_HEADS * HEAD_DIM
    width = ATTN_WIDTH + 2 * kvw
    proj = matmul(u, w_in, j, 0, width, BF16).reshape(batch, 1, seqlen, width)
    sink_row = jnp.repeat(sinks, math.gcd(seqlen, BAND_BLOCK))[None, :]
    o = banded_attention(proj, 0, proj, ATTN_WIDTH, ATTN_WIDTH + kvw, tables["swa"], batch=batch,
                         seqlen=seqlen, n_kv=SWA_KV_HEADS, dil=1, max_dist=SWA_WINDOW - 1, sink_row=sink_row,
                         out_dtype=BF16)
    return matmul_residual(o.reshape(batch * seqlen, ATTN_WIDTH), w_out, j, h, 1.0)


def _band_table_for(rel_table, seqlen, dil, max_dist):
    sub = seqlen // dil
    tq = math.gcd(sub, BAND_BLOCK)
    n_prev = min(-(-max_dist // tq), sub // tq - 1)
    return band_bias_table(rel_table, tq, (n_prev + 1) * tq, n_prev * tq, max_dist, dil)


def kernel(x, rel_table, ffn1_norm, ffn1_w_gate, ffn1_w_up, ffn1_w_down, mix_norm, ffn2_norm, ffn2_w_gate, ffn2_w_up, ffn2_w_down, final_norm, nsa_w_in, nsa_cmp_pos, nsa_cmp_k_w1, nsa_cmp_k_w2, nsa_cmp_v_w1, nsa_cmp_v_w2, nsa_w_out, dil_w_in, dil_w_out, moba_w_in, moba_w_out, swa_w_in, swa_sinks, swa_w_out):
    batch, seqlen, d_model = x.shape
    depth = ffn1_norm.shape[0]
    n_mixers = 4
    h = x.reshape(batch * seqlen, d_model)

    tables = {
        "causal": causal_bias_table(rel_table, SEL_TILE, seqlen // SEL_TILE),
        "cmp": cmp_bias_table(rel_table, seqlen, (seqlen - NSA_CMP_LEN) // NSA_CMP_STRIDE + 1),
        "nsa_win": _band_table_for(rel_table, seqlen, 1, NSA_WINDOW - 1),
        "swa": _band_table_for(rel_table, seqlen, 1, SWA_WINDOW - 1),
    }
    for window, dil in DIL_PAIRS:
        tables["dil%d" % dil] = _band_table_for(rel_table, seqlen, dil, window // dil)

    for i in range(depth):
        h = ffn_half_step(h, ffn1_norm[i], ffn1_w_gate, ffn1_w_up, ffn1_w_down, i)
        u = rms_norm(h, mix_norm[i], BF16)
        m, j = i % n_mixers, i // n_mixers
        if m == 0:
            h = nsa_mixer(u, h, nsa_w_in, j, nsa_cmp_pos[j], nsa_cmp_k_w1[j], nsa_cmp_k_w2[j],
                          nsa_cmp_v_w1[j], nsa_cmp_v_w2[j], nsa_w_out, tables, batch, seqlen)
        elif m == 1:
            h = dilated_mixer(u, h, dil_w_in, j, dil_w_out, tables, batch, seqlen)
        elif m == 2:
            h = moba_mixer(u, h, moba_w_in, j, moba_w_out, tables, batch, seqlen)
        else:
            h = swa_mixer(u, h, swa_w_in, j, swa_sinks[j], swa_w_out, tables, batch, seqlen)
        h = ffn_half_step(h, ffn2_norm[i], ffn2_w_gate, ffn2_w_up, ffn2_w_down, i)
    return rms_norm(h, final_norm, x.dtype).reshape(batch, seqlen, d_model)
```

```python
import functools
import math

import numpy as np
import jax
import jax.numpy as jnp
from jax import lax
from jax.experimental import pallas as pl
from jax.experimental.pallas import tpu as pltpu

HEAD_DIM = 128
N_HEADS = 32
ATTN_WIDTH = N_HEADS * HEAD_DIM
RMS_EPS = 1e-6
REL_BUCKETS = 32
REL_MAX_DIST = 2048
BAND_BLOCK = 128
NSA_KV_HEADS = 4
NSA_CMP_LEN = 32
NSA_CMP_STRIDE = 16
NSA_SEL_LEN = 64
NSA_SEL_TOPN = 16
NSA_WINDOW = 512
DIL_PAIRS = ((128, 1), (512, 4), (2048, 16))
DIL_KV_HEADS = 8
MOBA_BLOCK = 256
MOBA_TOPK = 3
MOBA_KV_HEADS = 8
SWA_WINDOW = 128
SWA_KV_HEADS = 4
ATTN_SCALE = HEAD_DIM ** -0.5
LOG2E = 1.0 / math.log(2.0)
LN2 = math.log(2.0)
SCORE_SCALE = ATTN_SCALE * LOG2E
NEG_INF = -1e30
MASKED_BELOW = -5e29
TINY = 1e-20
FORCED_SCORE = 1e9

LANES = 128
SUBLANES = 8
BF16_ROWS = 16
SEL_TILE = 256
VMEM_LIMIT = 56 * 1024 * 1024

F32 = jnp.float32
BF16 = jnp.bfloat16


def _params(semantics):
    return pltpu.CompilerParams(dimension_semantics=semantics, vmem_limit_bytes=VMEM_LIMIT)


def _dot_nt(a, b):
    return lax.dot_general(a, b, (((1,), (1,)), ((), ())), preferred_element_type=F32)


def _dot_tn(a, b):
    return lax.dot_general(a, b, (((0,), (0,)), ((), ())), preferred_element_type=F32)


def _stack_heads(q, n_heads):
    return jnp.concatenate([q[:, r * HEAD_DIM:(r + 1) * HEAD_DIM] for r in range(n_heads)], axis=0)


def _store_heads(o_ref, o_t, n_heads, tq, col0=0):
    for r in range(n_heads):
        cols = slice(col0 + r * HEAD_DIM, col0 + (r + 1) * HEAD_DIM)
        o_ref[0, :, cols] = o_t[:, r * tq:(r + 1) * tq].T.astype(o_ref.dtype)


def _rms_kernel(x_ref, g_ref, o_ref):
    x = x_ref[...]
    y = x * lax.rsqrt(jnp.mean(x * x, axis=-1, keepdims=True) + RMS_EPS)
    o_ref[...] = (y * g_ref[...]).astype(o_ref.dtype)


def rms_norm(x, gain, out_dtype):
    m, d = x.shape
    tm = 256
    return pl.pallas_call(
        _rms_kernel,
        grid=(m // tm,),
        in_specs=[pl.BlockSpec((tm, d), lambda i: (i, 0)), pl.BlockSpec((1, d), lambda i: (0, 0))],
        out_specs=pl.BlockSpec((tm, d), lambda i: (i, 0)),
        out_shape=jax.ShapeDtypeStruct((m, d), out_dtype),
        compiler_params=_params(("parallel",)),
        name="rms_norm",
    )(x, gain.reshape(1, d))


def _norm_prep_kernel(x_ref, g_ref, a_ref, ssq_ref):
    x = x_ref[...]
    a_ref[...] = (x * g_ref[...]).astype(a_ref.dtype)
    ssq_ref[...] = jnp.broadcast_to(jnp.sum(x * x, axis=-1, keepdims=True), ssq_ref.shape)


def norm_prep(x, gain):
    m, d = x.shape
    tm = 256
    return pl.pallas_call(
        _norm_prep_kernel,
        grid=(m // tm,),
        in_specs=[pl.BlockSpec((tm, d), lambda i: (i, 0)), pl.BlockSpec((1, d), lambda i: (0, 0))],
        out_specs=[pl.BlockSpec((tm, d), lambda i: (i, 0)), pl.BlockSpec((tm, LANES), lambda i: (i, 0))],
        out_shape=[jax.ShapeDtypeStruct((m, d), BF16), jax.ShapeDtypeStruct((m, LANES), F32)],
        compiler_params=_params(("parallel",)),
        name="norm_prep",
    )(x, gain.reshape(1, d))


def _row_scale(ssq_ref, d):
    return lax.rsqrt(ssq_ref[:, 0:1] * (1.0 / d) + RMS_EPS)


MM_VMEM_BUDGET = 50 * 1024 * 1024


def _mm_kernel(a_ref, ssq_ref, b_ref, o_ref, *scratch, dil):
    res = jnp.dot(a_ref[...], b_ref[...].astype(BF16), preferred_element_type=F32)
    res = res * _row_scale(ssq_ref, a_ref.shape[1])
    if dil == 1:
        o_ref[...] = res.astype(o_ref.dtype)
        return
    scr_ref, = scratch
    tm, tn = res.shape
    for s in range(tn // LANES):
        scr_ref[s] = res[:, s * LANES:(s + 1) * LANES]
    for rho in range(dil):
        for s in range(tn // LANES):
            o_ref[rho, :, s * LANES:(s + 1) * LANES] = (
                scr_ref[s, pl.ds(rho, tm // dil, stride=dil), :].astype(o_ref.dtype))


def _mm_res_kernel(a_ref, b_ref, r_ref, *rest, scale, emit_next):
    if emit_next:
        g_ref, o_ref, an_ref, ssq_ref = rest

        @pl.when(pl.program_id(1) == 0)
        def _():
            ssq_ref[...] = jnp.zeros(ssq_ref.shape, F32)
    else:
        o_ref, = rest
    h = r_ref[...] + scale * jnp.dot(a_ref[...], b_ref[...].astype(BF16), preferred_element_type=F32)
    o_ref[...] = h
    if emit_next:
        an_ref[...] = (h * g_ref[...]).astype(an_ref.dtype)
        ssq_ref[...] += jnp.sum(h * h, axis=-1, keepdims=True)


def _mm_swiglu_kernel(a_ref, ssq_ref, bg_ref, bu_ref, o_ref):
    a = a_ref[...]
    r = _row_scale(ssq_ref, a_ref.shape[1])
    gate = jnp.dot(a, bg_ref[...].astype(BF16), preferred_element_type=F32) * r
    up = jnp.dot(a, bu_ref[...].astype(BF16), preferred_element_type=F32) * r
    o_ref[...] = (jax.nn.silu(gate) * up).astype(o_ref.dtype)


def _mm_tiles(m, k, n, n_weights, io_bytes):
    tm = min(m, 1024)
    if n < LANES:
        return tm, n
    for tn in (512, 256, 128):
        need = (2 * tm * k * 2 + n_weights * (2 * k * tn * 4 + k * tn * 2) + 2 * tm * tn * io_bytes
                + n_weights * tm * tn * 4)
        if n % tn == 0 and need <= MM_VMEM_BUDGET:
            return tm, tn
    raise ValueError("no matmul tile fits VMEM")


def _weight_spec(w, layer, col_off, tn):
    assert col_off % tn == 0
    return pl.BlockSpec((None, w.shape[1], tn), lambda i, j: (layer, 0, col_off // tn + j))


def matmul(act, w, layer, col_off, n, out_dtype, *, dil=1, batch=1):
    a, ssq = act
    m, k = a.shape
    tm, tn = _mm_tiles(m, k, n, 1, jnp.dtype(out_dtype).itemsize)
    in_specs = [pl.BlockSpec((tm, k), lambda i, j: (i, 0)), pl.BlockSpec((tm, LANES), lambda i, j: (i, 0)),
                _weight_spec(w, layer, col_off, tn)]
    if dil == 1:
        out_specs = pl.BlockSpec((tm, tn), lambda i, j: (i, j))
        out_shape = jax.ShapeDtypeStruct((m, n), out_dtype)
        scratch = []
    else:
        per_batch = m // batch // tm
        assert m % (batch * tm) == 0 and tm % (dil * BF16_ROWS) == 0 and tn % LANES == 0
        out_specs = pl.BlockSpec((None, dil, tm // dil, tn), lambda i, j: (i // per_batch, 0, i % per_batch, j))
        out_shape = jax.ShapeDtypeStruct((batch, dil, m // batch // dil, n), out_dtype)
        scratch = [pltpu.VMEM((tn // LANES, tm, LANES), F32)]
    return pl.pallas_call(
        functools.partial(_mm_kernel, dil=dil),
        grid=(m // tm, n // tn),
        in_specs=in_specs,
        out_specs=out_specs,
        out_shape=out_shape,
        scratch_shapes=scratch,
        compiler_params=_params(("parallel", "arbitrary")),
        name="matmul",
    )(a, ssq, w)


def matmul_residual(a, w, layer, res, scale, next_gain=None):
    m, k = a.shape
    n = w.shape[2]
    emit_next = next_gain is not None
    tm, tn = _mm_tiles(m, k, n, 1, 10 if emit_next else 8)
    tile = pl.BlockSpec((tm, tn), lambda i, j: (i, j))
    in_specs = [pl.BlockSpec((tm, k), lambda i, j: (i, 0)), _weight_spec(w, layer, 0, tn), tile]
    args = [a, w, res]
    out_specs, out_shape = tile, jax.ShapeDtypeStruct((m, n), F32)
    if emit_next:
        in_specs.append(pl.BlockSpec((1, tn), lambda i, j: (0, j)))
        args.append(next_gain.reshape(1, n))
        out_specs = [tile, tile, pl.BlockSpec((tm, LANES), lambda i, j: (i, 0))]
        out_shape = [out_shape, jax.ShapeDtypeStruct((m, n), BF16), jax.ShapeDtypeStruct((m, LANES), F32)]
    outs = pl.pallas_call(
        functools.partial(_mm_res_kernel, scale=scale, emit_next=emit_next),
        grid=(m // tm, n // tn),
        in_specs=in_specs,
        out_specs=out_specs,
        out_shape=out_shape,
        compiler_params=_params(("parallel", "arbitrary")),
        name="matmul_residual",
    )(*args)
    return (outs[0], (outs[1], outs[2])) if emit_next else (outs, None)


def matmul_swiglu(act, wg, wu, layer):
    a, ssq = act
    m, k = a.shape
    n = wg.shape[2]
    tm, tn = _mm_tiles(m, k, n, 2, 2)
    return pl.pallas_call(
        _mm_swiglu_kernel,
        grid=(m // tm, n // tn),
        in_specs=[pl.BlockSpec((tm, k), lambda i, j: (i, 0)),
                  pl.BlockSpec((tm, LANES), lambda i, j: (i, 0)),
                  _weight_spec(wg, layer, 0, tn),
                  _weight_spec(wu, layer, 0, tn)],
        out_specs=pl.BlockSpec((tm, tn), lambda i, j: (i, j)),
        out_shape=jax.ShapeDtypeStruct((m, n), BF16),
        compiler_params=_params(("parallel", "arbitrary")),
        name="matmul_swiglu",
    )(a, ssq, wg, wu)


def ffn_half_step(h, act, w_gate, w_up, w_down, layer, next_gain):
    hidden = matmul_swiglu(act, w_gate, w_up, layer)
    return matmul_residual(hidden, w_down, layer, h, 0.5, next_gain)


def _t5_bucket(dist):
    n = jnp.maximum(dist, 0)
    exact = REL_BUCKETS // 2
    nf = jnp.maximum(n, 1).astype(F32)
    large = exact + (jnp.log(nf / exact) * ((REL_BUCKETS - exact) / math.log(REL_MAX_DIST / exact))).astype(jnp.int32)
    return jnp.where(n < exact, n, jnp.minimum(large, REL_BUCKETS - 1))


def _bias_table_kernel(tab_ref, o_ref, bucket_ref, *, base0, base_step, key_stride, max_dist, n_valid_keys,
                       dist_scale):
    blk = pl.program_id(0)
    h = pl.program_id(1)

    @pl.when(h == 0)
    def _():
        n_keys, n_qry = bucket_ref.shape
        key = lax.broadcasted_iota(jnp.int32, (n_keys, n_qry), 0)
        qry = lax.broadcasted_iota(jnp.int32, (n_keys, n_qry), 1)
        dist = base0 + blk * base_step + qry - key * key_stride
        valid = (dist >= 0) & (dist <= max_dist) & (key < n_valid_keys)
        bucket_ref[...] = jnp.where(valid, _t5_bucket(dist * dist_scale), -1)

    bucket = bucket_ref[...]
    level = [tab_ref[b, h] * LOG2E for b in range(REL_BUCKETS)]
    bit = 1
    while len(level) > 1:
        odd = (bucket & bit) != 0
        level = [jnp.where(odd, level[2 * t + 1], level[2 * t]) for t in range(len(level) // 2)]
        bit *= 2
    o_ref[...] = jnp.where(bucket >= 0, level[0], NEG_INF).reshape(o_ref.shape)


def band_bias_table(rel_table, tq, span, pad, max_dist, dist_scale):
    kern = functools.partial(_bias_table_kernel, base0=pad, base_step=0, key_stride=1, max_dist=max_dist,
                             n_valid_keys=span, dist_scale=dist_scale)
    return pl.pallas_call(
        kern,
        grid=(1, N_HEADS),
        in_specs=[pl.BlockSpec(memory_space=pltpu.SMEM)],
        out_specs=pl.BlockSpec((span, tq), lambda j, h: (0, h)),
        out_shape=jax.ShapeDtypeStruct((span, N_HEADS * tq), F32),
        scratch_shapes=[pltpu.VMEM((span, tq), jnp.int32)],
        compiler_params=_params(("parallel", "arbitrary")),
        name="band_bias_table",
    )(rel_table)


def causal_bias_table(rel_table, tile, n_cls):
    kern = functools.partial(_bias_table_kernel, base0=0, base_step=tile, key_stride=1, max_dist=2 ** 30,
                             n_valid_keys=tile, dist_scale=1)
    return pl.pallas_call(
        kern,
        grid=(n_cls, N_HEADS),
        in_specs=[pl.BlockSpec(memory_space=pltpu.SMEM)],
        out_specs=pl.BlockSpec((1, tile, tile), lambda c, h: (c, 0, h)),
        out_shape=jax.ShapeDtypeStruct((n_cls, tile, N_HEADS * tile), F32),
        scratch_shapes=[pltpu.VMEM((tile, tile), jnp.int32)],
        compiler_params=_params(("parallel", "arbitrary")),
        name="causal_bias_table",
    )(rel_table)


def cmp_bias_table(rel_table, seqlen, n_cmp):
    tq = LANES
    kern = functools.partial(_bias_table_kernel, base0=-(NSA_CMP_LEN - 1), base_step=tq,
                             key_stride=NSA_CMP_STRIDE, max_dist=2 ** 30, n_valid_keys=n_cmp, dist_scale=1)
    return pl.pallas_call(
        kern,
        grid=(seqlen // tq, N_HEADS),
        in_specs=[pl.BlockSpec(memory_space=pltpu.SMEM)],
        out_specs=pl.BlockSpec((1, LANES, tq), lambda i, h: (i, 0, h)),
        out_shape=jax.ShapeDtypeStruct((seqlen // tq, LANES, N_HEADS * tq), F32),
        scratch_shapes=[pltpu.VMEM((LANES, tq), jnp.int32)],
        compiler_params=_params(("parallel", "arbitrary")),
        name="cmp_bias_table",
    )(rel_table)


def _banded_kernel(*refs, n_rep, n_grp, tq, n_prev, seq, has_sink, want_lse):
    q_ref, k_ref, v_ref, bias_ref = refs[:4]
    pos = 4
    sink_ref = None
    if has_sink:
        sink_ref = refs[pos]
        pos += 1
    o_ref = refs[pos]
    pos += 1
    lse_ref = None
    if want_lse:
        lse_ref = refs[pos]
        pos += 1
    kpad_ref, vpad_ref = refs[pos:pos + 2]

    i = pl.program_id(3)
    pad = n_prev * tq
    span = pad + tq
    rows = n_rep * tq
    qw = n_rep * HEAD_DIM

    @pl.when(i == 0)
    def _():
        if pad:
            kpad_ref[0:pad, :] = jnp.zeros((pad, n_grp * HEAD_DIM), BF16)
            vpad_ref[0:pad, :] = jnp.zeros((pad, n_grp * HEAD_DIM), BF16)
        kpad_ref[pad:pad + seq, :] = k_ref[0]
        vpad_ref[pad:pad + seq, :] = v_ref[0]

    start = pl.multiple_of(i * tq, tq)
    for gg in range(n_grp):
        kv_cols = slice(gg * HEAD_DIM, (gg + 1) * HEAD_DIM)
        row_cols = slice(gg * rows, (gg + 1) * rows)
        ks = kpad_ref[pl.ds(start, span), kv_cols]
        vs = vpad_ref[pl.ds(start, span), kv_cols]
        qs = _stack_heads(q_ref[0, :, gg * qw:(gg + 1) * qw], n_rep)
        lt = _dot_nt(ks, qs) * SCORE_SCALE +bias_ref[:, row_cols]
        if pad:
            key = lax.broadcasted_iota(jnp.int32, (span, rows), 0)
            lt = jnp.where(key >= pad - i * tq, lt, NEG_INF)
        m = jnp.max(lt, axis=0, keepdims=True)
        sink = None
        if has_sink:
            sink = sink_ref[:, row_cols] * LOG2E
            m = jnp.maximum(m, sink)
        p = jnp.exp2(lt - m)
        s = jnp.sum(p, axis=0, keepdims=True)
        if has_sink:
            s = s + jnp.exp2(sink - m)
        o_t = _dot_tn(vs, p.astype(BF16))
        s = jnp.maximum(s, TINY)
        _store_heads(o_ref, o_t / s, n_rep, tq, gg * qw)
        if want_lse:
            lse = (m + jnp.log2(s)) * LN2
            head = lax.broadcasted_iota(jnp.int32, (LANES, tq), 0)
            tile = jnp.zeros((LANES, tq), F32)
            for r in range(n_rep):
                tile = jnp.where(head == r, lse[:, r * tq:(r + 1) * tq], tile)
            lse_ref[gg, 0] = tile.T


def banded_attention(q_arr, q_off, kv_arr, k_off, v_off, bias, *, batch, seqlen, n_kv, dil,
                     max_dist, sink_row=None, want_lse=False, out_dtype=F32):
    n_rep = N_HEADS // n_kv
    sub = seqlen // dil
    tq = math.gcd(sub, BAND_BLOCK)
    n_blk = sub // tq
    n_prev = min(-(-max_dist // tq), n_blk - 1)
    span = (n_prev + 1) * tq
    qw = n_rep * HEAD_DIM
    rows = n_rep * tq
    n_grp = 2 if rows <= 512 and n_kv % 2 == 0 else 1
    gqw, gkw = n_grp * qw, n_grp * HEAD_DIM
    assert q_arr.shape[:3] == kv_arr.shape[:3] == (batch, dil, sub)
    assert q_off % gqw == 0 and k_off % gkw == 0 and v_off % gkw == 0 and bias.shape == (span, N_HEADS * tq)

    in_specs = [
        pl.BlockSpec((None, 1, tq, gqw), lambda b, rho, g, i: (b, rho, i, q_off // gqw + g)),
        pl.BlockSpec((None, 1, sub, gkw), lambda b, rho, g, i: (b, rho, 0, k_off // gkw + g)),
        pl.BlockSpec((None, 1, sub, gkw), lambda b, rho, g, i: (b, rho, 0, v_off // gkw + g)),
        pl.BlockSpec((span, n_grp * rows), lambda b, rho, g, i: (0, g)),
    ]
    args = [q_arr, kv_arr, kv_arr, bias]
    if sink_row is not None:
        in_specs.append(pl.BlockSpec((1, n_grp * rows), lambda b, rho, g, i: (0, g)))
        args.append(sink_row)
    out_specs = [pl.BlockSpec((None, 1, tq, gqw), lambda b, rho, g, i: (b, rho, i, g))]
    out_shape = [jax.ShapeDtypeStruct((batch, dil, sub, ATTN_WIDTH), out_dtype)]
    if want_lse:
        out_specs.append(pl.BlockSpec((None, n_grp, 1, tq, LANES), lambda b, rho, g, i: (b, g, rho, i, 0)))
        out_shape.append(jax.ShapeDtypeStruct((batch, n_kv, dil, sub, LANES), F32))
    kern = functools.partial(_banded_kernel, n_rep=n_rep, n_grp=n_grp, tq=tq, n_prev=n_prev, seq=sub,
                             has_sink=sink_row is not None, want_lse=want_lse)
    outs = pl.pallas_call(
        kern,
        grid=(batch, dil, n_kv // n_grp, n_blk),
        in_specs=in_specs,
        out_specs=out_specs,
        out_shape=out_shape,
        scratch_shapes=[pltpu.VMEM((n_prev * tq + sub, gkw), BF16),
                        pltpu.VMEM((n_prev * tq + sub, gkw), BF16)],
        compiler_params=_params(("parallel", "parallel", "parallel", "arbitrary")),
        name="banded_attention",
    )(*args)
    return tuple(outs) if want_lse else outs[0]


def _flash_init(m_ref, s_ref, acc_ref):
    m_ref[...] = jnp.full(m_ref.shape, NEG_INF, F32)
    s_ref[...] = jnp.zeros(s_ref.shape, F32)
    acc_ref[...] = jnp.zeros(acc_ref.shape, F32)


def _flash_step(lt, v_blk, m_ref, s_ref, acc_ref):
    m_prev = m_ref[...]
    m_new = jnp.maximum(m_prev, jnp.max(lt, axis=0, keepdims=True))
    alpha = jnp.exp2(m_prev - m_new)
    p = jnp.exp2(lt - m_new)
    s_ref[...] = alpha * s_ref[...] + jnp.sum(p, axis=0, keepdims=True)
    acc_ref[...] = alpha * acc_ref[...] + _dot_tn(v_blk, p.astype(BF16))
    m_ref[...] = m_new


def _flash_finish(o_ref, s_ref, acc_ref, n_rep, tq):
    _store_heads(o_ref, acc_ref[...] / jnp.maximum(s_ref[...], TINY), n_rep, tq)


def _first_rank(score, n_cand):
    idx = lax.broadcasted_iota(jnp.int32, score.shape, 0)
    rank = jnp.zeros(score.shape, F32)
    for jp in range(n_cand):
        other = score[jp:jp + 1, :]
        ahead = jnp.where(other > score, 1.0, jnp.where(other == score, jnp.where(idx > jp, 1.0, 0.0), 0.0))
        rank = rank + ahead
    return rank


def _nsa_cmp_kernel(x_ref, pos_ref, w1k_ref, w2k_ref, w1v_ref, w2v_ref, ko_ref, vo_ref):
    width = 2 * NSA_KV_HEADS * HEAD_DIM
    for kv, (w1_ref, w2_ref, o_ref) in enumerate(((w1k_ref, w2k_ref, ko_ref), (w1v_ref, w2v_ref, vo_ref))):
        for g in range(NSA_KV_HEADS):
            off = kv * NSA_KV_HEADS * HEAD_DIM + g * HEAD_DIM
            chunk = jnp.concatenate(
                [x_ref[0, :, l * width + off:l * width + off + HEAD_DIM] for l in range(NSA_CMP_STRIDE)], axis=1)
            first = jnp.dot((chunk + pos_ref[0:1, :]).astype(BF16), w1_ref[0], preferred_element_type=F32)
            second = jnp.dot((chunk + pos_ref[1:2, :]).astype(BF16), w1_ref[1], preferred_element_type=F32)
            hidden = jax.nn.gelu(first + pltpu.roll(second, second.shape[0] - 1, axis=0))
            o_ref[0, g] = jnp.dot(hidden.astype(BF16), w2_ref[...], preferred_element_type=F32).astype(o_ref.dtype)


def nsa_compress(kcvc, cmp_pos, k_w1, k_w2, v_w1, v_w2, batch, seqlen):
    n_chunk = seqlen // NSA_CMP_STRIDE
    width = 2 * NSA_KV_HEADS * HEAD_DIM
    half = NSA_CMP_STRIDE * HEAD_DIM
    x = kcvc.reshape(batch, n_chunk, NSA_CMP_STRIDE * width)
    out = jax.ShapeDtypeStruct((batch, NSA_KV_HEADS, n_chunk, HEAD_DIM), BF16)
    full = lambda shape: pl.BlockSpec(shape, lambda b: (0,) * len(shape))
    return pl.pallas_call(
        _nsa_cmp_kernel,
        grid=(batch,),
        in_specs=[pl.BlockSpec((1, n_chunk, NSA_CMP_STRIDE * width), lambda b: (b, 0, 0)),
                  full((2, half)), full((2, half, HEAD_DIM)), full((HEAD_DIM, HEAD_DIM)),
                  full((2, half, HEAD_DIM)), full((HEAD_DIM, HEAD_DIM))],
        out_specs=[pl.BlockSpec((1, NSA_KV_HEADS, n_chunk, HEAD_DIM), lambda b: (b, 0, 0, 0))] * 2,
        out_shape=[out, out],
        compiler_params=_params(("parallel",)),
        name="nsa_compress",
    )(x, cmp_pos.reshape(2, half), k_w1.reshape(2, half, HEAD_DIM).astype(BF16), k_w2.astype(BF16),
      v_w1.reshape(2, half, HEAD_DIM).astype(BF16), v_w2.astype(BF16))


def _nsa_cmp_attn_kernel(q_ref, kc_ref, vc_ref, bias_ref, c2s_ref, o_ref, sel_ref, *, n_rep, tq, n_sel_blk):
    i = pl.program_id(2)
    qs = _stack_heads(q_ref[0], n_rep)
    bias = bias_ref[0]
    valid = bias > MASKED_BELOW
    lt = jnp.where(valid, _dot_nt(kc_ref[0, 0], qs) * SCORE_SCALE +bias, NEG_INF)
    m = jnp.max(lt, axis=0, keepdims=True)
    p = jnp.where(valid, jnp.exp2(lt - m), 0.0)
    s = jnp.sum(p, axis=0, keepdims=True)
    p_cmp = p / jnp.maximum(s, TINY)
    _store_heads(o_ref, _dot_tn(vc_ref[0, 0], p_cmp.astype(BF16)), n_rep, tq)

    p_sum = p_cmp[:, 0:tq]
    for r in range(1, n_rep):
        p_sum = p_sum + p_cmp[:, r * tq:(r + 1) * tq]
    imp = jnp.dot(c2s_ref[...], p_sum.astype(BF16), preferred_element_type=F32)[0:n_sel_blk]
    blk = lax.broadcasted_iota(jnp.int32, (n_sel_blk, tq), 0)
    tpos = i * tq + lax.broadcasted_iota(jnp.int32, (n_sel_blk, tq), 1)
    cur = tpos // NSA_SEL_LEN
    forced = (blk == 0) | (blk == cur) | (blk == cur - 1)
    score = jnp.where(forced, FORCED_SCORE, jnp.where(blk * NSA_SEL_LEN <= tpos, imp, NEG_INF))
    sel_ref[0, 0] = jnp.where(_first_rank(score, n_sel_blk) < min(NSA_SEL_TOPN, n_sel_blk), 1.0, 0.0)


def nsa_cmp_attention(q, kcmp, vcmp, bias, batch, seqlen):
    n_rep = N_HEADS // NSA_KV_HEADS
    tq = LANES
    qw = n_rep * HEAD_DIM
    n_sel_blk = seqlen // NSA_SEL_LEN
    n_cmp = (seqlen - NSA_CMP_LEN) // NSA_CMP_STRIDE + 1
    a, b = NSA_SEL_LEN // NSA_CMP_STRIDE, NSA_CMP_LEN // NSA_CMP_STRIDE
    w = np.zeros((LANES, LANES), np.float32)
    j = np.arange(n_sel_blk)
    for mm in range(a):
        for nn in range(b):
            ii = a * j + mm + nn - (b - 1)
            ok = (ii >= 0) & (ii < n_cmp)
            np.add.at(w, (j[ok], ii[ok]), 1.0)
    kern = functools.partial(_nsa_cmp_attn_kernel, n_rep=n_rep, tq=tq, n_sel_blk=n_sel_blk)
    return pl.pallas_call(
        kern,
        grid=(batch, NSA_KV_HEADS, seqlen // tq),
        in_specs=[pl.BlockSpec((1, tq, qw), lambda b_, g, i: (b_, i, g)),
                  pl.BlockSpec((1, 1, LANES, HEAD_DIM), lambda b_, g, i: (b_, g, 0, 0)),
                  pl.BlockSpec((1, 1, LANES, HEAD_DIM), lambda b_, g, i: (b_, g, 0, 0)),
                  pl.BlockSpec((1, LANES, n_rep * tq), lambda b_, g, i: (i, 0, g)),
                  pl.BlockSpec((LANES, LANES), lambda b_, g, i: (0, 0))],
        out_specs=[pl.BlockSpec((1, tq, qw), lambda b_, g, i: (b_, i, g)),
                   pl.BlockSpec((1, 1, n_sel_blk, tq), lambda b_, g, i: (b_, g, 0, i))],
        out_shape=[jax.ShapeDtypeStruct((batch, seqlen, ATTN_WIDTH), F32),
                   jax.ShapeDtypeStruct((batch, NSA_KV_HEADS, n_sel_blk, seqlen), F32)],
        compiler_params=_params(("parallel", "parallel", "arbitrary")),
        name="nsa_cmp_attention",
    )(q.reshape(batch, seqlen, ATTN_WIDTH), kcmp, vcmp, bias, jnp.asarray(w, BF16))


def _visit_past_chunks(i, visit):
    def pair(pp, carry):
        visit(i - 2 - 2 * pp, 2)
        return carry

    lax.fori_loop(0, i // 2, pair, 0)

    @pl.when((i & 1) == 1)
    def _():
        visit(0, 1)


def _chunk_bias(bias_ref, i, c_lo, n):
    if n == 1:
        return bias_ref[i - c_lo]
    return jnp.concatenate([bias_ref[i - c_lo - t] for t in range(n)], axis=0)


def _nsa_sel_kernel(q_ref, k_ref, v_ref, sel_ref, bias_ref, o_ref, m_ref, s_ref, acc_ref, *, n_rep, tq):
    i = pl.program_id(2)
    qs = _stack_heads(q_ref[0], n_rep)
    per = tq // NSA_SEL_LEN
    _flash_init(m_ref, s_ref, acc_ref)

    def visit(c_lo, n):
        start = pl.multiple_of(c_lo * tq, tq)
        lt = _dot_nt(k_ref[0, pl.ds(start, n * tq), :], qs) * SCORE_SCALE +_chunk_bias(bias_ref, i, c_lo, n)
        slabs = []
        for b in range(n * per):
            on = sel_ref[0, 0, pl.ds(c_lo * per + b, 1), :]
            on = jnp.concatenate([on] * n_rep, axis=1)
            slabs.append(jnp.where(on > 0.5, lt[b * NSA_SEL_LEN:(b + 1) * NSA_SEL_LEN], NEG_INF))
        _flash_step(jnp.concatenate(slabs, axis=0), v_ref[0, pl.ds(start, n * tq), :], m_ref, s_ref, acc_ref)

    visit(i, 1)
    _visit_past_chunks(i, visit)
    _flash_finish(o_ref, s_ref, acc_ref, n_rep, tq)


def nsa_selected_attention(q, kv, k_off, v_off, sel, causal_bias, batch, seqlen):
    n_rep = N_HEADS // NSA_KV_HEADS
    tq = SEL_TILE
    qw = n_rep * HEAD_DIM
    n_blk = seqlen // tq
    n_sel_blk = seqlen // NSA_SEL_LEN
    ckv = kv.shape[1]
    kv3 = kv.reshape(batch, seqlen, ckv)
    kern = functools.partial(_nsa_sel_kernel, n_rep=n_rep, tq=tq)
    rows = n_rep * tq
    return pl.pallas_call(
        kern,
        grid=(batch, NSA_KV_HEADS, n_blk),
        in_specs=[pl.BlockSpec((1, tq, qw), lambda b, g, i: (b, i, g)),
                  pl.BlockSpec((1, seqlen, HEAD_DIM), lambda b, g, i: (b, 0, k_off // HEAD_DIM + g)),
                  pl.BlockSpec((1, seqlen, HEAD_DIM), lambda b, g, i: (b, 0, v_off // HEAD_DIM + g)),
                  pl.BlockSpec((1, 1, n_sel_blk, tq), lambda b, g, i: (b, g, 0, i)),
                  pl.BlockSpec((n_blk, tq, rows), lambda b, g, i: (0, 0, g), pipeline_mode=pl.Buffered(1))],
        out_specs=pl.BlockSpec((1, tq, qw), lambda b, g, i: (b, i, g)),
        out_shape=jax.ShapeDtypeStruct((batch, seqlen, ATTN_WIDTH), F32),
        scratch_shapes=[pltpu.VMEM((1, rows), F32), pltpu.VMEM((1, rows), F32),
                        pltpu.VMEM((HEAD_DIM, rows), F32)],
        compiler_params=_params(("parallel", "parallel", "arbitrary")),
        name="nsa_selected_attention",
    )(q.reshape(batch, seqlen, ATTN_WIDTH), kv3, kv3, sel, causal_bias)


def _nsa_gate_kernel(g_ref, oc_ref, os_ref, ow_ref, o_ref):
    gate = jax.nn.sigmoid(g_ref[...])
    for h in range(N_HEADS):
        cols = slice(h * HEAD_DIM, (h + 1) * HEAD_DIM)
        mix = (gate[:, 3 * h:3 * h + 1] * oc_ref[:, cols] + gate[:, 3 * h + 1:3 * h + 2] * os_ref[:, cols]
               + gate[:, 3 * h + 2:3 * h + 3] * ow_ref[:, cols])
        o_ref[:, cols] = mix.astype(o_ref.dtype)


def nsa_gate_combine(gates, o_cmp, o_slc, o_win):
    m = gates.shape[0]
    tm = 128
    wide = pl.BlockSpec((tm, ATTN_WIDTH), lambda i: (i, 0))
    return pl.pallas_call(
        _nsa_gate_kernel,
        grid=(m // tm,),
        in_specs=[pl.BlockSpec((tm, gates.shape[1]), lambda i: (i, 0)), wide, wide, wide],
        out_specs=wide,
        out_shape=jax.ShapeDtypeStruct((m, ATTN_WIDTH), BF16),
        compiler_params=_params(("parallel",)),
        name="nsa_gate_combine",
    )(gates, o_cmp, o_slc, o_win)


def _moba_kernel(q_ref, k_ref, v_ref, bias_ref, o_ref, kb_ref, vb_ref, km_ref, m_ref, s_ref, acc_ref,
                 *, n_rep, tq, n_blk):
    i = pl.program_id(2)
    rows = n_rep * tq

    @pl.when(i == 0)
    def _():
        k = k_ref[0]
        kb_ref[...] = k.astype(BF16)
        vb_ref[...] = v_ref[0].astype(BF16)
        slot = lax.broadcasted_iota(jnp.int32, (BF16_ROWS, HEAD_DIM), 0)
        means = jnp.zeros((BF16_ROWS, HEAD_DIM), F32)
        for j in range(n_blk):
            means = jnp.where(slot == j, jnp.mean(k[j * tq:(j + 1) * tq], axis=0, keepdims=True), means)
        km_ref[...] = means.astype(BF16)

    qs = _stack_heads(q_ref[0], n_rep)
    gate = _dot_nt(km_ref[...], qs)[0:n_blk]
    blk = lax.broadcasted_iota(jnp.int32, (n_blk, rows), 0)
    past = blk < i
    rank = _first_rank(jnp.where(past, gate, NEG_INF), n_blk)
    chosen = jnp.where(past, jnp.where(rank < min(MOBA_TOPK, max(n_blk - 1, 1)), 1.0, 0.0), 0.0)

    _flash_init(m_ref, s_ref, acc_ref)
    own = pl.multiple_of(i * tq, tq)
    lt = _dot_nt(kb_ref[pl.ds(own, tq), :], qs) * SCORE_SCALE +bias_ref[0]
    _flash_step(lt, vb_ref[pl.ds(own, tq), :], m_ref, s_ref, acc_ref)

    def visit(c_lo, n):
        start = pl.multiple_of(c_lo * tq, tq)
        lt = _dot_nt(kb_ref[pl.ds(start, n * tq), :], qs) * SCORE_SCALE +_chunk_bias(bias_ref, i, c_lo, n)
        parts = []
        for t in range(n):
            row_on = jnp.sum(jnp.where(blk == c_lo + t, chosen, 0.0), axis=0, keepdims=True)
            parts.append(jnp.where(row_on > 0.5, lt[t * tq:(t + 1) * tq], NEG_INF))
        _flash_step(jnp.concatenate(parts, axis=0), vb_ref[pl.ds(start, n * tq), :], m_ref, s_ref, acc_ref)

    _visit_past_chunks(i, visit)
    _flash_finish(o_ref, s_ref, acc_ref, n_rep, tq)


def moba_attention(q, kv, causal_bias, batch, seqlen):
    n_rep = N_HEADS // MOBA_KV_HEADS
    tq = MOBA_BLOCK
    qw = n_rep * HEAD_DIM
    n_blk = seqlen // tq
    assert n_blk <= SUBLANES
    rows = n_rep * tq
    kv3 = kv.reshape(batch, seqlen, kv.shape[1])
    kern = functools.partial(_moba_kernel, n_rep=n_rep, tq=tq, n_blk=n_blk)
    return pl.pallas_call(
        kern,
        grid=(batch, MOBA_KV_HEADS, n_blk),
        in_specs=[pl.BlockSpec((1, tq, qw), lambda b, g, i: (b, i, g)),
                  pl.BlockSpec((1, seqlen, HEAD_DIM), lambda b, g, i: (b, 0, g)),
                  pl.BlockSpec((1, seqlen, HEAD_DIM), lambda b, g, i: (b, 0, MOBA_KV_HEADS + g)),
                  pl.BlockSpec((n_blk, tq, rows), lambda b, g, i: (0, 0, g), pipeline_mode=pl.Buffered(1))],
        out_specs=pl.BlockSpec((1, tq, qw), lambda b, g, i: (b, i, g)),
        out_shape=jax.ShapeDtypeStruct((batch, seqlen, ATTN_WIDTH), BF16),
        scratch_shapes=[pltpu.VMEM((seqlen, HEAD_DIM), BF16), pltpu.VMEM((seqlen, HEAD_DIM), BF16),
                        pltpu.VMEM((BF16_ROWS, HEAD_DIM), BF16),
                        pltpu.VMEM((1, rows), F32), pltpu.VMEM((1, rows), F32),
                        pltpu.VMEM((HEAD_DIM, rows), F32)],
        compiler_params=_params(("parallel", "parallel", "arbitrary")),
        name="moba_attention",
    )(q.reshape(batch, seqlen, ATTN_WIDTH), kv3, kv3, causal_bias)


def _dil_combine_kernel(*refs, n_rep, dils, tile):
    n_grp = len(dils)
    o_refs, l_refs, o_ref = refs[:n_grp], refs[n_grp:2 * n_grp], refs[2 * n_grp]
    scratch = list(refs[2 * n_grp + 1:])
    outs, lses = [], []
    for o_g, l_g, dil in zip(o_refs, l_refs, dils):
        if dil == 1:
            outs.append([o_g[0, :, r * HEAD_DIM:(r + 1) * HEAD_DIM] for r in range(n_rep)])
            lses.append(l_g[0])
            continue
        nat_o, nat_l = scratch.pop(0), scratch.pop(0)
        per = tile // dil
        for rho in range(dil):
            for r in range(n_rep):
                nat_o[r, pl.ds(rho, per, stride=dil), :] = o_g[rho, :, r * HEAD_DIM:(r + 1) * HEAD_DIM]
            nat_l[pl.ds(rho, per, stride=dil), :] = l_g[rho]
        outs.append([nat_o[r] for r in range(n_rep)])
        lses.append(nat_l[...])
    top = functools.reduce(jnp.maximum, lses)
    weights = [jnp.exp(l - top) for l in lses]
    den = functools.reduce(lambda x, y: x + y, weights)
    weights = [w / den for w in weights]
    for r in range(n_rep):
        mix = weights[0][:, r:r + 1] * outs[0][r]
        for w, o in zip(weights[1:], outs[1:]):
            mix = mix + w[:, r:r + 1] * o[r]
        o_ref[0, :, r * HEAD_DIM:(r + 1) * HEAD_DIM] = mix.astype(o_ref.dtype)


def dilated_combine(outs, lses, dils, batch, seqlen):
    n_rep = N_HEADS // DIL_KV_HEADS
    tile = SEL_TILE
    qw = n_rep * HEAD_DIM
    in_specs, scratch = [], []
    for dil in dils:
        in_specs.append(pl.BlockSpec((None, dil, tile // dil, qw), lambda b, g, i: (b, 0, i, g)))
    for dil in dils:
        in_specs.append(pl.BlockSpec((None, None, dil, tile // dil, LANES), lambda b, g, i: (b, g, 0, i, 0)))
        if dil > 1:
            scratch += [pltpu.VMEM((n_rep, tile, HEAD_DIM), F32), pltpu.VMEM((tile, LANES), F32)]
    return pl.pallas_call(
        functools.partial(_dil_combine_kernel, n_rep=n_rep, dils=tuple(dils), tile=tile),
        grid=(batch, DIL_KV_HEADS, seqlen // tile),
        in_specs=in_specs,
        out_specs=pl.BlockSpec((1, tile, qw), lambda b, g, i: (b, i, g)),
        out_shape=jax.ShapeDtypeStruct((batch, seqlen, ATTN_WIDTH), BF16),
        scratch_shapes=scratch,
        compiler_params=_params(("parallel", "parallel", "parallel")),
        name="dilated_combine",
    )(*outs, *lses)


def nsa_mixer(act, h, w_in, j, cmp_pos, k_w1, k_w2, v_w1, v_w2, w_out, next_gain, tables, batch, seqlen):
    kvw = NSA_KV_HEADS * HEAD_DIM
    c0 = ATTN_WIDTH
    tokens = batch * seqlen
    q = matmul(act,w_in, j, 0, c0, BF16)
    kcvc = matmul(act,w_in, j, c0, 2 * kvw, F32)
    kvsw = matmul(act,w_in, j, c0 + 2 * kvw, 4 * kvw, BF16)
    gates = matmul(act,w_in[:, :, c0 + 6 * kvw:], j, 0, 3 * N_HEADS, F32)
    kcmp, vcmp = nsa_compress(kcvc, cmp_pos, k_w1, k_w2, v_w1, v_w2, batch, seqlen)
    o_cmp, sel = nsa_cmp_attention(q, kcmp, vcmp, tables["cmp"], batch, seqlen)
    o_slc = nsa_selected_attention(q, kvsw, 0, kvw, sel, tables["causal"], batch, seqlen)
    o_win = banded_attention(q.reshape(batch, 1, seqlen, c0), 0, kvsw.reshape(batch, 1, seqlen, 4 * kvw),
                             2 * kvw, 3 * kvw, tables["nsa_win"], batch=batch, seqlen=seqlen,
                             n_kv=NSA_KV_HEADS, dil=1, max_dist=NSA_WINDOW - 1)
    o = nsa_gate_combine(gates, o_cmp.reshape(tokens, ATTN_WIDTH), o_slc.reshape(tokens, ATTN_WIDTH),
                         o_win.reshape(tokens, ATTN_WIDTH))
    return matmul_residual(o, w_out, j, h, 1.0, next_gain)


def dilated_mixer(act, h, w_in, j, w_out, next_gain, tables, batch, seqlen):
    kvw = DIL_KV_HEADS * HEAD_DIM
    group = ATTN_WIDTH + 2 * kvw
    outs, lses, dils = [], [], []
    for gi, (window, dil) in enumerate(DIL_PAIRS):
        proj = matmul(act,w_in, j, gi * group, group, BF16, dil=dil, batch=batch)
        proj = proj.reshape(batch, dil, seqlen // dil, group)
        o, lse = banded_attention(proj, 0, proj, ATTN_WIDTH, ATTN_WIDTH + kvw, tables["dil%d" % dil],
                                  batch=batch, seqlen=seqlen, n_kv=DIL_KV_HEADS, dil=dil,
                                  max_dist=window // dil, want_lse=True)
        outs.append(o)
        lses.append(lse)
        dils.append(dil)
    o = dilated_combine(outs, lses, dils, batch, seqlen)
    return matmul_residual(o.reshape(batch * seqlen, ATTN_WIDTH), w_out, j, h, 1.0, next_gain)


def moba_mixer(act, h, w_in, j, w_out, next_gain, tables, batch, seqlen):
    q = matmul(act,w_in, j, 0, ATTN_WIDTH, BF16)
    kv = matmul(act,w_in, j, ATTN_WIDTH, 2 * MOBA_KV_HEADS * HEAD_DIM, F32)
    o = moba_attention(q, kv, tables["causal"], batch, seqlen)
    return matmul_residual(o.reshape(batch * seqlen, ATTN_WIDTH), w_out, j, h, 1.0, next_gain)


def swa_mixer(act, h, w_in, j, sinks, w_out, next_gain, tables, batch, seqlen):
    kvw = SWA_KV_HEADS * HEAD_DIM
    width = ATTN_WIDTH + 2 * kvw
    proj = matmul(act,w_in, j, 0, width, BF16).reshape(batch, 1, seqlen, width)
    sink_row = jnp.repeat(sinks, math.gcd(seqlen, BAND_BLOCK))[None, :]
    o = banded_attention(proj, 0, proj, ATTN_WIDTH, ATTN_WIDTH + kvw, tables["swa"], batch=batch,
                         seqlen=seqlen, n_kv=SWA_KV_HEADS, dil=1, max_dist=SWA_WINDOW - 1, sink_row=sink_row,
                         out_dtype=BF16)
    return matmul_residual(o.reshape(batch * seqlen, ATTN_WIDTH), w_out, j, h, 1.0, next_gain)


def _band_table_for(rel_table, seqlen, dil, max_dist):
    sub = seqlen // dil
    tq = math.gcd(sub, BAND_BLOCK)
    n_prev = min(-(-max_dist // tq), sub // tq - 1)
    return band_bias_table(rel_table, tq, (n_prev + 1) * tq, n_prev * tq, max_dist, dil)


def kernel(x, rel_table, ffn1_norm, ffn1_w_gate, ffn1_w_up, ffn1_w_down, mix_norm, ffn2_norm, ffn2_w_gate, ffn2_w_up, ffn2_w_down, final_norm, nsa_w_in, nsa_cmp_pos, nsa_cmp_k_w1, nsa_cmp_k_w2, nsa_cmp_v_w1, nsa_cmp_v_w2, nsa_w_out, dil_w_in, dil_w_out, moba_w_in, moba_w_out, swa_w_in, swa_sinks, swa_w_out):
    batch, seqlen, d_model = x.shape
    depth = ffn1_norm.shape[0]
    n_mixers = 4
    h = x.reshape(batch * seqlen, d_model)

    tables = {
        "causal": causal_bias_table(rel_table, SEL_TILE, seqlen // SEL_TILE),
        "cmp": cmp_bias_table(rel_table, seqlen, (seqlen - NSA_CMP_LEN) // NSA_CMP_STRIDE + 1),
        "nsa_win": _band_table_for(rel_table, seqlen, 1, NSA_WINDOW - 1),
        "swa": _band_table_for(rel_table, seqlen, 1, SWA_WINDOW - 1),
    }
    for window, dil in DIL_PAIRS:
        tables["dil%d" % dil] = _band_table_for(rel_table, seqlen, dil, window // dil)

    act = norm_prep(h, ffn1_norm[0])
    for i in range(depth):
        h, act = ffn_half_step(h, act, ffn1_w_gate, ffn1_w_up, ffn1_w_down, i, mix_norm[i])
        m, j = i % n_mixers, i // n_mixers
        if m == 0:
            h, act = nsa_mixer(act, h, nsa_w_in, j, nsa_cmp_pos[j], nsa_cmp_k_w1[j], nsa_cmp_k_w2[j],
                               nsa_cmp_v_w1[j], nsa_cmp_v_w2[j], nsa_w_out, ffn2_norm[i], tables, batch, seqlen)
        elif m == 1:
            h, act = dilated_mixer(act, h, dil_w_in, j, dil_w_out, ffn2_norm[i], tables, batch, seqlen)
        elif m == 2:
            h, act = moba_mixer(act, h, moba_w_in, j, moba_w_out, ffn2_norm[i], tables, batch, seqlen)
        else:
            h, act = swa_mixer(act, h, swa_w_in, j, swa_sinks[j], swa_w_out, ffn2_norm[i], tables, batch, seqlen)
        next_gain = ffn1_norm[i + 1] if i + 1 < depth else None
        h, act = ffn_half_step(h, act, ffn2_w_gate, ffn2_w_up, ffn2_w_down, i, next_gain)
    return rms_norm(h, final_norm, x.dtype).reshape(batch, seqlen, d_model)
```

```python
import functools
import math

import numpy as np
import jax
import jax.numpy as jnp
from jax import lax
from jax.experimental import pallas as pl
from jax.experimental.pallas import tpu as pltpu

HEAD_DIM = 128
N_HEADS = 32
ATTN_WIDTH = N_HEADS * HEAD_DIM
RMS_EPS = 1e-6
REL_BUCKETS = 32
REL_MAX_DIST = 2048
BAND_BLOCK = 128
NSA_KV_HEADS = 4
NSA_CMP_LEN = 32
NSA_CMP_STRIDE = 16
NSA_SEL_LEN = 64
NSA_SEL_TOPN = 16
NSA_WINDOW = 512
DIL_PAIRS = ((128, 1), (512, 4), (2048, 16))
DIL_KV_HEADS = 8
MOBA_BLOCK = 256
MOBA_TOPK = 3
MOBA_KV_HEADS = 8
SWA_WINDOW = 128
SWA_KV_HEADS = 4
ATTN_SCALE = HEAD_DIM ** -0.5
LOG2E = 1.0 / math.log(2.0)
LN2 = math.log(2.0)
SCORE_SCALE = ATTN_SCALE * LOG2E
NEG_INF = -1e30
MASKED_BELOW = -5e29
TINY = 1e-20
FORCED_SCORE = 1e9

LANES = 128
SUBLANES = 8
BF16_ROWS = 16
SEL_TILE = 256
VMEM_LIMIT = 56 * 1024 * 1024

F32 = jnp.float32
BF16 = jnp.bfloat16


def _params(semantics):
    return pltpu.CompilerParams(dimension_semantics=semantics, vmem_limit_bytes=VMEM_LIMIT)


def _dot_nt(a, b):
    return lax.dot_general(a, b, (((1,), (1,)), ((), ())), preferred_element_type=F32)


def _dot_tn(a, b):
    return lax.dot_general(a, b, (((0,), (0,)), ((), ())), preferred_element_type=F32)


def _stack_heads(q, n_heads):
    return jnp.concatenate([q[:, r * HEAD_DIM:(r + 1) * HEAD_DIM] for r in range(n_heads)], axis=0)


def _store_heads(o_ref, o_t, n_heads, tq, col0=0):
    for r in range(n_heads):
        cols = slice(col0 + r * HEAD_DIM, col0 + (r + 1) * HEAD_DIM)
        o_ref[0, :, cols] = o_t[:, r * tq:(r + 1) * tq].T.astype(o_ref.dtype)


def _rms_kernel(x_ref, g_ref, o_ref):
    x = x_ref[...]
    y = x * lax.rsqrt(jnp.mean(x * x, axis=-1, keepdims=True) + RMS_EPS)
    o_ref[...] = (y * g_ref[...]).astype(o_ref.dtype)


def rms_norm(x, gain, out_dtype):
    m, d = x.shape
    tm = 256
    return pl.pallas_call(
        _rms_kernel,
        grid=(m // tm,),
        in_specs=[pl.BlockSpec((tm, d), lambda i: (i, 0)), pl.BlockSpec((1, d), lambda i: (0, 0))],
        out_specs=pl.BlockSpec((tm, d), lambda i: (i, 0)),
        out_shape=jax.ShapeDtypeStruct((m, d), out_dtype),
        compiler_params=_params(("parallel",)),
        name="rms_norm",
    )(x, gain.reshape(1, d))


def _norm_prep_kernel(x_ref, g_ref, a_ref, ssq_ref):
    x = x_ref[...]
    a_ref[...] = (x * g_ref[...]).astype(a_ref.dtype)
    ssq_ref[...] = jnp.broadcast_to(jnp.sum(x * x, axis=-1, keepdims=True), ssq_ref.shape)


def norm_prep(x, gain):
    m, d = x.shape
    tm = 256
    return pl.pallas_call(
        _norm_prep_kernel,
        grid=(m // tm,),
        in_specs=[pl.BlockSpec((tm, d), lambda i: (i, 0)), pl.BlockSpec((1, d), lambda i: (0, 0))],
        out_specs=[pl.BlockSpec((tm, d), lambda i: (i, 0)), pl.BlockSpec((tm, LANES), lambda i: (i, 0))],
        out_shape=[jax.ShapeDtypeStruct((m, d), BF16), jax.ShapeDtypeStruct((m, LANES), F32)],
        compiler_params=_params(("parallel",)),
        name="norm_prep",
    )(x, gain.reshape(1, d))


def _row_scale(ssq_ref, d):
    return lax.rsqrt(ssq_ref[:, 0:1] * (1.0 / d) + RMS_EPS)


MM_VMEM_BUDGET = 50 * 1024 * 1024


def _mm_kernel(a_ref, ssq_ref, b_ref, o_ref, *scratch, dil):
    res = jnp.dot(a_ref[...], b_ref[...].astype(BF16), preferred_element_type=F32)
    res = res * _row_scale(ssq_ref, a_ref.shape[1])
    if dil == 1:
        o_ref[...] = res.astype(o_ref.dtype)
        return
    scr_ref, = scratch
    tm, tn = res.shape
    for s in range(tn // LANES):
        scr_ref[s] = res[:, s * LANES:(s + 1) * LANES]
    for rho in range(dil):
        for s in range(tn // LANES):
            o_ref[rho, :, s * LANES:(s + 1) * LANES] = (
                scr_ref[s, pl.ds(rho, tm // dil, stride=dil), :].astype(o_ref.dtype))


def _mm_res_kernel(a_ref, b_ref, r_ref, *rest, scale, emit_next):
    if emit_next:
        g_ref, o_ref, an_ref, ssq_ref = rest

        @pl.when(pl.program_id(1) == 0)
        def _():
            ssq_ref[...] = jnp.zeros(ssq_ref.shape, F32)
    else:
        o_ref, = rest
    h = r_ref[...] + scale * jnp.dot(a_ref[...], b_ref[...].astype(BF16), preferred_element_type=F32)
    o_ref[...] = h
    if emit_next:
        an_ref[...] = (h * g_ref[...]).astype(an_ref.dtype)
        ssq_ref[...] += jnp.sum(h * h, axis=-1, keepdims=True)


def _mm_swiglu_kernel(a_ref, ssq_ref, bg_ref, bu_ref, o_ref):
    a = a_ref[...]
    r = _row_scale(ssq_ref, a_ref.shape[1])
    gate = jnp.dot(a, bg_ref[...].astype(BF16), preferred_element_type=F32) * r
    up = jnp.dot(a, bu_ref[...].astype(BF16), preferred_element_type=F32) * r
    o_ref[...] = (jax.nn.silu(gate) * up).astype(o_ref.dtype)


def _mm_tiles(m, k, n, n_weights, io_bytes):
    tm = min(m, 1024)
    if n < LANES:
        return tm, n
    for tn in (512, 256, 128):
        need = (2 * tm * k * 2 + n_weights * (2 * k * tn * 4 + k * tn * 2) + 2 * tm * tn * io_bytes
                + n_weights * tm * tn * 4)
        if n % tn == 0 and need <= MM_VMEM_BUDGET:
            return tm, tn
    raise ValueError("no matmul tile fits VMEM")


def _weight_spec(w, layer, col_off, tn):
    assert col_off % tn == 0
    return pl.BlockSpec((None, w.shape[1], tn), lambda i, j: (layer, 0, col_off // tn + j))


def matmul(act, w, layer, col_off, n, out_dtype, *, dil=1, batch=1):
    a, ssq = act
    m, k = a.shape
    tm, tn = _mm_tiles(m, k, n, 1, jnp.dtype(out_dtype).itemsize)
    in_specs = [pl.BlockSpec((tm, k), lambda i, j: (i, 0)), pl.BlockSpec((tm, LANES), lambda i, j: (i, 0)),
                _weight_spec(w, layer, col_off, tn)]
    if dil == 1:
        out_specs = pl.BlockSpec((tm, tn), lambda i, j: (i, j))
        out_shape = jax.ShapeDtypeStruct((m, n), out_dtype)
        scratch = []
    else:
        per_batch = m // batch // tm
        assert m % (batch * tm) == 0 and tm % (dil * BF16_ROWS) == 0 and tn % LANES == 0
        out_specs = pl.BlockSpec((None, dil, tm // dil, tn), lambda i, j: (i // per_batch, 0, i % per_batch, j))
        out_shape = jax.ShapeDtypeStruct((batch, dil, m // batch // dil, n), out_dtype)
        scratch = [pltpu.VMEM((tn // LANES, tm, LANES), F32)]
    return pl.pallas_call(
        functools.partial(_mm_kernel, dil=dil),
        grid=(m // tm, n // tn),
        in_specs=in_specs,
        out_specs=out_specs,
        out_shape=out_shape,
        scratch_shapes=scratch,
        compiler_params=_params(("parallel", "arbitrary")),
        name="matmul",
    )(a, ssq, w)


def matmul_residual(a, w, layer, res, scale, next_gain=None):
    m, k = a.shape
    n = w.shape[2]
    emit_next = next_gain is not None
    tm, tn = _mm_tiles(m, k, n, 1, 10 if emit_next else 8)
    tile = pl.BlockSpec((tm, tn), lambda i, j: (i, j))
    in_specs = [pl.BlockSpec((tm, k), lambda i, j: (i, 0)), _weight_spec(w, layer, 0, tn), tile]
    args = [a, w, res]
    out_specs, out_shape = tile, jax.ShapeDtypeStruct((m, n), F32)
    if emit_next:
        in_specs.append(pl.BlockSpec((1, tn), lambda i, j: (0, j)))
        args.append(next_gain.reshape(1, n))
        out_specs = [tile, tile, pl.BlockSpec((tm, LANES), lambda i, j: (i, 0))]
        out_shape = [out_shape, jax.ShapeDtypeStruct((m, n), BF16), jax.ShapeDtypeStruct((m, LANES), F32)]
    outs = pl.pallas_call(
        functools.partial(_mm_res_kernel, scale=scale, emit_next=emit_next),
        grid=(m // tm, n // tn),
        in_specs=in_specs,
        out_specs=out_specs,
        out_shape=out_shape,
        compiler_params=_params(("parallel", "arbitrary")),
        name="matmul_residual",
    )(*args)
    return (outs[0], (outs[1], outs[2])) if emit_next else (outs, None)


def matmul_swiglu(act, wg, wu, layer):
    a, ssq = act
    m, k = a.shape
    n = wg.shape[2]
    tm, tn = _mm_tiles(m, k, n, 2, 2)
    return pl.pallas_call(
        _mm_swiglu_kernel,
        grid=(m // tm, n // tn),
        in_specs=[pl.BlockSpec((tm, k), lambda i, j: (i, 0)),
                  pl.BlockSpec((tm, LANES), lambda i, j: (i, 0)),
                  _weight_spec(wg, layer, 0, tn),
                  _weight_spec(wu, layer, 0, tn)],
        out_specs=pl.BlockSpec((tm, tn), lambda i, j: (i, j)),
        out_shape=jax.ShapeDtypeStruct((m, n), BF16),
        compiler_params=_params(("parallel", "arbitrary")),
        name="matmul_swiglu",
    )(a, ssq, wg, wu)


def ffn_half_step(h, act, w_gate, w_up, w_down, layer, next_gain):
    hidden = matmul_swiglu(act, w_gate, w_up, layer)
    return matmul_residual(hidden, w_down, layer, h, 0.5, next_gain)


def _t5_bucket(dist):
    n = jnp.maximum(dist, 0)
    exact = REL_BUCKETS // 2
    nf = jnp.maximum(n, 1).astype(F32)
    large = exact + (jnp.log(nf / exact) * ((REL_BUCKETS - exact) / math.log(REL_MAX_DIST / exact))).astype(jnp.int32)
    return jnp.where(n < exact, n, jnp.minimum(large, REL_BUCKETS - 1))


def _bias_table_kernel(tab_ref, o_ref, bucket_ref, *, base0, base_step, key_stride, max_dist, n_valid_keys,
                       dist_scale):
    blk = pl.program_id(0)
    h = pl.program_id(1)

    @pl.when(h == 0)
    def _():
        n_keys, n_qry = bucket_ref.shape
        key = lax.broadcasted_iota(jnp.int32, (n_keys, n_qry), 0)
        qry = lax.broadcasted_iota(jnp.int32, (n_keys, n_qry), 1)
        dist = base0 + blk * base_step + qry - key * key_stride
        valid = (dist >= 0) & (dist <= max_dist) & (key < n_valid_keys)
        bucket_ref[...] = jnp.where(valid, _t5_bucket(dist * dist_scale), -1)

    bucket = bucket_ref[...]
    level = [tab_ref[b, h] * LOG2E for b in range(REL_BUCKETS)]
    bit = 1
    while len(level) > 1:
        odd = (bucket & bit) != 0
        level = [jnp.where(odd, level[2 * t + 1], level[2 * t]) for t in range(len(level) // 2)]
        bit *= 2
    o_ref[...] = jnp.where(bucket >= 0, level[0], NEG_INF).reshape(o_ref.shape)


def band_bias_table(rel_table, tq, span, pad, max_dist, dist_scale):
    kern = functools.partial(_bias_table_kernel, base0=pad, base_step=0, key_stride=1, max_dist=max_dist,
                             n_valid_keys=span, dist_scale=dist_scale)
    return pl.pallas_call(
        kern,
        grid=(1, N_HEADS),
        in_specs=[pl.BlockSpec(memory_space=pltpu.SMEM)],
        out_specs=pl.BlockSpec((span, tq), lambda j, h: (0, h)),
        out_shape=jax.ShapeDtypeStruct((span, N_HEADS * tq), F32),
        scratch_shapes=[pltpu.VMEM((span, tq), jnp.int32)],
        compiler_params=_params(("parallel", "arbitrary")),
        name="band_bias_table",
    )(rel_table)


def causal_bias_table(rel_table, tile, n_cls):
    kern = functools.partial(_bias_table_kernel, base0=0, base_step=tile, key_stride=1, max_dist=2 ** 30,
                             n_valid_keys=tile, dist_scale=1)
    return pl.pallas_call(
        kern,
        grid=(n_cls, N_HEADS),
        in_specs=[pl.BlockSpec(memory_space=pltpu.SMEM)],
        out_specs=pl.BlockSpec((1, tile, tile), lambda c, h: (c, 0, h)),
        out_shape=jax.ShapeDtypeStruct((n_cls, tile, N_HEADS * tile), F32),
        scratch_shapes=[pltpu.VMEM((tile, tile), jnp.int32)],
        compiler_params=_params(("parallel", "arbitrary")),
        name="causal_bias_table",
    )(rel_table)


def cmp_bias_table(rel_table, seqlen, n_cmp):
    tq = LANES
    kern = functools.partial(_bias_table_kernel, base0=-(NSA_CMP_LEN - 1), base_step=tq,
                             key_stride=NSA_CMP_STRIDE, max_dist=2 ** 30, n_valid_keys=n_cmp, dist_scale=1)
    return pl.pallas_call(
        kern,
        grid=(seqlen // tq, N_HEADS),
        in_specs=[pl.BlockSpec(memory_space=pltpu.SMEM)],
        out_specs=pl.BlockSpec((1, LANES, tq), lambda i, h: (i, 0, h)),
        out_shape=jax.ShapeDtypeStruct((seqlen // tq, LANES, N_HEADS * tq), F32),
        scratch_shapes=[pltpu.VMEM((LANES, tq), jnp.int32)],
        compiler_params=_params(("parallel", "arbitrary")),
        name="cmp_bias_table",
    )(rel_table)


def _banded_kernel(*refs, n_rep, n_grp, tq, n_prev, seq, has_sink, want_lse):
    q_ref, k_ref, v_ref, bias_ref = refs[:4]
    pos = 4
    sink_ref = None
    if has_sink:
        sink_ref = refs[pos]
        pos += 1
    o_ref = refs[pos]
    pos += 1
    lse_ref = None
    if want_lse:
        lse_ref = refs[pos]
        pos += 1
    kpad_ref, vpad_ref = refs[pos:pos + 2]

    i = pl.program_id(3)
    pad = n_prev * tq
    span = pad + tq
    rows = n_rep * tq
    qw = n_rep * HEAD_DIM

    @pl.when(i == 0)
    def _():
        if pad:
            kpad_ref[0:pad, :] = jnp.zeros((pad, n_grp * HEAD_DIM), BF16)
            vpad_ref[0:pad, :] = jnp.zeros((pad, n_grp * HEAD_DIM), BF16)
        kpad_ref[pad:pad + seq, :] = k_ref[0]
        vpad_ref[pad:pad + seq, :] = v_ref[0]

    start = pl.multiple_of(i * tq, tq)
    for gg in range(n_grp):
        kv_cols = slice(gg * HEAD_DIM, (gg + 1) * HEAD_DIM)
        row_cols = slice(gg * rows, (gg + 1) * rows)
        ks = kpad_ref[pl.ds(start, span), kv_cols]
        vs = vpad_ref[pl.ds(start, span), kv_cols]
        qs = _stack_heads(q_ref[0, :, gg * qw:(gg + 1) * qw], n_rep)
        lt = _dot_nt(ks, qs) * SCORE_SCALE +bias_ref[:, row_cols]
        if pad:
            key = lax.broadcasted_iota(jnp.int32, (span, rows), 0)
            lt = jnp.where(key >= pad - i * tq, lt, NEG_INF)
        m = jnp.max(lt, axis=0, keepdims=True)
        sink = None
        if has_sink:
            sink = sink_ref[:, row_cols] * LOG2E
            m = jnp.maximum(m, sink)
        p = jnp.exp2(lt - m)
        s = jnp.sum(p, axis=0, keepdims=True)
        if has_sink:
            s = s + jnp.exp2(sink - m)
        o_t = _dot_tn(vs, p.astype(BF16))
        s = jnp.maximum(s, TINY)
        _store_heads(o_ref, o_t / s, n_rep, tq, gg * qw)
        if want_lse:
            lse = (m + jnp.log2(s)) * LN2
            head = lax.broadcasted_iota(jnp.int32, (LANES, tq), 0)
            tile = jnp.zeros((LANES, tq), F32)
            for r in range(n_rep):
                tile = jnp.where(head == r, lse[:, r * tq:(r + 1) * tq], tile)
            lse_ref[gg, 0] = tile.T


def banded_attention(q_arr, q_off, kv_arr, k_off, v_off, bias, *, batch, seqlen, n_kv, dil,
                     max_dist, sink_row=None, want_lse=False, out_dtype=F32):
    n_rep = N_HEADS // n_kv
    sub = seqlen // dil
    tq = math.gcd(sub, BAND_BLOCK)
    n_blk = sub // tq
    n_prev = min(-(-max_dist // tq), n_blk - 1)
    span = (n_prev + 1) * tq
    qw = n_rep * HEAD_DIM
    rows = n_rep * tq
    n_grp = 2 if rows <= 512 and n_kv % 2 == 0 else 1
    gqw, gkw = n_grp * qw, n_grp * HEAD_DIM
    assert q_arr.shape[:3] == kv_arr.shape[:3] == (batch, dil, sub)
    assert q_off % gqw == 0 and k_off % gkw == 0 and v_off % gkw == 0 and bias.shape == (span, N_HEADS * tq)

    in_specs = [
        pl.BlockSpec((None, 1, tq, gqw), lambda b, rho, g, i: (b, rho, i, q_off // gqw + g)),
        pl.BlockSpec((None, 1, sub, gkw), lambda b, rho, g, i: (b, rho, 0, k_off // gkw + g)),
        pl.BlockSpec((None, 1, sub, gkw), lambda b, rho, g, i: (b, rho, 0, v_off // gkw + g)),
        pl.BlockSpec((span, n_grp * rows), lambda b, rho, g, i: (0, g)),
    ]
    args = [q_arr, kv_arr, kv_arr, bias]
    if sink_row is not None:
        in_specs.append(pl.BlockSpec((1, n_grp * rows), lambda b, rho, g, i: (0, g)))
        args.append(sink_row)
    out_specs = [pl.BlockSpec((None, 1, tq, gqw), lambda b, rho, g, i: (b, rho, i, g))]
    out_shape = [jax.ShapeDtypeStruct((batch, dil, sub, ATTN_WIDTH), out_dtype)]
    if want_lse:
        out_specs.append(pl.BlockSpec((None, n_grp, 1, tq, LANES), lambda b, rho, g, i: (b, g, rho, i, 0)))
        out_shape.append(jax.ShapeDtypeStruct((batch, n_kv, dil, sub, LANES), F32))
    kern = functools.partial(_banded_kernel, n_rep=n_rep, n_grp=n_grp, tq=tq, n_prev=n_prev, seq=sub,
                             has_sink=sink_row is not None, want_lse=want_lse)
    outs = pl.pallas_call(
        kern,
        grid=(batch, dil, n_kv // n_grp, n_blk),
        in_specs=in_specs,
        out_specs=out_specs,
        out_shape=out_shape,
        scratch_shapes=[pltpu.VMEM((n_prev * tq + sub, gkw), BF16),
                        pltpu.VMEM((n_prev * tq + sub, gkw), BF16)],
        compiler_params=_params(("parallel", "parallel", "parallel", "arbitrary")),
        name="banded_attention",
    )(*args)
    return tuple(outs) if want_lse else outs[0]


def _softmax_pv(lt, v, o_ref, n_heads, tq, col0=0):
    m = jnp.max(lt, axis=0, keepdims=True)
    p = jnp.exp2(lt - m)
    s = jnp.maximum(jnp.sum(p, axis=0, keepdims=True), TINY)
    _store_heads(o_ref, _dot_tn(v, p.astype(BF16)) / s, n_heads, tq, col0)


def _first_rank(score, n_cand):
    idx = lax.broadcasted_iota(jnp.int32, score.shape, 0)
    rank = jnp.zeros(score.shape, F32)
    for jp in range(n_cand):
        other = score[jp:jp + 1, :]
        ahead = jnp.where(other > score, 1.0, jnp.where(other == score, jnp.where(idx > jp, 1.0, 0.0), 0.0))
        rank = rank + ahead
    return rank


def _nsa_cmp_kernel(x_ref, pos_ref, w1k_ref, w2k_ref, w1v_ref, w2v_ref, ko_ref, vo_ref):
    width = 2 * NSA_KV_HEADS * HEAD_DIM
    for kv, (w1_ref, w2_ref, o_ref) in enumerate(((w1k_ref, w2k_ref, ko_ref), (w1v_ref, w2v_ref, vo_ref))):
        for g in range(NSA_KV_HEADS):
            off = kv * NSA_KV_HEADS * HEAD_DIM + g * HEAD_DIM
            chunk = jnp.concatenate(
                [x_ref[0, :, l * width + off:l * width + off + HEAD_DIM] for l in range(NSA_CMP_STRIDE)], axis=1)
            first = jnp.dot((chunk + pos_ref[0:1, :]).astype(BF16), w1_ref[0], preferred_element_type=F32)
            second = jnp.dot((chunk + pos_ref[1:2, :]).astype(BF16), w1_ref[1], preferred_element_type=F32)
            hidden = jax.nn.gelu(first + pltpu.roll(second, second.shape[0] - 1, axis=0))
            o_ref[0, g] = jnp.dot(hidden.astype(BF16), w2_ref[...], preferred_element_type=F32).astype(o_ref.dtype)


def nsa_compress(kcvc, cmp_pos, k_w1, k_w2, v_w1, v_w2, batch, seqlen):
    n_chunk = seqlen // NSA_CMP_STRIDE
    width = 2 * NSA_KV_HEADS * HEAD_DIM
    half = NSA_CMP_STRIDE * HEAD_DIM
    x = kcvc.reshape(batch, n_chunk, NSA_CMP_STRIDE * width)
    out = jax.ShapeDtypeStruct((batch, NSA_KV_HEADS, n_chunk, HEAD_DIM), BF16)
    full = lambda shape: pl.BlockSpec(shape, lambda b: (0,) * len(shape))
    return pl.pallas_call(
        _nsa_cmp_kernel,
        grid=(batch,),
        in_specs=[pl.BlockSpec((1, n_chunk, NSA_CMP_STRIDE * width), lambda b: (b, 0, 0)),
                  full((2, half)), full((2, half, HEAD_DIM)), full((HEAD_DIM, HEAD_DIM)),
                  full((2, half, HEAD_DIM)), full((HEAD_DIM, HEAD_DIM))],
        out_specs=[pl.BlockSpec((1, NSA_KV_HEADS, n_chunk, HEAD_DIM), lambda b: (b, 0, 0, 0))] * 2,
        out_shape=[out, out],
        compiler_params=_params(("parallel",)),
        name="nsa_compress",
    )(x, cmp_pos.reshape(2, half), k_w1.reshape(2, half, HEAD_DIM).astype(BF16), k_w2.astype(BF16),
      v_w1.reshape(2, half, HEAD_DIM).astype(BF16), v_w2.astype(BF16))


def _nsa_cmp_attn_kernel(q_ref, kc_ref, vc_ref, bias_ref, c2s_ref, o_ref, sel_ref, *, n_rep, tq, n_sel_blk):
    i = pl.program_id(2)
    qs = _stack_heads(q_ref[0], n_rep)
    bias = bias_ref[0]
    valid = bias > MASKED_BELOW
    lt = jnp.where(valid, _dot_nt(kc_ref[0, 0], qs) * SCORE_SCALE +bias, NEG_INF)
    m = jnp.max(lt, axis=0, keepdims=True)
    p = jnp.where(valid, jnp.exp2(lt - m), 0.0)
    s = jnp.sum(p, axis=0, keepdims=True)
    p_cmp = p / jnp.maximum(s, TINY)
    _store_heads(o_ref, _dot_tn(vc_ref[0, 0], p_cmp.astype(BF16)), n_rep, tq)

    p_sum = p_cmp[:, 0:tq]
    for r in range(1, n_rep):
        p_sum = p_sum + p_cmp[:, r * tq:(r + 1) * tq]
    imp = jnp.dot(c2s_ref[...], p_sum.astype(BF16), preferred_element_type=F32)[0:n_sel_blk]
    blk = lax.broadcasted_iota(jnp.int32, (n_sel_blk, tq), 0)
    tpos = i * tq + lax.broadcasted_iota(jnp.int32, (n_sel_blk, tq), 1)
    cur = tpos // NSA_SEL_LEN
    forced = (blk == 0) | (blk == cur) | (blk == cur - 1)
    score = jnp.where(forced, FORCED_SCORE, jnp.where(blk * NSA_SEL_LEN <= tpos, imp, NEG_INF))
    sel_ref[0, 0] = jnp.where(_first_rank(score, n_sel_blk) < min(NSA_SEL_TOPN, n_sel_blk), 1.0, 0.0)


def nsa_cmp_attention(q, kcmp, vcmp, bias, batch, seqlen):
    n_rep = N_HEADS // NSA_KV_HEADS
    tq = LANES
    qw = n_rep * HEAD_DIM
    n_sel_blk = seqlen // NSA_SEL_LEN
    n_cmp = (seqlen - NSA_CMP_LEN) // NSA_CMP_STRIDE + 1
    a, b = NSA_SEL_LEN // NSA_CMP_STRIDE, NSA_CMP_LEN // NSA_CMP_STRIDE
    w = np.zeros((LANES, LANES), np.float32)
    j = np.arange(n_sel_blk)
    for mm in range(a):
        for nn in range(b):
            ii = a * j + mm + nn - (b - 1)
            ok = (ii >= 0) & (ii < n_cmp)
            np.add.at(w, (j[ok], ii[ok]), 1.0)
    kern = functools.partial(_nsa_cmp_attn_kernel, n_rep=n_rep, tq=tq, n_sel_blk=n_sel_blk)
    return pl.pallas_call(
        kern,
        grid=(batch, NSA_KV_HEADS, seqlen // tq),
        in_specs=[pl.BlockSpec((1, tq, qw), lambda b_, g, i: (b_, i, g)),
                  pl.BlockSpec((1, 1, LANES, HEAD_DIM), lambda b_, g, i: (b_, g, 0, 0)),
                  pl.BlockSpec((1, 1, LANES, HEAD_DIM), lambda b_, g, i: (b_, g, 0, 0)),
                  pl.BlockSpec((1, LANES, n_rep * tq), lambda b_, g, i: (i, 0, g)),
                  pl.BlockSpec((LANES, LANES), lambda b_, g, i: (0, 0))],
        out_specs=[pl.BlockSpec((1, tq, qw), lambda b_, g, i: (b_, i, g)),
                   pl.BlockSpec((1, 1, n_sel_blk, tq), lambda b_, g, i: (b_, g, 0, i))],
        out_shape=[jax.ShapeDtypeStruct((batch, seqlen, ATTN_WIDTH), F32),
                   jax.ShapeDtypeStruct((batch, NSA_KV_HEADS, n_sel_blk, seqlen), F32)],
        compiler_params=_params(("parallel", "parallel", "arbitrary")),
        name="nsa_cmp_attention",
    )(q.reshape(batch, seqlen, ATTN_WIDTH), kcmp, vcmp, bias, jnp.asarray(w, BF16))


def _nsa_sel_kernel(q_ref, k_ref, v_ref, sel_ref, bias_ref, o_ref, *, n_rep, tq, n_blk, n_pass):
    i = pl.program_id(2)
    per = tq // NSA_SEL_LEN
    hpp = n_rep // n_pass
    for k in range(n_blk):
        @pl.when(i == k)
        def _(k=k):
            n_keys = (k + 1) * tq
            keys = k_ref[0, 0:n_keys, :]
            vals = v_ref[0, 0:n_keys, :]
            for part in range(n_pass):
                qs = _stack_heads(q_ref[0, :, part * hpp * HEAD_DIM:(part + 1) * hpp * HEAD_DIM], hpp)
                cols = slice(part * hpp * tq, (part + 1) * hpp * tq)
                bias = jnp.concatenate([bias_ref[k - c, :, cols] for c in range(k + 1)], axis=0)
                lt = _dot_nt(keys, qs) * SCORE_SCALE + bias
                slabs = []
                for b in range((k + 1) * per):
                    on = jnp.concatenate([sel_ref[0, 0, b:b + 1, :]] * hpp, axis=1)
                    slabs.append(jnp.where(on > 0.5, lt[b * NSA_SEL_LEN:(b + 1) * NSA_SEL_LEN], NEG_INF))
                _softmax_pv(jnp.concatenate(slabs, axis=0), vals, o_ref, hpp, tq, part * hpp * HEAD_DIM)


def nsa_selected_attention(q, kv, k_off, v_off, sel, causal_bias, batch, seqlen):
    n_rep = N_HEADS // NSA_KV_HEADS
    tq = SEL_TILE
    qw = n_rep * HEAD_DIM
    n_blk = seqlen // tq
    n_sel_blk = seqlen // NSA_SEL_LEN
    ckv = kv.shape[1]
    kv3 = kv.reshape(batch, seqlen, ckv)
    kern = functools.partial(_nsa_sel_kernel, n_rep=n_rep, tq=tq, n_blk=n_blk, n_pass=2)
    rows = n_rep * tq
    return pl.pallas_call(
        kern,
        grid=(batch, NSA_KV_HEADS, n_blk),
        in_specs=[pl.BlockSpec((1, tq, qw), lambda b, g, i: (b, i, g)),
                  pl.BlockSpec((1, seqlen, HEAD_DIM), lambda b, g, i: (b, 0, k_off // HEAD_DIM + g)),
                  pl.BlockSpec((1, seqlen, HEAD_DIM), lambda b, g, i: (b, 0, v_off // HEAD_DIM + g)),
                  pl.BlockSpec((1, 1, n_sel_blk, tq), lambda b, g, i: (b, g, 0, i)),
                  pl.BlockSpec((n_blk, tq, rows), lambda b, g, i: (0, 0, g), pipeline_mode=pl.Buffered(1))],
        out_specs=pl.BlockSpec((1, tq, qw), lambda b, g, i: (b, i, g)),
        out_shape=jax.ShapeDtypeStruct((batch, seqlen, ATTN_WIDTH), F32),
        compiler_params=_params(("parallel", "parallel", "arbitrary")),
        name="nsa_selected_attention",
    )(q.reshape(batch, seqlen, ATTN_WIDTH), kv3, kv3, sel, causal_bias)


def _nsa_gate_kernel(g_ref, oc_ref, os_ref, ow_ref, o_ref):
    gate = jax.nn.sigmoid(g_ref[...])
    for h in range(N_HEADS):
        cols = slice(h * HEAD_DIM, (h + 1) * HEAD_DIM)
        mix = (gate[:, 3 * h:3 * h + 1] * oc_ref[:, cols] + gate[:, 3 * h + 1:3 * h + 2] * os_ref[:, cols]
               + gate[:, 3 * h + 2:3 * h + 3] * ow_ref[:, cols])
        o_ref[:, cols] = mix.astype(o_ref.dtype)


def nsa_gate_combine(gates, o_cmp, o_slc, o_win):
    m = gates.shape[0]
    tm = 128
    wide = pl.BlockSpec((tm, ATTN_WIDTH), lambda i: (i, 0))
    return pl.pallas_call(
        _nsa_gate_kernel,
        grid=(m // tm,),
        in_specs=[pl.BlockSpec((tm, gates.shape[1]), lambda i: (i, 0)), wide, wide, wide],
        out_specs=wide,
        out_shape=jax.ShapeDtypeStruct((m, ATTN_WIDTH), BF16),
        compiler_params=_params(("parallel",)),
        name="nsa_gate_combine",
    )(gates, o_cmp, o_slc, o_win)


def _moba_kernel(q_ref, k_ref, v_ref, bias_ref, o_ref, kb_ref, vb_ref, km_ref, *, n_rep, tq, n_blk):
    i = pl.program_id(2)
    rows = n_rep * tq

    @pl.when(i == 0)
    def _():
        k = k_ref[0]
        kb_ref[...] = k.astype(BF16)
        vb_ref[...] = v_ref[0].astype(BF16)
        slot = lax.broadcasted_iota(jnp.int32, (BF16_ROWS, HEAD_DIM), 0)
        means = jnp.zeros((BF16_ROWS, HEAD_DIM), F32)
        for j in range(n_blk):
            means = jnp.where(slot == j, jnp.mean(k[j * tq:(j + 1) * tq], axis=0, keepdims=True), means)
        km_ref[...] = means.astype(BF16)

    qs = _stack_heads(q_ref[0], n_rep)
    gate = _dot_nt(km_ref[...], qs)[0:n_blk]
    blk = lax.broadcasted_iota(jnp.int32, (n_blk, rows), 0)
    past = blk < i
    rank = _first_rank(jnp.where(past, gate, NEG_INF), n_blk)
    chosen = jnp.where(past, jnp.where(rank < min(MOBA_TOPK, max(n_blk - 1, 1)), 1.0, 0.0), 0.0)

    for k in range(n_blk):
        @pl.when(i == k)
        def _(k=k):
            n_keys = (k + 1) * tq
            bias = jnp.concatenate([bias_ref[k - c] for c in range(k + 1)], axis=0)
            lt = _dot_nt(kb_ref[0:n_keys, :], qs) * SCORE_SCALE + bias
            parts = [jnp.where(chosen[c:c + 1, :] > 0.5, lt[c * tq:(c + 1) * tq], NEG_INF) for c in range(k)]
            parts.append(lt[k * tq:n_keys])
            _softmax_pv(jnp.concatenate(parts, axis=0), vb_ref[0:n_keys, :], o_ref, n_rep, tq)


def moba_attention(q, kv, causal_bias, batch, seqlen):
    n_rep = N_HEADS // MOBA_KV_HEADS
    tq = MOBA_BLOCK
    qw = n_rep * HEAD_DIM
    n_blk = seqlen // tq
    assert n_blk <= SUBLANES
    rows = n_rep * tq
    kv3 = kv.reshape(batch, seqlen, kv.shape[1])
    kern = functools.partial(_moba_kernel, n_rep=n_rep, tq=tq, n_blk=n_blk)
    return pl.pallas_call(
        kern,
        grid=(batch, MOBA_KV_HEADS, n_blk),
        in_specs=[pl.BlockSpec((1, tq, qw), lambda b, g, i: (b, i, g)),
                  pl.BlockSpec((1, seqlen, HEAD_DIM), lambda b, g, i: (b, 0, g)),
                  pl.BlockSpec((1, seqlen, HEAD_DIM), lambda b, g, i: (b, 0, MOBA_KV_HEADS + g)),
                  pl.BlockSpec((n_blk, tq, rows), lambda b, g, i: (0, 0, g), pipeline_mode=pl.Buffered(1))],
        out_specs=pl.BlockSpec((1, tq, qw), lambda b, g, i: (b, i, g)),
        out_shape=jax.ShapeDtypeStruct((batch, seqlen, ATTN_WIDTH), BF16),
        scratch_shapes=[pltpu.VMEM((seqlen, HEAD_DIM), BF16), pltpu.VMEM((seqlen, HEAD_DIM), BF16),
                        pltpu.VMEM((BF16_ROWS, HEAD_DIM), BF16)],
        compiler_params=_params(("parallel", "parallel", "arbitrary")),
        name="moba_attention",
    )(q.reshape(batch, seqlen, ATTN_WIDTH), kv3, kv3, causal_bias)


def _dil_combine_kernel(*refs, n_rep, dils, tile):
    n_grp = len(dils)
    o_refs, l_refs, o_ref = refs[:n_grp], refs[n_grp:2 * n_grp], refs[2 * n_grp]
    scratch = list(refs[2 * n_grp + 1:])
    outs, lses = [], []
    for o_g, l_g, dil in zip(o_refs, l_refs, dils):
        if dil == 1:
            outs.append([o_g[0, :, r * HEAD_DIM:(r + 1) * HEAD_DIM] for r in range(n_rep)])
            lses.append(l_g[0])
            continue
        nat_o, nat_l = scratch.pop(0), scratch.pop(0)
        per = tile // dil
        for rho in range(dil):
            for r in range(n_rep):
                nat_o[r, pl.ds(rho, per, stride=dil), :] = o_g[rho, :, r * HEAD_DIM:(r + 1) * HEAD_DIM]
            nat_l[pl.ds(rho, per, stride=dil), :] = l_g[rho]
        outs.append([nat_o[r] for r in range(n_rep)])
        lses.append(nat_l[...])
    top = functools.reduce(jnp.maximum, lses)
    weights = [jnp.exp(l - top) for l in lses]
    den = functools.reduce(lambda x, y: x + y, weights)
    weights = [w / den for w in weights]
    for r in range(n_rep):
        mix = weights[0][:, r:r + 1] * outs[0][r]
        for w, o in zip(weights[1:], outs[1:]):
            mix = mix + w[:, r:r + 1] * o[r]
        o_ref[0, :, r * HEAD_DIM:(r + 1) * HEAD_DIM] = mix.astype(o_ref.dtype)


def dilated_combine(outs, lses, dils, batch, seqlen):
    n_rep = N_HEADS // DIL_KV_HEADS
    tile = SEL_TILE
    qw = n_rep * HEAD_DIM
    in_specs, scratch = [], []
    for dil in dils:
        in_specs.append(pl.BlockSpec((None, dil, tile // dil, qw), lambda b, g, i: (b, 0, i, g)))
    for dil in dils:
        in_specs.append(pl.BlockSpec((None, None, dil, tile // dil, LANES), lambda b, g, i: (b, g, 0, i, 0)))
        if dil > 1:
            scratch += [pltpu.VMEM((n_rep, tile, HEAD_DIM), F32), pltpu.VMEM((tile, LANES), F32)]
    return pl.pallas_call(
        functools.partial(_dil_combine_kernel, n_rep=n_rep, dils=tuple(dils), tile=tile),
        grid=(batch, DIL_KV_HEADS, seqlen // tile),
        in_specs=in_specs,
        out_specs=pl.BlockSpec((1, tile, qw), lambda b, g, i: (b, i, g)),
        out_shape=jax.ShapeDtypeStruct((batch, seqlen, ATTN_WIDTH), BF16),
        scratch_shapes=scratch,
        compiler_params=_params(("parallel", "parallel", "parallel")),
        name="dilated_combine",
    )(*outs, *lses)


def nsa_mixer(act, h, w_in, j, cmp_pos, k_w1, k_w2, v_w1, v_w2, w_out, next_gain, tables, batch, seqlen):
    kvw = NSA_KV_HEADS * HEAD_DIM
    c0 = ATTN_WIDTH
    tokens = batch * seqlen
    q = matmul(act,w_in, j, 0, c0, BF16)
    kcvc = matmul(act,w_in, j, c0, 2 * kvw, F32)
    kvsw = matmul(act,w_in, j, c0 + 2 * kvw, 4 * kvw, BF16)
    gates = matmul(act,w_in[:, :, c0 + 6 * kvw:], j, 0, 3 * N_HEADS, F32)
    kcmp, vcmp = nsa_compress(kcvc, cmp_pos, k_w1, k_w2, v_w1, v_w2, batch, seqlen)
    o_cmp, sel = nsa_cmp_attention(q, kcmp, vcmp, tables["cmp"], batch, seqlen)
    o_slc = nsa_selected_attention(q, kvsw, 0, kvw, sel, tables["causal"], batch, seqlen)
    o_win = banded_attention(q.reshape(batch, 1, seqlen, c0), 0, kvsw.reshape(batch, 1, seqlen, 4 * kvw),
                             2 * kvw, 3 * kvw, tables["nsa_win"], batch=batch, seqlen=seqlen,
                             n_kv=NSA_KV_HEADS, dil=1, max_dist=NSA_WINDOW - 1)
    o = nsa_gate_combine(gates, o_cmp.reshape(tokens, ATTN_WIDTH), o_slc.reshape(tokens, ATTN_WIDTH),
                         o_win.reshape(tokens, ATTN_WIDTH))
    return matmul_residual(o, w_out, j, h, 1.0, next_gain)


def dilated_mixer(act, h, w_in, j, w_out, next_gain, tables, batch, seqlen):
    kvw = DIL_KV_HEADS * HEAD_DIM
    group = ATTN_WIDTH + 2 * kvw
    outs, lses, dils = [], [], []
    for gi, (window, dil) in enumerate(DIL_PAIRS):
        proj = matmul(act,w_in, j, gi * group, group, BF16, dil=dil, batch=batch)
        proj = proj.reshape(batch, dil, seqlen // dil, group)
        o, lse = banded_attention(proj, 0, proj, ATTN_WIDTH, ATTN_WIDTH + kvw, tables["dil%d" % dil],
                                  batch=batch, seqlen=seqlen, n_kv=DIL_KV_HEADS, dil=dil,
                                  max_dist=window // dil, want_lse=True)
        outs.append(o)
        lses.append(lse)
        dils.append(dil)
    o = dilated_combine(outs, lses, dils, batch, seqlen)
    return matmul_residual(o.reshape(batch * seqlen, ATTN_WIDTH), w_out, j, h, 1.0, next_gain)


def moba_mixer(act, h, w_in, j, w_out, next_gain, tables, batch, seqlen):
    q = matmul(act,w_in, j, 0, ATTN_WIDTH, BF16)
    kv = matmul(act,w_in, j, ATTN_WIDTH, 2 * MOBA_KV_HEADS * HEAD_DIM, F32)
    o = moba_attention(q, kv, tables["causal"], batch, seqlen)
    return matmul_residual(o.reshape(batch * seqlen, ATTN_WIDTH), w_out, j, h, 1.0, next_gain)


def swa_mixer(act, h, w_in, j, sinks, w_out, next_gain, tables, batch, seqlen):
    kvw = SWA_KV_HEADS * HEAD_DIM
    width = ATTN_WIDTH + 2 * kvw
    proj = matmul(act,w_in, j, 0, width, BF16).reshape(batch, 1, seqlen, width)
    sink_row = jnp.repeat(sinks, math.gcd(seqlen, BAND_BLOCK))[None, :]
    o = banded_attention(proj, 0, proj, ATTN_WIDTH, ATTN_WIDTH + kvw, tables["swa"], batch=batch,
                         seqlen=seqlen, n_kv=SWA_KV_HEADS, dil=1, max_dist=SWA_WINDOW - 1, sink_row=sink_row,
                         out_dtype=BF16)
    return matmul_residual(o.reshape(batch * seqlen, ATTN_WIDTH), w_out, j, h, 1.0, next_gain)


def _band_table_for(rel_table, seqlen, dil, max_dist):
    sub = seqlen // dil
    tq = math.gcd(sub, BAND_BLOCK)
    n_prev = min(-(-max_dist // tq), sub // tq - 1)
    return band_bias_table(rel_table, tq, (n_prev + 1) * tq, n_prev * tq, max_dist, dil)


def kernel(x, rel_table, ffn1_norm, ffn1_w_gate, ffn1_w_up, ffn1_w_down, mix_norm, ffn2_norm, ffn2_w_gate, ffn2_w_up, ffn2_w_down, final_norm, nsa_w_in, nsa_cmp_pos, nsa_cmp_k_w1, nsa_cmp_k_w2, nsa_cmp_v_w1, nsa_cmp_v_w2, nsa_w_out, dil_w_in, dil_w_out, moba_w_in, moba_w_out, swa_w_in, swa_sinks, swa_w_out):
    batch, seqlen, d_model = x.shape
    depth = ffn1_norm.shape[0]
    n_mixers = 4
    h = x.reshape(batch * seqlen, d_model)

    tables = {
        "causal": causal_bias_table(rel_table, SEL_TILE, seqlen // SEL_TILE),
        "cmp": cmp_bias_table(rel_table, seqlen, (seqlen - NSA_CMP_LEN) // NSA_CMP_STRIDE + 1),
        "nsa_win": _band_table_for(rel_table, seqlen, 1, NSA_WINDOW - 1),
        "swa": _band_table_for(rel_table, seqlen, 1, SWA_WINDOW - 1),
    }
    for window, dil in DIL_PAIRS:
        tables["dil%d" % dil] = _band_table_for(rel_table, seqlen, dil, window // dil)

    act = norm_prep(h, ffn1_norm[0])
    for i in range(depth):
        h, act = ffn_half_step(h, act, ffn1_w_gate, ffn1_w_up, ffn1_w_down, i, mix_norm[i])
        m, j = i % n_mixers, i // n_mixers
        if m == 0:
            h, act = nsa_mixer(act, h, nsa_w_in, j, nsa_cmp_pos[j], nsa_cmp_k_w1[j], nsa_cmp_k_w2[j],
                               nsa_cmp_v_w1[j], nsa_cmp_v_w2[j], nsa_w_out, ffn2_norm[i], tables, batch, seqlen)
        elif m == 1:
            h, act = dilated_mixer(act, h, dil_w_in, j, dil_w_out, ffn2_norm[i], tables, batch, seqlen)
        elif m == 2:
            h, act = moba_mixer(act, h, moba_w_in, j, moba_w_out, ffn2_norm[i], tables, batch, seqlen)
        else:
            h, act = swa_mixer(act, h, swa_w_in, j, swa_sinks[j], swa_w_out, ffn2_norm[i], tables, batch, seqlen)
        next_gain = ffn1_norm[i + 1] if i + 1 < depth else None
        h, act = ffn_half_step(h, act, ffn2_w_gate, ffn2_w_up, ffn2_w_down, i, next_gain)
    return rms_norm(h, final_norm, x.dtype).reshape(batch, seqlen, d_model)
```

```python
import functools
import math

import numpy as np
import jax
import jax.numpy as jnp
from jax import lax
from jax.experimental import pallas as pl
from jax.experimental.pallas import tpu as pltpu

HEAD_DIM = 128
N_HEADS = 32
ATTN_WIDTH = N_HEADS * HEAD_DIM
RMS_EPS = 1e-6
REL_BUCKETS = 32
REL_MAX_DIST = 2048
BAND_BLOCK = 128
NSA_KV_HEADS = 4
NSA_CMP_LEN = 32
NSA_CMP_STRIDE = 16
NSA_SEL_LEN = 64
NSA_SEL_TOPN = 16
NSA_WINDOW = 512
DIL_PAIRS = ((128, 1), (512, 4), (2048, 16))
DIL_KV_HEADS = 8
MOBA_BLOCK = 256
MOBA_TOPK = 3
MOBA_KV_HEADS = 8
SWA_WINDOW = 128
SWA_KV_HEADS = 4
ATTN_SCALE = HEAD_DIM ** -0.5
LOG2E = 1.0 / math.log(2.0)
LN2 = math.log(2.0)
SCORE_SCALE = ATTN_SCALE * LOG2E
NEG_INF = -1e30
MASKED_BELOW = -5e29
TINY = 1e-20
FORCED_SCORE = 1e9

LANES = 128
SUBLANES = 8
BF16_ROWS = 16
SEL_TILE = 256
COMBINE_TILE = 512
TABLE_HEADS_PER_STEP = 8
VMEM_LIMIT = 56 * 1024 * 1024

F32 = jnp.float32
BF16 = jnp.bfloat16


def _params(semantics):
    return pltpu.CompilerParams(dimension_semantics=semantics, vmem_limit_bytes=VMEM_LIMIT)


def _dot_nt(a, b):
    return lax.dot_general(a, b, (((1,), (1,)), ((), ())), preferred_element_type=F32)


def _dot_tn(a, b):
    return lax.dot_general(a, b, (((0,), (0,)), ((), ())), preferred_element_type=F32)


def _stack_heads(q, n_heads):
    return jnp.concatenate([q[:, r * HEAD_DIM:(r + 1) * HEAD_DIM] for r in range(n_heads)], axis=0)


def _store_heads(o_ref, o_t, n_heads, tq, col0=0):
    for r in range(n_heads):
        cols = slice(col0 + r * HEAD_DIM, col0 + (r + 1) * HEAD_DIM)
        o_ref[0, :, cols] = o_t[:, r * tq:(r + 1) * tq].T.astype(o_ref.dtype)


def _rms_kernel(x_ref, g_ref, o_ref):
    x = x_ref[...]
    y = x * lax.rsqrt(jnp.mean(x * x, axis=-1, keepdims=True) + RMS_EPS)
    o_ref[...] = (y * g_ref[...]).astype(o_ref.dtype)


def rms_norm(x, gain, out_dtype):
    m, d = x.shape
    tm = 256
    return pl.pallas_call(
        _rms_kernel,
        grid=(m // tm,),
        in_specs=[pl.BlockSpec((tm, d), lambda i: (i, 0)), pl.BlockSpec((1, d), lambda i: (0, 0))],
        out_specs=pl.BlockSpec((tm, d), lambda i: (i, 0)),
        out_shape=jax.ShapeDtypeStruct((m, d), out_dtype),
        compiler_params=_params(("parallel",)),
        name="rms_norm",
    )(x, gain.reshape(1, d))


def _norm_prep_kernel(x_ref, g_ref, a_ref, ssq_ref):
    x = x_ref[...]
    a_ref[...] = (x * g_ref[...]).astype(a_ref.dtype)
    ssq_ref[...] = jnp.broadcast_to(jnp.sum(x * x, axis=-1, keepdims=True), ssq_ref.shape)


def norm_prep(x, gain):
    m, d = x.shape
    tm = 256
    return pl.pallas_call(
        _norm_prep_kernel,
        grid=(m // tm,),
        in_specs=[pl.BlockSpec((tm, d), lambda i: (i, 0)), pl.BlockSpec((1, d), lambda i: (0, 0))],
        out_specs=[pl.BlockSpec((tm, d), lambda i: (i, 0)), pl.BlockSpec((tm, LANES), lambda i: (i, 0))],
        out_shape=[jax.ShapeDtypeStruct((m, d), BF16), jax.ShapeDtypeStruct((m, LANES), F32)],
        compiler_params=_params(("parallel",)),
        name="norm_prep",
    )(x, gain.reshape(1, d))


def _row_scale(ssq_ref, d):
    return lax.rsqrt(ssq_ref[:, 0:1] * (1.0 / d) + RMS_EPS)


MM_VMEM_BUDGET = 50 * 1024 * 1024


def _mm_kernel(a_ref, ssq_ref, b_ref, o_ref, *scratch, dil):
    res = jnp.dot(a_ref[...], b_ref[...].astype(BF16), preferred_element_type=F32)
    res = res * _row_scale(ssq_ref, a_ref.shape[1])
    if dil == 1:
        o_ref[...] = res.astype(o_ref.dtype)
        return
    scr_ref, = scratch
    tm, tn = res.shape
    for s in range(tn // LANES):
        scr_ref[s] = res[:, s * LANES:(s + 1) * LANES]
    for rho in range(dil):
        for s in range(tn // LANES):
            o_ref[rho, :, s * LANES:(s + 1) * LANES] = (
                scr_ref[s, pl.ds(rho, tm // dil, stride=dil), :].astype(o_ref.dtype))


def _mm_res_kernel(a_ref, b_ref, r_ref, *rest, scale, emit_next):
    if emit_next:
        g_ref, o_ref, an_ref, ssq_ref = rest

        @pl.when(pl.program_id(1) == 0)
        def _():
            ssq_ref[...] = jnp.zeros(ssq_ref.shape, F32)
    else:
        o_ref, = rest
    h = r_ref[...] + scale * jnp.dot(a_ref[...], b_ref[...].astype(BF16), preferred_element_type=F32)
    o_ref[...] = h
    if emit_next:
        an_ref[...] = (h * g_ref[...]).astype(an_ref.dtype)
        ssq_ref[...] += jnp.sum(h * h, axis=-1, keepdims=True)


def _mm_swiglu_kernel(a_ref, ssq_ref, bg_ref, bu_ref, o_ref):
    a = a_ref[...]
    r = _row_scale(ssq_ref, a_ref.shape[1])
    gate = jnp.dot(a, bg_ref[...].astype(BF16), preferred_element_type=F32) * r
    up = jnp.dot(a, bu_ref[...].astype(BF16), preferred_element_type=F32) * r
    o_ref[...] = (jax.nn.silu(gate) * up).astype(o_ref.dtype)


def _mm_tiles(m, k, n, n_weights, io_bytes):
    tm = min(m, 1024)
    if n < LANES:
        return tm, n
    for tn in (512, 256, 128):
        need = (2 * tm * k * 2 + n_weights * (2 * k * tn * 4 + k * tn * 2) + 2 * tm * tn * io_bytes
                + n_weights * tm * tn * 4)
        if n % tn == 0 and need <= MM_VMEM_BUDGET:
            return tm, tn
    raise ValueError("no matmul tile fits VMEM")


def _weight_spec(w, layer, col_off, tn):
    assert col_off % tn == 0
    return pl.BlockSpec((None, w.shape[1], tn), lambda i, j: (layer, 0, col_off // tn + j))


def matmul(act, w, layer, col_off, n, out_dtype, *, dil=1, batch=1):
    a, ssq = act
    m, k = a.shape
    tm, tn = _mm_tiles(m, k, n, 1, jnp.dtype(out_dtype).itemsize)
    in_specs = [pl.BlockSpec((tm, k), lambda i, j: (i, 0)), pl.BlockSpec((tm, LANES), lambda i, j: (i, 0)),
                _weight_spec(w, layer, col_off, tn)]
    if dil == 1:
        out_specs = pl.BlockSpec((tm, tn), lambda i, j: (i, j))
        out_shape = jax.ShapeDtypeStruct((m, n), out_dtype)
        scratch = []
    else:
        per_batch = m // batch // tm
        assert m % (batch * tm) == 0 and tm % (dil * BF16_ROWS) == 0 and tn % LANES == 0
        out_specs = pl.BlockSpec((None, dil, tm // dil, tn), lambda i, j: (i // per_batch, 0, i % per_batch, j))
        out_shape = jax.ShapeDtypeStruct((batch, dil, m // batch // dil, n), out_dtype)
        scratch = [pltpu.VMEM((tn // LANES, tm, LANES), F32)]
    return pl.pallas_call(
        functools.partial(_mm_kernel, dil=dil),
        grid=(m // tm, n // tn),
        in_specs=in_specs,
        out_specs=out_specs,
        out_shape=out_shape,
        scratch_shapes=scratch,
        compiler_params=_params(("parallel", "arbitrary")),
        name="matmul",
    )(a, ssq, w)


def matmul_residual(a, w, layer, res, scale, next_gain=None):
    m, k = a.shape
    n = w.shape[2]
    emit_next = next_gain is not None
    tm, tn = _mm_tiles(m, k, n, 1, 10 if emit_next else 8)
    tile = pl.BlockSpec((tm, tn), lambda i, j: (i, j))
    in_specs = [pl.BlockSpec((tm, k), lambda i, j: (i, 0)), _weight_spec(w, layer, 0, tn), tile]
    args = [a, w, res]
    out_specs, out_shape = tile, jax.ShapeDtypeStruct((m, n), F32)
    if emit_next:
        in_specs.append(pl.BlockSpec((1, tn), lambda i, j: (0, j)))
        args.append(next_gain.reshape(1, n))
        out_specs = [tile, tile, pl.BlockSpec((tm, LANES), lambda i, j: (i, 0))]
        out_shape = [out_shape, jax.ShapeDtypeStruct((m, n), BF16), jax.ShapeDtypeStruct((m, LANES), F32)]
    outs = pl.pallas_call(
        functools.partial(_mm_res_kernel, scale=scale, emit_next=emit_next),
        grid=(m // tm, n // tn),
        in_specs=in_specs,
        out_specs=out_specs,
        out_shape=out_shape,
        compiler_params=_params(("parallel", "arbitrary")),
        name="matmul_residual",
    )(*args)
    return (outs[0], (outs[1], outs[2])) if emit_next else (outs, None)


def matmul_swiglu(act, wg, wu, layer):
    a, ssq = act
    m, k = a.shape
    n = wg.shape[2]
    tm, tn = _mm_tiles(m, k, n, 2, 2)
    return pl.pallas_call(
        _mm_swiglu_kernel,
        grid=(m // tm, n // tn),
        in_specs=[pl.BlockSpec((tm, k), lambda i, j: (i, 0)),
                  pl.BlockSpec((tm, LANES), lambda i, j: (i, 0)),
                  _weight_spec(wg, layer, 0, tn),
                  _weight_spec(wu, layer, 0, tn)],
        out_specs=pl.BlockSpec((tm, tn), lambda i, j: (i, j)),
        out_shape=jax.ShapeDtypeStruct((m, n), BF16),
        compiler_params=_params(("parallel", "arbitrary")),
        name="matmul_swiglu",
    )(a, ssq, wg, wu)


def ffn_half_step(h, act, w_gate, w_up, w_down, layer, next_gain):
    hidden = matmul_swiglu(act, w_gate, w_up, layer)
    return matmul_residual(hidden, w_down, layer, h, 0.5, next_gain)


def _t5_bucket(dist):
    n = jnp.maximum(dist, 0)
    exact = REL_BUCKETS // 2
    nf = jnp.maximum(n, 1).astype(F32)
    large = exact + (jnp.log(nf / exact) * ((REL_BUCKETS - exact) / math.log(REL_MAX_DIST / exact))).astype(jnp.int32)
    return jnp.where(n < exact, n, jnp.minimum(large, REL_BUCKETS - 1))


def _bias_table_kernel(tab_ref, o_ref, bucket_ref, *, base0, base_step, key_stride, max_dist, n_valid_keys,
                       dist_scale, heads_per_step):
    blk = pl.program_id(0)
    hg = pl.program_id(1)
    n_qry = bucket_ref.shape[1]

    @pl.when(hg == 0)
    def _():
        n_keys, n_qry = bucket_ref.shape
        key = lax.broadcasted_iota(jnp.int32, (n_keys, n_qry), 0)
        qry = lax.broadcasted_iota(jnp.int32, (n_keys, n_qry), 1)
        dist = base0 + blk * base_step + qry - key * key_stride
        valid = (dist >= 0) & (dist <= max_dist) & (key < n_valid_keys)
        bucket_ref[...] = jnp.where(valid, _t5_bucket(dist * dist_scale), -1)

    bucket = bucket_ref[...]
    bits = [(bucket & (1 << t)) != 0 for t in range(REL_BUCKETS.bit_length() - 1)]
    for hh in range(heads_per_step):
        h = hg * heads_per_step + hh
        level = [tab_ref[b, h] * LOG2E for b in range(REL_BUCKETS)]
        for odd in bits:
            level = [jnp.where(odd, level[2 * t + 1], level[2 * t]) for t in range(len(level) // 2)]
        tile = jnp.where(bucket >= 0, level[0], NEG_INF)
        if len(o_ref.shape) == 2:
            o_ref[:, hh * n_qry:(hh + 1) * n_qry] = tile
        else:
            o_ref[0, :, hh * n_qry:(hh + 1) * n_qry] = tile


def band_bias_table(rel_table, tq, span, pad, max_dist, dist_scale):
    kern = functools.partial(_bias_table_kernel, base0=pad, base_step=0, key_stride=1, max_dist=max_dist,
                             n_valid_keys=span, dist_scale=dist_scale, heads_per_step=TABLE_HEADS_PER_STEP)
    return pl.pallas_call(
        kern,
        grid=(1, N_HEADS // TABLE_HEADS_PER_STEP),
        in_specs=[pl.BlockSpec(memory_space=pltpu.SMEM)],
        out_specs=pl.BlockSpec((span, TABLE_HEADS_PER_STEP * tq), lambda j, h: (0, h)),
        out_shape=jax.ShapeDtypeStruct((span, N_HEADS * tq), F32),
        scratch_shapes=[pltpu.VMEM((span, tq), jnp.int32)],
        compiler_params=_params(("parallel", "arbitrary")),
        name="band_bias_table",
    )(rel_table)


def causal_bias_table(rel_table, tile, n_cls):
    kern = functools.partial(_bias_table_kernel, base0=0, base_step=tile, key_stride=1, max_dist=2 ** 30,
                             n_valid_keys=tile, dist_scale=1, heads_per_step=1)
    return pl.pallas_call(
        kern,
        grid=(n_cls, N_HEADS),
        in_specs=[pl.BlockSpec(memory_space=pltpu.SMEM)],
        out_specs=pl.BlockSpec((1, tile, tile), lambda c, h: (c, 0, h)),
        out_shape=jax.ShapeDtypeStruct((n_cls, tile, N_HEADS * tile), F32),
        scratch_shapes=[pltpu.VMEM((tile, tile), jnp.int32)],
        compiler_params=_params(("parallel", "arbitrary")),
        name="causal_bias_table",
    )(rel_table)


def cmp_bias_table(rel_table, seqlen, n_cmp):
    tq = LANES
    kern = functools.partial(_bias_table_kernel, base0=-(NSA_CMP_LEN - 1), base_step=tq,
                             key_stride=NSA_CMP_STRIDE, max_dist=2 ** 30, n_valid_keys=n_cmp, dist_scale=1,
                             heads_per_step=TABLE_HEADS_PER_STEP)
    return pl.pallas_call(
        kern,
        grid=(seqlen // tq, N_HEADS // TABLE_HEADS_PER_STEP),
        in_specs=[pl.BlockSpec(memory_space=pltpu.SMEM)],
        out_specs=pl.BlockSpec((1, LANES, TABLE_HEADS_PER_STEP * tq), lambda i, h: (i, 0, h)),
        out_shape=jax.ShapeDtypeStruct((seqlen // tq, LANES, N_HEADS * tq), F32),
        scratch_shapes=[pltpu.VMEM((LANES, tq), jnp.int32)],
        compiler_params=_params(("parallel", "arbitrary")),
        name="cmp_bias_table",
    )(rel_table)


def _banded_kernel(*refs, n_rep, n_grp, tq, n_prev, seq, has_sink, want_lse):
    q_ref, k_ref, v_ref, bias_ref = refs[:4]
    pos = 4
    sink_ref = None
    if has_sink:
        sink_ref = refs[pos]
        pos += 1
    o_ref = refs[pos]
    pos += 1
    lse_ref = None
    if want_lse:
        lse_ref = refs[pos]
        pos += 1
    kpad_ref, vpad_ref = refs[pos:pos + 2]

    i = pl.program_id(3)
    pad = n_prev * tq
    span = pad + tq
    rows = n_rep * tq
    qw = n_rep * HEAD_DIM

    @pl.when(i == 0)
    def _():
        if pad:
            kpad_ref[0:pad, :] = jnp.zeros((pad, n_grp * HEAD_DIM), BF16)
            vpad_ref[0:pad, :] = jnp.zeros((pad, n_grp * HEAD_DIM), BF16)
        kpad_ref[pad:pad + seq, :] = k_ref[0]
        vpad_ref[pad:pad + seq, :] = v_ref[0]

    start = pl.multiple_of(i * tq, tq)

    def attend(gg, span_has_padding):
        kv_cols = slice(gg * HEAD_DIM, (gg + 1) * HEAD_DIM)
        row_cols = slice(gg * rows, (gg + 1) * rows)
        ks = kpad_ref[pl.ds(start, span), kv_cols]
        vs = vpad_ref[pl.ds(start, span), kv_cols]
        qs = _stack_heads(q_ref[0, :, gg * qw:(gg + 1) * qw], n_rep)
        lt = _dot_nt(ks, qs) * SCORE_SCALE + bias_ref[:, row_cols]
        if span_has_padding:
            key = lax.broadcasted_iota(jnp.int32, (span, rows), 0)
            lt = jnp.where(key >= pad - i * tq, lt, NEG_INF)
        m = jnp.max(lt, axis=0, keepdims=True)
        sink = None
        if has_sink:
            sink = sink_ref[:, row_cols] * LOG2E
            m = jnp.maximum(m, sink)
        p = jnp.exp2(lt - m)
        s = jnp.sum(p, axis=0, keepdims=True)
        if has_sink:
            s = s + jnp.exp2(sink - m)
        o_t = _dot_tn(vs, p.astype(BF16))
        s = jnp.maximum(s, TINY)
        _store_heads(o_ref, o_t / s, n_rep, tq, gg * qw)
        if want_lse:
            lse = (m + jnp.log2(s)) * LN2
            head = lax.broadcasted_iota(jnp.int32, (LANES, tq), 0)
            tile = jnp.zeros((LANES, tq), F32)
            for r in range(n_rep):
                tile = jnp.where(head == r, lse[:, r * tq:(r + 1) * tq], tile)
            lse_ref[gg, 0] = tile.T

    if pad:
        @pl.when(i < n_prev)
        def _():
            for gg in range(n_grp):
                attend(gg, True)

        @pl.when(i >= n_prev)
        def _():
            for gg in range(n_grp):
                attend(gg, False)
    else:
        for gg in range(n_grp):
            attend(gg, False)


def banded_attention(q_arr, q_off, kv_arr, k_off, v_off, bias, *, batch, seqlen, n_kv, dil,
                     max_dist, sink_row=None, want_lse=False, out_dtype=F32):
    n_rep = N_HEADS // n_kv
    sub = seqlen // dil
    tq = math.gcd(sub, BAND_BLOCK)
    n_blk = sub // tq
    n_prev = min(-(-max_dist // tq), n_blk - 1)
    span = (n_prev + 1) * tq
    qw = n_rep * HEAD_DIM
    rows = n_rep * tq
    n_grp = 2 if rows <= 512 and n_kv % 2 == 0 else 1
    gqw, gkw = n_grp * qw, n_grp * HEAD_DIM
    assert q_arr.shape[:3] == kv_arr.shape[:3] == (batch, dil, sub)
    assert q_off % gqw == 0 and k_off % gkw == 0 and v_off % gkw == 0 and bias.shape == (span, N_HEADS * tq)

    in_specs = [
        pl.BlockSpec((None, 1, tq, gqw), lambda b, rho, g, i: (b, rho, i, q_off // gqw + g)),
        pl.BlockSpec((None, 1, sub, gkw), lambda b, rho, g, i: (b, rho, 0, k_off // gkw + g)),
        pl.BlockSpec((None, 1, sub, gkw), lambda b, rho, g, i: (b, rho, 0, v_off // gkw + g)),
        pl.BlockSpec((span, n_grp * rows), lambda b, rho, g, i: (0, g)),
    ]
    args = [q_arr, kv_arr, kv_arr, bias]
    if sink_row is not None:
        in_specs.append(pl.BlockSpec((1, n_grp * rows), lambda b, rho, g, i: (0, g)))
        args.append(sink_row)
    out_specs = [pl.BlockSpec((None, 1, tq, gqw), lambda b, rho, g, i: (b, rho, i, g))]
    out_shape = [jax.ShapeDtypeStruct((batch, dil, sub, ATTN_WIDTH), out_dtype)]
    if want_lse:
        out_specs.append(pl.BlockSpec((None, n_grp, 1, tq, LANES), lambda b, rho, g, i: (b, g, rho, i, 0)))
        out_shape.append(jax.ShapeDtypeStruct((batch, n_kv, dil, sub, LANES), F32))
    kern = functools.partial(_banded_kernel, n_rep=n_rep, n_grp=n_grp, tq=tq, n_prev=n_prev, seq=sub,
                             has_sink=sink_row is not None, want_lse=want_lse)
    outs = pl.pallas_call(
        kern,
        grid=(batch, dil, n_kv // n_grp, n_blk),
        in_specs=in_specs,
        out_specs=out_specs,
        out_shape=out_shape,
        scratch_shapes=[pltpu.VMEM((n_prev * tq + sub, gkw), BF16),
                        pltpu.VMEM((n_prev * tq + sub, gkw), BF16)],
        compiler_params=_params(("parallel", "parallel", "parallel", "arbitrary")),
        name="banded_attention",
    )(*args)
    return tuple(outs) if want_lse else outs[0]


def _softmax_pv(lt, v, o_ref, n_heads, tq, col0=0):
    m = jnp.max(lt, axis=0, keepdims=True)
    p = jnp.exp2(lt - m)
    s = jnp.maximum(jnp.sum(p, axis=0, keepdims=True), TINY)
    _store_heads(o_ref, _dot_tn(v, p.astype(BF16)) / s, n_heads, tq, col0)


def _first_rank(score, n_cand):
    idx = lax.broadcasted_iota(jnp.int32, score.shape, 0)
    rank = jnp.zeros(score.shape, F32)
    for jp in range(n_cand):
        other = score[jp:jp + 1, :]
        ahead = jnp.where(other > score, 1.0, jnp.where(other == score, jnp.where(idx > jp, 1.0, 0.0), 0.0))
        rank = rank + ahead
    return rank


def _nsa_cmp_kernel(x_ref, pos_ref, w1k_ref, w2k_ref, w1v_ref, w2v_ref, ko_ref, vo_ref):
    width = 2 * NSA_KV_HEADS * HEAD_DIM
    for kv, (w1_ref, w2_ref, o_ref) in enumerate(((w1k_ref, w2k_ref, ko_ref), (w1v_ref, w2v_ref, vo_ref))):
        for g in range(NSA_KV_HEADS):
            off = kv * NSA_KV_HEADS * HEAD_DIM + g * HEAD_DIM
            chunk = jnp.concatenate(
                [x_ref[0, :, l * width + off:l * width + off + HEAD_DIM] for l in range(NSA_CMP_STRIDE)], axis=1)
            first = jnp.dot((chunk + pos_ref[0:1, :]).astype(BF16), w1_ref[0], preferred_element_type=F32)
            second = jnp.dot((chunk + pos_ref[1:2, :]).astype(BF16), w1_ref[1], preferred_element_type=F32)
            hidden = jax.nn.gelu(first + pltpu.roll(second, second.shape[0] - 1, axis=0))
            o_ref[0, g] = jnp.dot(hidden.astype(BF16), w2_ref[...], preferred_element_type=F32).astype(o_ref.dtype)


def nsa_compress(kcvc, cmp_pos, k_w1, k_w2, v_w1, v_w2, batch, seqlen):
    n_chunk = seqlen // NSA_CMP_STRIDE
    width = 2 * NSA_KV_HEADS * HEAD_DIM
    half = NSA_CMP_STRIDE * HEAD_DIM
    x = kcvc.reshape(batch, n_chunk, NSA_CMP_STRIDE * width)
    out = jax.ShapeDtypeStruct((batch, NSA_KV_HEADS, n_chunk, HEAD_DIM), BF16)
    full = lambda shape: pl.BlockSpec(shape, lambda b: (0,) * len(shape))
    return pl.pallas_call(
        _nsa_cmp_kernel,
        grid=(batch,),
        in_specs=[pl.BlockSpec((1, n_chunk, NSA_CMP_STRIDE * width), lambda b: (b, 0, 0)),
                  full((2, half)), full((2, half, HEAD_DIM)), full((HEAD_DIM, HEAD_DIM)),
                  full((2, half, HEAD_DIM)), full((HEAD_DIM, HEAD_DIM))],
        out_specs=[pl.BlockSpec((1, NSA_KV_HEADS, n_chunk, HEAD_DIM), lambda b: (b, 0, 0, 0))] * 2,
        out_shape=[out, out],
        compiler_params=_params(("parallel",)),
        name="nsa_compress",
    )(x, cmp_pos.reshape(2, half), k_w1.reshape(2, half, HEAD_DIM).astype(BF16), k_w2.astype(BF16),
      v_w1.reshape(2, half, HEAD_DIM).astype(BF16), v_w2.astype(BF16))


def _nsa_cmp_attn_kernel(q_ref, kc_ref, vc_ref, bias_ref, c2s_ref, o_ref, sel_ref, *, n_rep, tq, n_sel_blk):
    i = pl.program_id(2)
    qs = _stack_heads(q_ref[0], n_rep)
    bias = bias_ref[0]
    valid = bias > MASKED_BELOW
    lt = jnp.where(valid, _dot_nt(kc_ref[0, 0], qs) * SCORE_SCALE +bias, NEG_INF)
    m = jnp.max(lt, axis=0, keepdims=True)
    p = jnp.where(valid, jnp.exp2(lt - m), 0.0)
    s = jnp.sum(p, axis=0, keepdims=True)
    p_cmp = p / jnp.maximum(s, TINY)
    _store_heads(o_ref, _dot_tn(vc_ref[0, 0], p_cmp.astype(BF16)), n_rep, tq)

    p_sum = p_cmp[:, 0:tq]
    for r in range(1, n_rep):
        p_sum = p_sum + p_cmp[:, r * tq:(r + 1) * tq]
    imp = jnp.dot(c2s_ref[...], p_sum.astype(BF16), preferred_element_type=F32)[0:n_sel_blk]
    blk = lax.broadcasted_iota(jnp.int32, (n_sel_blk, tq), 0)
    tpos = i * tq + lax.broadcasted_iota(jnp.int32, (n_sel_blk, tq), 1)
    cur = tpos // NSA_SEL_LEN
    forced = (blk == 0) | (blk == cur) | (blk == cur - 1)
    score = jnp.where(forced, FORCED_SCORE, jnp.where(blk * NSA_SEL_LEN <= tpos, imp, NEG_INF))
    sel_ref[0, 0] = jnp.where(_first_rank(score, n_sel_blk) < min(NSA_SEL_TOPN, n_sel_blk), 1.0, 0.0)


def nsa_cmp_attention(q, kcmp, vcmp, bias, batch, seqlen):
    n_rep = N_HEADS // NSA_KV_HEADS
    tq = LANES
    qw = n_rep * HEAD_DIM
    n_sel_blk = seqlen // NSA_SEL_LEN
    n_cmp = (seqlen - NSA_CMP_LEN) // NSA_CMP_STRIDE + 1
    a, b = NSA_SEL_LEN // NSA_CMP_STRIDE, NSA_CMP_LEN // NSA_CMP_STRIDE
    w = np.zeros((LANES, LANES), np.float32)
    j = np.arange(n_sel_blk)
    for mm in range(a):
        for nn in range(b):
            ii = a * j + mm + nn - (b - 1)
            ok = (ii >= 0) & (ii < n_cmp)
            np.add.at(w, (j[ok], ii[ok]), 1.0)
    kern = functools.partial(_nsa_cmp_attn_kernel, n_rep=n_rep, tq=tq, n_sel_blk=n_sel_blk)
    return pl.pallas_call(
        kern,
        grid=(batch, NSA_KV_HEADS, seqlen // tq),
        in_specs=[pl.BlockSpec((1, tq, qw), lambda b_, g, i: (b_, i, g)),
                  pl.BlockSpec((1, 1, LANES, HEAD_DIM), lambda b_, g, i: (b_, g, 0, 0)),
                  pl.BlockSpec((1, 1, LANES, HEAD_DIM), lambda b_, g, i: (b_, g, 0, 0)),
                  pl.BlockSpec((1, LANES, n_rep * tq), lambda b_, g, i: (i, 0, g)),
                  pl.BlockSpec((LANES, LANES), lambda b_, g, i: (0, 0))],
        out_specs=[pl.BlockSpec((1, tq, qw), lambda b_, g, i: (b_, i, g)),
                   pl.BlockSpec((1, 1, n_sel_blk, tq), lambda b_, g, i: (b_, g, 0, i))],
        out_shape=[jax.ShapeDtypeStruct((batch, seqlen, ATTN_WIDTH), F32),
                   jax.ShapeDtypeStruct((batch, NSA_KV_HEADS, n_sel_blk, seqlen), F32)],
        compiler_params=_params(("parallel", "parallel", "arbitrary")),
        name="nsa_cmp_attention",
    )(q.reshape(batch, seqlen, ATTN_WIDTH), kcmp, vcmp, bias, jnp.asarray(w, BF16))


def _nsa_sel_kernel(q_ref, k_ref, v_ref, sel_ref, bias_ref, o_ref, *, n_rep, tq, n_blk, n_pass):
    i = pl.program_id(2)
    per = tq // NSA_SEL_LEN
    hpp = n_rep // n_pass
    for k in range(n_blk):
        @pl.when(i == k)
        def _(k=k):
            n_keys = (k + 1) * tq
            keys = k_ref[0, 0:n_keys, :]
            vals = v_ref[0, 0:n_keys, :]
            for part in range(n_pass):
                qs = _stack_heads(q_ref[0, :, part * hpp * HEAD_DIM:(part + 1) * hpp * HEAD_DIM], hpp)
                cols = slice(part * hpp * tq, (part + 1) * hpp * tq)
                bias = jnp.concatenate([bias_ref[k - c, :, cols] for c in range(k + 1)], axis=0)
                lt = _dot_nt(keys, qs) * SCORE_SCALE + bias
                slabs = []
                for b in range((k + 1) * per):
                    on = jnp.concatenate([sel_ref[0, 0, b:b + 1, :]] * hpp, axis=1)
                    slabs.append(jnp.where(on > 0.5, lt[b * NSA_SEL_LEN:(b + 1) * NSA_SEL_LEN], NEG_INF))
                _softmax_pv(jnp.concatenate(slabs, axis=0), vals, o_ref, hpp, tq, part * hpp * HEAD_DIM)


def nsa_selected_attention(q, kv, k_off, v_off, sel, causal_bias, batch, seqlen):
    n_rep = N_HEADS // NSA_KV_HEADS
    tq = SEL_TILE
    qw = n_rep * HEAD_DIM
    n_blk = seqlen // tq
    n_sel_blk = seqlen // NSA_SEL_LEN
    ckv = kv.shape[1]
    kv3 = kv.reshape(batch, seqlen, ckv)
    kern = functools.partial(_nsa_sel_kernel, n_rep=n_rep, tq=tq, n_blk=n_blk, n_pass=2)
    rows = n_rep * tq
    return pl.pallas_call(
        kern,
        grid=(batch, NSA_KV_HEADS, n_blk),
        in_specs=[pl.BlockSpec((1, tq, qw), lambda b, g, i: (b, i, g)),
                  pl.BlockSpec((1, seqlen, HEAD_DIM), lambda b, g, i: (b, 0, k_off // HEAD_DIM + g)),
                  pl.BlockSpec((1, seqlen, HEAD_DIM), lambda b, g, i: (b, 0, v_off // HEAD_DIM + g)),
                  pl.BlockSpec((1, 1, n_sel_blk, tq), lambda b, g, i: (b, g, 0, i)),
                  pl.BlockSpec((n_blk, tq, rows), lambda b, g, i: (0, 0, g), pipeline_mode=pl.Buffered(1))],
        out_specs=pl.BlockSpec((1, tq, qw), lambda b, g, i: (b, i, g)),
        out_shape=jax.ShapeDtypeStruct((batch, seqlen, ATTN_WIDTH), F32),
        compiler_params=_params(("parallel", "parallel", "arbitrary")),
        name="nsa_selected_attention",
    )(q.reshape(batch, seqlen, ATTN_WIDTH), kv3, kv3, sel, causal_bias)


def _nsa_gate_kernel(g_ref, oc_ref, os_ref, ow_ref, o_ref):
    gate = jax.nn.sigmoid(g_ref[...])
    for h in range(N_HEADS):
        cols = slice(h * HEAD_DIM, (h + 1) * HEAD_DIM)
        mix = (gate[:, 3 * h:3 * h + 1] * oc_ref[:, cols] + gate[:, 3 * h + 1:3 * h + 2] * os_ref[:, cols]
               + gate[:, 3 * h + 2:3 * h + 3] * ow_ref[:, cols])
        o_ref[:, cols] = mix.astype(o_ref.dtype)


def nsa_gate_combine(gates, o_cmp, o_slc, o_win):
    m = gates.shape[0]
    tm = 128
    wide = pl.BlockSpec((tm, ATTN_WIDTH), lambda i: (i, 0))
    return pl.pallas_call(
        _nsa_gate_kernel,
        grid=(m // tm,),
        in_specs=[pl.BlockSpec((tm, gates.shape[1]), lambda i: (i, 0)), wide, wide, wide],
        out_specs=wide,
        out_shape=jax.ShapeDtypeStruct((m, ATTN_WIDTH), BF16),
        compiler_params=_params(("parallel",)),
        name="nsa_gate_combine",
    )(gates, o_cmp, o_slc, o_win)


def _moba_kernel(q_ref, k_ref, v_ref, bias_ref, o_ref, kb_ref, vb_ref, km_ref, *, n_rep, tq, n_blk):
    i = pl.program_id(2)
    rows = n_rep * tq

    @pl.when(i == 0)
    def _():
        k = k_ref[0]
        kb_ref[...] = k.astype(BF16)
        vb_ref[...] = v_ref[0].astype(BF16)
        slot = lax.broadcasted_iota(jnp.int32, (BF16_ROWS, HEAD_DIM), 0)
        means = jnp.zeros((BF16_ROWS, HEAD_DIM), F32)
        for j in range(n_blk):
            means = jnp.where(slot == j, jnp.mean(k[j * tq:(j + 1) * tq], axis=0, keepdims=True), means)
        km_ref[...] = means.astype(BF16)

    qs = _stack_heads(q_ref[0], n_rep)
    gate = _dot_nt(km_ref[...], qs)[0:n_blk]
    blk = lax.broadcasted_iota(jnp.int32, (n_blk, rows), 0)
    past = blk < i
    rank = _first_rank(jnp.where(past, gate, NEG_INF), n_blk)
    chosen = jnp.where(past, jnp.where(rank < min(MOBA_TOPK, max(n_blk - 1, 1)), 1.0, 0.0), 0.0)

    for k in range(n_blk):
        @pl.when(i == k)
        def _(k=k):
            n_keys = (k + 1) * tq
            bias = jnp.concatenate([bias_ref[k - c] for c in range(k + 1)], axis=0)
            lt = _dot_nt(kb_ref[0:n_keys, :], qs) * SCORE_SCALE + bias
            parts = [jnp.where(chosen[c:c + 1, :] > 0.5, lt[c * tq:(c + 1) * tq], NEG_INF) for c in range(k)]
            parts.append(lt[k * tq:n_keys])
            _softmax_pv(jnp.concatenate(parts, axis=0), vb_ref[0:n_keys, :], o_ref, n_rep, tq)


def moba_attention(q, kv, causal_bias, batch, seqlen):
    n_rep = N_HEADS // MOBA_KV_HEADS
    tq = MOBA_BLOCK
    qw = n_rep * HEAD_DIM
    n_blk = seqlen // tq
    assert n_blk <= SUBLANES
    rows = n_rep * tq
    kv3 = kv.reshape(batch, seqlen, kv.shape[1])
    kern = functools.partial(_moba_kernel, n_rep=n_rep, tq=tq, n_blk=n_blk)
    return pl.pallas_call(
        kern,
        grid=(batch, MOBA_KV_HEADS, n_blk),
        in_specs=[pl.BlockSpec((1, tq, qw), lambda b, g, i: (b, i, g)),
                  pl.BlockSpec((1, seqlen, HEAD_DIM), lambda b, g, i: (b, 0, g)),
                  pl.BlockSpec((1, seqlen, HEAD_DIM), lambda b, g, i: (b, 0, MOBA_KV_HEADS + g)),
                  pl.BlockSpec((n_blk, tq, rows), lambda b, g, i: (0, 0, g), pipeline_mode=pl.Buffered(1))],
        out_specs=pl.BlockSpec((1, tq, qw), lambda b, g, i: (b, i, g)),
        out_shape=jax.ShapeDtypeStruct((batch, seqlen, ATTN_WIDTH), BF16),
        scratch_shapes=[pltpu.VMEM((seqlen, HEAD_DIM), BF16), pltpu.VMEM((seqlen, HEAD_DIM), BF16),
                        pltpu.VMEM((BF16_ROWS, HEAD_DIM), BF16)],
        compiler_params=_params(("parallel", "parallel", "arbitrary")),
        name="moba_attention",
    )(q.reshape(batch, seqlen, ATTN_WIDTH), kv3, kv3, causal_bias)


def _dil_combine_kernel(*refs, n_rep, dils, tile):
    n_grp = len(dils)
    o_refs, l_refs, o_ref = refs[:n_grp], refs[n_grp:2 * n_grp], refs[2 * n_grp]
    scratch = list(refs[2 * n_grp + 1:])
    outs, lses = [], []
    for o_g, l_g, dil in zip(o_refs, l_refs, dils):
        if dil == 1:
            outs.append([o_g[0, :, r * HEAD_DIM:(r + 1) * HEAD_DIM] for r in range(n_rep)])
            lses.append(l_g[0])
            continue
        nat_o, nat_l = scratch.pop(0), scratch.pop(0)
        per = tile // dil
        for rho in range(dil):
            for r in range(n_rep):
                nat_o[r, pl.ds(rho, per, stride=dil), :] = o_g[rho, :, r * HEAD_DIM:(r + 1) * HEAD_DIM]
            nat_l[pl.ds(rho, per, stride=dil), :] = l_g[rho]
        outs.append([nat_o[r] for r in range(n_rep)])
        lses.append(nat_l[...])
    top = functools.reduce(jnp.maximum, lses)
    weights = [jnp.exp(l - top) for l in lses]
    den = functools.reduce(lambda x, y: x + y, weights)
    weights = [w / den for w in weights]
    for r in range(n_rep):
        mix = weights[0][:, r:r + 1] * outs[0][r]
        for w, o in zip(weights[1:], outs[1:]):
            mix = mix + w[:, r:r + 1] * o[r]
        o_ref[0, :, r * HEAD_DIM:(r + 1) * HEAD_DIM] = mix.astype(o_ref.dtype)


def dilated_combine(outs, lses, dils, batch, seqlen):
    n_rep = N_HEADS // DIL_KV_HEADS
    tile = COMBINE_TILE
    qw = n_rep * HEAD_DIM
    in_specs, scratch = [], []
    for dil in dils:
        in_specs.append(pl.BlockSpec((None, dil, tile // dil, qw), lambda b, g, i: (b, 0, i, g)))
    for dil in dils:
        in_specs.append(pl.BlockSpec((None, None, dil, tile // dil, LANES), lambda b, g, i: (b, g, 0, i, 0)))
        if dil > 1:
            scratch += [pltpu.VMEM((n_rep, tile, HEAD_DIM), F32), pltpu.VMEM((tile, LANES), F32)]
    return pl.pallas_call(
        functools.partial(_dil_combine_kernel, n_rep=n_rep, dils=tuple(dils), tile=tile),
        grid=(batch, DIL_KV_HEADS, seqlen // tile),
        in_specs=in_specs,
        out_specs=pl.BlockSpec((1, tile, qw), lambda b, g, i: (b, i, g)),
        out_shape=jax.ShapeDtypeStruct((batch, seqlen, ATTN_WIDTH), BF16),
        scratch_shapes=scratch,
        compiler_params=_params(("parallel", "parallel", "parallel")),
        name="dilated_combine",
    )(*outs, *lses)


def nsa_mixer(act, h, w_in, j, cmp_pos, k_w1, k_w2, v_w1, v_w2, w_out, next_gain, tables, batch, seqlen):
    kvw = NSA_KV_HEADS * HEAD_DIM
    c0 = ATTN_WIDTH
    tokens = batch * seqlen
    q = matmul(act,w_in, j, 0, c0, BF16)
    kcvc = matmul(act,w_in, j, c0, 2 * kvw, F32)
    kvsw = matmul(act,w_in, j, c0 + 2 * kvw, 4 * kvw, BF16)
    gates = matmul(act,w_in[:, :, c0 + 6 * kvw:], j, 0, 3 * N_HEADS, F32)
    kcmp, vcmp = nsa_compress(kcvc, cmp_pos, k_w1, k_w2, v_w1, v_w2, batch, seqlen)
    o_cmp, sel = nsa_cmp_attention(q, kcmp, vcmp, tables["cmp"], batch, seqlen)
    o_slc = nsa_selected_attention(q, kvsw, 0, kvw, sel, tables["causal"], batch, seqlen)
    o_win = banded_attention(q.reshape(batch, 1, seqlen, c0), 0, kvsw.reshape(batch, 1, seqlen, 4 * kvw),
                             2 * kvw, 3 * kvw, tables["nsa_win"], batch=batch, seqlen=seqlen,
                             n_kv=NSA_KV_HEADS, dil=1, max_dist=NSA_WINDOW - 1)
    o = nsa_gate_combine(gates, o_cmp.reshape(tokens, ATTN_WIDTH), o_slc.reshape(tokens, ATTN_WIDTH),
                         o_win.reshape(tokens, ATTN_WIDTH))
    return matmul_residual(o, w_out, j, h, 1.0, next_gain)


def dilated_mixer(act, h, w_in, j, w_out, next_gain, tables, batch, seqlen):
    kvw = DIL_KV_HEADS * HEAD_DIM
    group = ATTN_WIDTH + 2 * kvw
    outs, lses, dils = [], [], []
    for gi, (window, dil) in enumerate(DIL_PAIRS):
        proj = matmul(act,w_in, j, gi * group, group, BF16, dil=dil, batch=batch)
        proj = proj.reshape(batch, dil, seqlen // dil, group)
        o, lse = banded_attention(proj, 0, proj, ATTN_WIDTH, ATTN_WIDTH + kvw, tables["dil%d" % dil],
                                  batch=batch, seqlen=seqlen, n_kv=DIL_KV_HEADS, dil=dil,
                                  max_dist=window // dil, want_lse=True)
        outs.append(o)
        lses.append(lse)
        dils.append(dil)
    o = dilated_combine(outs, lses, dils, batch, seqlen)
    return matmul_residual(o.reshape(batch * seqlen, ATTN_WIDTH), w_out, j, h, 1.0, next_gain)


def moba_mixer(act, h, w_in, j, w_out, next_gain, tables, batch, seqlen):
    q = matmul(act,w_in, j, 0, ATTN_WIDTH, BF16)
    kv = matmul(act,w_in, j, ATTN_WIDTH, 2 * MOBA_KV_HEADS * HEAD_DIM, F32)
    o = moba_attention(q, kv, tables["causal"], batch, seqlen)
    return matmul_residual(o.reshape(batch * seqlen, ATTN_WIDTH), w_out, j, h, 1.0, next_gain)


def swa_mixer(act, h, w_in, j, sinks, w_out, next_gain, tables, batch, seqlen):
    kvw = SWA_KV_HEADS * HEAD_DIM
    width = ATTN_WIDTH + 2 * kvw
    proj = matmul(act,w_in, j, 0, width, BF16).reshape(batch, 1, seqlen, width)
    sink_row = jnp.repeat(sinks, math.gcd(seqlen, BAND_BLOCK))[None, :]
    o = banded_attention(proj, 0, proj, ATTN_WIDTH, ATTN_WIDTH + kvw, tables["swa"], batch=batch,
                         seqlen=seqlen, n_kv=SWA_KV_HEADS, dil=1, max_dist=SWA_WINDOW - 1, sink_row=sink_row,
                         out_dtype=BF16)
    return matmul_residual(o.reshape(batch * seqlen, ATTN_WIDTH), w_out, j, h, 1.0, next_gain)


def _band_table_for(rel_table, seqlen, dil, max_dist):
    sub = seqlen // dil
    tq = math.gcd(sub, BAND_BLOCK)
    n_prev = min(-(-max_dist // tq), sub // tq - 1)
    return band_bias_table(rel_table, tq, (n_prev + 1) * tq, n_prev * tq, max_dist, dil)


def kernel(x, rel_table, ffn1_norm, ffn1_w_gate, ffn1_w_up, ffn1_w_down, mix_norm, ffn2_norm, ffn2_w_gate, ffn2_w_up, ffn2_w_down, final_norm, nsa_w_in, nsa_cmp_pos, nsa_cmp_k_w1, nsa_cmp_k_w2, nsa_cmp_v_w1, nsa_cmp_v_w2, nsa_w_out, dil_w_in, dil_w_out, moba_w_in, moba_w_out, swa_w_in, swa_sinks, swa_w_out):
    batch, seqlen, d_model = x.shape
    depth = ffn1_norm.shape[0]
    n_mixers = 4
    h = x.reshape(batch * seqlen, d_model)

    tables = {
        "causal": causal_bias_table(rel_table, SEL_TILE, seqlen // SEL_TILE),
        "cmp": cmp_bias_table(rel_table, seqlen, (seqlen - NSA_CMP_LEN) // NSA_CMP_STRIDE + 1),
        "nsa_win": _band_table_for(rel_table, seqlen, 1, NSA_WINDOW - 1),
        "swa": _band_table_for(rel_table, seqlen, 1, SWA_WINDOW - 1),
    }
    for window, dil in DIL_PAIRS:
        tables["dil%d" % dil] = _band_table_for(rel_table, seqlen, dil, window // dil)

    act = norm_prep(h, ffn1_norm[0])
    for i in range(depth):
        h, act = ffn_half_step(h, act, ffn1_w_gate, ffn1_w_up, ffn1_w_down, i, mix_norm[i])
        m, j = i % n_mixers, i // n_mixers
        if m == 0:
            h, act = nsa_mixer(act, h, nsa_w_in, j, nsa_cmp_pos[j], nsa_cmp_k_w1[j], nsa_cmp_k_w2[j],
                               nsa_cmp_v_w1[j], nsa_cmp_v_w2[j], nsa_w_out, ffn2_norm[i], tables, batch, seqlen)
        elif m == 1:
            h, act = dilated_mixer(act, h, dil_w_in, j, dil_w_out, ffn2_norm[i], tables, batch, seqlen)
        elif m == 2:
            h, act = moba_mixer(act, h, moba_w_in, j, moba_w_out, ffn2_norm[i], tables, batch, seqlen)
        else:
            h, act = swa_mixer(act, h, swa_w_in, j, swa_sinks[j], swa_w_out, ffn2_norm[i], tables, batch, seqlen)
        next_gain = ffn1_norm[i + 1] if i + 1 < depth else None
        h, act = ffn_half_step(h, act, ffn2_w_gate, ffn2_w_up, ffn2_w_down, i, next_gain)
    return rms_norm(h, final_norm, x.dtype).reshape(batch, seqlen, d_model)
```

```python
import functools
import math

import numpy as np
import jax
import jax.numpy as jnp
from jax import lax
from jax.experimental import pallas as pl
from jax.experimental.pallas import tpu as pltpu

HEAD_DIM = 128
N_HEADS = 32
ATTN_WIDTH = N_HEADS * HEAD_DIM
RMS_EPS = 1e-6
REL_BUCKETS = 32
REL_MAX_DIST = 2048
BAND_BLOCK = 128
NSA_KV_HEADS = 4
NSA_CMP_LEN = 32
NSA_CMP_STRIDE = 16
NSA_SEL_LEN = 64
NSA_SEL_TOPN = 16
NSA_WINDOW = 512
DIL_PAIRS = ((128, 1), (512, 4), (2048, 16))
DIL_KV_HEADS = 8
MOBA_BLOCK = 256
MOBA_TOPK = 3
MOBA_KV_HEADS = 8
SWA_WINDOW = 128
SWA_KV_HEADS = 4
ATTN_SCALE = HEAD_DIM ** -0.5
LOG2E = 1.0 / math.log(2.0)
LN2 = math.log(2.0)
SCORE_SCALE = ATTN_SCALE * LOG2E
NEG_INF = -1e30
MASKED_BELOW = -5e29
TINY = 1e-20
FORCED_SCORE = 1e9

LANES = 128
SUBLANES = 8
BF16_ROWS = 16
SEL_TILE = 256
COMBINE_TILE = 512
BANDED_ROWS_PER_STEP = 2048
TABLE_HEADS_PER_STEP = 8
VMEM_LIMIT = 56 * 1024 * 1024

F32 = jnp.float32
BF16 = jnp.bfloat16


def _params(semantics):
    return pltpu.CompilerParams(dimension_semantics=semantics, vmem_limit_bytes=VMEM_LIMIT)


def _dot_nt(a, b):
    return lax.dot_general(a, b, (((1,), (1,)), ((), ())), preferred_element_type=F32)


def _dot_tn(a, b):
    return lax.dot_general(a, b, (((0,), (0,)), ((), ())), preferred_element_type=F32)


def _stack_heads(q, n_heads):
    return jnp.concatenate([q[:, r * HEAD_DIM:(r + 1) * HEAD_DIM] for r in range(n_heads)], axis=0)


def _store_heads(o_ref, o_t, n_heads, tq, col0=0):
    for r in range(n_heads):
        cols = slice(col0 + r * HEAD_DIM, col0 + (r + 1) * HEAD_DIM)
        o_ref[0, :, cols] = o_t[:, r * tq:(r + 1) * tq].T.astype(o_ref.dtype)


def _rms_kernel(x_ref, g_ref, o_ref):
    x = x_ref[...]
    y = x * lax.rsqrt(jnp.mean(x * x, axis=-1, keepdims=True) + RMS_EPS)
    o_ref[...] = (y * g_ref[...]).astype(o_ref.dtype)


def rms_norm(x, gain, out_dtype):
    m, d = x.shape
    tm = 256
    return pl.pallas_call(
        _rms_kernel,
        grid=(m // tm,),
        in_specs=[pl.BlockSpec((tm, d), lambda i: (i, 0)), pl.BlockSpec((1, d), lambda i: (0, 0))],
        out_specs=pl.BlockSpec((tm, d), lambda i: (i, 0)),
        out_shape=jax.ShapeDtypeStruct((m, d), out_dtype),
        compiler_params=_params(("parallel",)),
        name="rms_norm",
    )(x, gain.reshape(1, d))


def _norm_prep_kernel(x_ref, g_ref, a_ref, ssq_ref):
    x = x_ref[...]
    a_ref[...] = (x * g_ref[...]).astype(a_ref.dtype)
    ssq_ref[...] = jnp.broadcast_to(jnp.sum(x * x, axis=-1, keepdims=True), ssq_ref.shape)


def norm_prep(x, gain):
    m, d = x.shape
    tm = 256
    return pl.pallas_call(
        _norm_prep_kernel,
        grid=(m // tm,),
        in_specs=[pl.BlockSpec((tm, d), lambda i: (i, 0)), pl.BlockSpec((1, d), lambda i: (0, 0))],
        out_specs=[pl.BlockSpec((tm, d), lambda i: (i, 0)), pl.BlockSpec((tm, LANES), lambda i: (i, 0))],
        out_shape=[jax.ShapeDtypeStruct((m, d), BF16), jax.ShapeDtypeStruct((m, LANES), F32)],
        compiler_params=_params(("parallel",)),
        name="norm_prep",
    )(x, gain.reshape(1, d))


def _row_scale(ssq_ref, d):
    return lax.rsqrt(ssq_ref[:, 0:1] * (1.0 / d) + RMS_EPS)


MM_VMEM_BUDGET = 50 * 1024 * 1024


def _mm_kernel(a_ref, ssq_ref, b_ref, o_ref, *scratch, dil):
    res = jnp.dot(a_ref[...], b_ref[...].astype(BF16), preferred_element_type=F32)
    res = res * _row_scale(ssq_ref, a_ref.shape[1])
    if dil == 1:
        o_ref[...] = res.astype(o_ref.dtype)
        return
    scr_ref, = scratch
    tm, tn = res.shape
    for s in range(tn // LANES):
        scr_ref[s] = res[:, s * LANES:(s + 1) * LANES]
    for rho in range(dil):
        for s in range(tn // LANES):
            o_ref[rho, :, s * LANES:(s + 1) * LANES] = (
                scr_ref[s, pl.ds(rho, tm // dil, stride=dil), :].astype(o_ref.dtype))


def _mm_res_kernel(a_ref, b_ref, r_ref, *rest, scale, emit_next):
    if emit_next:
        g_ref, o_ref, an_ref, ssq_ref = rest

        @pl.when(pl.program_id(1) == 0)
        def _():
            ssq_ref[...] = jnp.zeros(ssq_ref.shape, F32)
    else:
        o_ref, = rest
    h = r_ref[...] + scale * jnp.dot(a_ref[...], b_ref[...].astype(BF16), preferred_element_type=F32)
    o_ref[...] = h
    if emit_next:
        an_ref[...] = (h * g_ref[...]).astype(an_ref.dtype)
        ssq_ref[...] += jnp.sum(h * h, axis=-1, keepdims=True)


def _mm_swiglu_kernel(a_ref, ssq_ref, bg_ref, bu_ref, o_ref):
    a = a_ref[...]
    r = _row_scale(ssq_ref, a_ref.shape[1])
    gate = jnp.dot(a, bg_ref[...].astype(BF16), preferred_element_type=F32) * r
    up = jnp.dot(a, bu_ref[...].astype(BF16), preferred_element_type=F32) * r
    o_ref[...] = (jax.nn.silu(gate) * up).astype(o_ref.dtype)


def _mm_tiles(m, k, n, n_weights, io_bytes):
    tm = min(m, 1024)
    if n < LANES:
        return tm, n
    for tn in (512, 256, 128):
        need = (2 * tm * k * 2 + n_weights * (2 * k * tn * 4 + k * tn * 2) + 2 * tm * tn * io_bytes
                + n_weights * tm * tn * 4)
        if n % tn == 0 and need <= MM_VMEM_BUDGET:
            return tm, tn
    raise ValueError("no matmul tile fits VMEM")


def _weight_spec(w, layer, col_off, tn):
    assert col_off % tn == 0
    return pl.BlockSpec((None, w.shape[1], tn), lambda i, j: (layer, 0, col_off // tn + j))


def matmul(act, w, layer, col_off, n, out_dtype, *, dil=1, batch=1):
    a, ssq = act
    m, k = a.shape
    tm, tn = _mm_tiles(m, k, n, 1, jnp.dtype(out_dtype).itemsize)
    in_specs = [pl.BlockSpec((tm, k), lambda i, j: (i, 0)), pl.BlockSpec((tm, LANES), lambda i, j: (i, 0)),
                _weight_spec(w, layer, col_off, tn)]
    if dil == 1:
        out_specs = pl.BlockSpec((tm, tn), lambda i, j: (i, j))
        out_shape = jax.ShapeDtypeStruct((m, n), out_dtype)
        scratch = []
    else:
        per_batch = m // batch // tm
        assert m % (batch * tm) == 0 and tm % (dil * BF16_ROWS) == 0 and tn % LANES == 0
        out_specs = pl.BlockSpec((None, dil, tm // dil, tn), lambda i, j: (i // per_batch, 0, i % per_batch, j))
        out_shape = jax.ShapeDtypeStruct((batch, dil, m // batch // dil, n), out_dtype)
        scratch = [pltpu.VMEM((tn // LANES, tm, LANES), F32)]
    return pl.pallas_call(
        functools.partial(_mm_kernel, dil=dil),
        grid=(m // tm, n // tn),
        in_specs=in_specs,
        out_specs=out_specs,
        out_shape=out_shape,
        scratch_shapes=scratch,
        compiler_params=_params(("parallel", "arbitrary")),
        name="matmul",
    )(a, ssq, w)


def matmul_residual(a, w, layer, res, scale, next_gain=None):
    m, k = a.shape
    n = w.shape[2]
    emit_next = next_gain is not None
    tm, tn = _mm_tiles(m, k, n, 1, 10 if emit_next else 8)
    tile = pl.BlockSpec((tm, tn), lambda i, j: (i, j))
    in_specs = [pl.BlockSpec((tm, k), lambda i, j: (i, 0)), _weight_spec(w, layer, 0, tn), tile]
    args = [a, w, res]
    out_specs, out_shape = tile, jax.ShapeDtypeStruct((m, n), F32)
    if emit_next:
        in_specs.append(pl.BlockSpec((1, tn), lambda i, j: (0, j)))
        args.append(next_gain.reshape(1, n))
        out_specs = [tile, tile, pl.BlockSpec((tm, LANES), lambda i, j: (i, 0))]
        out_shape = [out_shape, jax.ShapeDtypeStruct((m, n), BF16), jax.ShapeDtypeStruct((m, LANES), F32)]
    outs = pl.pallas_call(
        functools.partial(_mm_res_kernel, scale=scale, emit_next=emit_next),
        grid=(m // tm, n // tn),
        in_specs=in_specs,
        out_specs=out_specs,
        out_shape=out_shape,
        compiler_params=_params(("parallel", "arbitrary")),
        name="matmul_residual",
    )(*args)
    return (outs[0], (outs[1], outs[2])) if emit_next else (outs, None)


def matmul_swiglu(act, wg, wu, layer):
    a, ssq = act
    m, k = a.shape
    n = wg.shape[2]
    tm, tn = _mm_tiles(m, k, n, 2, 2)
    return pl.pallas_call(
        _mm_swiglu_kernel,
        grid=(m // tm, n // tn),
        in_specs=[pl.BlockSpec((tm, k), lambda i, j: (i, 0)),
                  pl.BlockSpec((tm, LANES), lambda i, j: (i, 0)),
                  _weight_spec(wg, layer, 0, tn),
                  _weight_spec(wu, layer, 0, tn)],
        out_specs=pl.BlockSpec((tm, tn), lambda i, j: (i, j)),
        out_shape=jax.ShapeDtypeStruct((m, n), BF16),
        compiler_params=_params(("parallel", "arbitrary")),
        name="matmul_swiglu",
    )(a, ssq, wg, wu)


def ffn_half_step(h, act, w_gate, w_up, w_down, layer, next_gain):
    hidden = matmul_swiglu(act, w_gate, w_up, layer)
    return matmul_residual(hidden, w_down, layer, h, 0.5, next_gain)


def _t5_bucket(dist):
    n = jnp.maximum(dist, 0)
    exact = REL_BUCKETS // 2
    nf = jnp.maximum(n, 1).astype(F32)
    large = exact + (jnp.log(nf / exact) * ((REL_BUCKETS - exact) / math.log(REL_MAX_DIST / exact))).astype(jnp.int32)
    return jnp.where(n < exact, n, jnp.minimum(large, REL_BUCKETS - 1))


def _bias_table_kernel(tab_ref, o_ref, bucket_ref, *, base0, base_step, key_stride, max_dist, n_valid_keys,
                       dist_scale, heads_per_step):
    blk = pl.program_id(0)
    hg = pl.program_id(1)
    n_qry = bucket_ref.shape[1]

    @pl.when(hg == 0)
    def _():
        n_keys, n_qry = bucket_ref.shape
        key = lax.broadcasted_iota(jnp.int32, (n_keys, n_qry), 0)
        qry = lax.broadcasted_iota(jnp.int32, (n_keys, n_qry), 1)
        dist = base0 + blk * base_step + qry - key * key_stride
        valid = (dist >= 0) & (dist <= max_dist) & (key < n_valid_keys)
        bucket_ref[...] = jnp.where(valid, _t5_bucket(dist * dist_scale), -1)

    bucket = bucket_ref[...]
    bits = [(bucket & (1 << t)) != 0 for t in range(REL_BUCKETS.bit_length() - 1)]
    for hh in range(heads_per_step):
        h = hg * heads_per_step + hh
        level = [tab_ref[b, h] * LOG2E for b in range(REL_BUCKETS)]
        for odd in bits:
            level = [jnp.where(odd, level[2 * t + 1], level[2 * t]) for t in range(len(level) // 2)]
        tile = jnp.where(bucket >= 0, level[0], NEG_INF)
        if len(o_ref.shape) == 2:
            o_ref[:, hh * n_qry:(hh + 1) * n_qry] = tile
        else:
            o_ref[0, :, hh * n_qry:(hh + 1) * n_qry] = tile


def band_bias_table(rel_table, tq, span, pad, max_dist, dist_scale):
    kern = functools.partial(_bias_table_kernel, base0=pad, base_step=0, key_stride=1, max_dist=max_dist,
                             n_valid_keys=span, dist_scale=dist_scale, heads_per_step=TABLE_HEADS_PER_STEP)
    return pl.pallas_call(
        kern,
        grid=(1, N_HEADS // TABLE_HEADS_PER_STEP),
        in_specs=[pl.BlockSpec(memory_space=pltpu.SMEM)],
        out_specs=pl.BlockSpec((span, TABLE_HEADS_PER_STEP * tq), lambda j, h: (0, h)),
        out_shape=jax.ShapeDtypeStruct((span, N_HEADS * tq), F32),
        scratch_shapes=[pltpu.VMEM((span, tq), jnp.int32)],
        compiler_params=_params(("parallel", "arbitrary")),
        name="band_bias_table",
    )(rel_table)


def causal_bias_table(rel_table, tile, n_cls):
    kern = functools.partial(_bias_table_kernel, base0=0, base_step=tile, key_stride=1, max_dist=2 ** 30,
                             n_valid_keys=tile, dist_scale=1, heads_per_step=1)
    return pl.pallas_call(
        kern,
        grid=(n_cls, N_HEADS),
        in_specs=[pl.BlockSpec(memory_space=pltpu.SMEM)],
        out_specs=pl.BlockSpec((1, tile, tile), lambda c, h: (c, 0, h)),
        out_shape=jax.ShapeDtypeStruct((n_cls, tile, N_HEADS * tile), F32),
        scratch_shapes=[pltpu.VMEM((tile, tile), jnp.int32)],
        compiler_params=_params(("parallel", "arbitrary")),
        name="causal_bias_table",
    )(rel_table)


def cmp_bias_table(rel_table, seqlen, n_cmp):
    tq = SEL_TILE
    kern = functools.partial(_bias_table_kernel, base0=-(NSA_CMP_LEN - 1), base_step=tq,
                             key_stride=NSA_CMP_STRIDE, max_dist=2 ** 30, n_valid_keys=n_cmp, dist_scale=1,
                             heads_per_step=TABLE_HEADS_PER_STEP)
    return pl.pallas_call(
        kern,
        grid=(seqlen // tq, N_HEADS // TABLE_HEADS_PER_STEP),
        in_specs=[pl.BlockSpec(memory_space=pltpu.SMEM)],
        out_specs=pl.BlockSpec((1, LANES, TABLE_HEADS_PER_STEP * tq), lambda i, h: (i, 0, h)),
        out_shape=jax.ShapeDtypeStruct((seqlen // tq, LANES, N_HEADS * tq), F32),
        scratch_shapes=[pltpu.VMEM((LANES, tq), jnp.int32)],
        compiler_params=_params(("parallel", "arbitrary")),
        name="cmp_bias_table",
    )(rel_table)


def _banded_kernel(*refs, n_rep, n_grp, tq, n_prev, seq, has_sink, want_lse):
    q_ref, k_ref, v_ref, bias_ref = refs[:4]
    pos = 4
    sink_ref = None
    if has_sink:
        sink_ref = refs[pos]
        pos += 1
    o_ref = refs[pos]
    pos += 1
    lse_ref = None
    if want_lse:
        lse_ref = refs[pos]
        pos += 1
    kpad_ref, vpad_ref = refs[pos:pos + 2]

    i = pl.program_id(3)
    pad = n_prev * tq
    span = pad + tq
    rows = n_rep * tq
    qw = n_rep * HEAD_DIM

    @pl.when(i == 0)
    def _():
        if pad:
            kpad_ref[0:pad, :] = jnp.zeros((pad, n_grp * HEAD_DIM), BF16)
            vpad_ref[0:pad, :] = jnp.zeros((pad, n_grp * HEAD_DIM), BF16)
        kpad_ref[pad:pad + seq, :] = k_ref[0]
        vpad_ref[pad:pad + seq, :] = v_ref[0]

    start = pl.multiple_of(i * tq, tq)

    def attend(gg, span_has_padding):
        kv_cols = slice(gg * HEAD_DIM, (gg + 1) * HEAD_DIM)
        row_cols = slice(gg * rows, (gg + 1) * rows)
        ks = kpad_ref[pl.ds(start, span), kv_cols]
        vs = vpad_ref[pl.ds(start, span), kv_cols]
        qs = _stack_heads(q_ref[0, :, gg * qw:(gg + 1) * qw], n_rep)
        lt = _dot_nt(ks, qs) * SCORE_SCALE + bias_ref[:, row_cols]
        if span_has_padding:
            key = lax.broadcasted_iota(jnp.int32, (span, rows), 0)
            lt = jnp.where(key >= pad - i * tq, lt, NEG_INF)
        m = jnp.max(lt, axis=0, keepdims=True)
        sink = None
        if has_sink:
            sink = sink_ref[:, row_cols] * LOG2E
            m = jnp.maximum(m, sink)
        p = jnp.exp2(lt - m)
        s = jnp.sum(p, axis=0, keepdims=True)
        if has_sink:
            s = s + jnp.exp2(sink - m)
        o_t = _dot_tn(vs, p.astype(BF16))
        s = jnp.maximum(s, TINY)
        _store_heads(o_ref, o_t / s, n_rep, tq, gg * qw)
        if want_lse:
            lse = (m + jnp.log2(s)) * LN2
            head = lax.broadcasted_iota(jnp.int32, (LANES, tq), 0)
            tile = jnp.zeros((LANES, tq), F32)
            for r in range(n_rep):
                tile = jnp.where(head == r, lse[:, r * tq:(r + 1) * tq], tile)
            lse_ref[gg, 0] = tile.T

    if pad:
        @pl.when(i < n_prev)
        def _():
            for gg in range(n_grp):
                attend(gg, True)

        @pl.when(i >= n_prev)
        def _():
            for gg in range(n_grp):
                attend(gg, False)
    else:
        for gg in range(n_grp):
            attend(gg, False)


def banded_attention(q_arr, q_off, kv_arr, k_off, v_off, bias, *, batch, seqlen, n_kv, dil,
                     max_dist, sink_row=None, want_lse=False, out_dtype=F32):
    n_rep = N_HEADS // n_kv
    sub = seqlen // dil
    tq = math.gcd(sub, BAND_BLOCK)
    n_blk = sub // tq
    n_prev = min(-(-max_dist // tq), n_blk - 1)
    span = (n_prev + 1) * tq
    qw = n_rep * HEAD_DIM
    rows = n_rep * tq
    n_grp = max(c for c in (4, 2, 1) if c * rows <= BANDED_ROWS_PER_STEP and n_kv % c == 0)
    gqw, gkw = n_grp * qw, n_grp * HEAD_DIM
    assert q_arr.shape[:3] == kv_arr.shape[:3] == (batch, dil, sub)
    assert q_off % gqw == 0 and k_off % gkw == 0 and v_off % gkw == 0 and bias.shape == (span, N_HEADS * tq)

    in_specs = [
        pl.BlockSpec((None, 1, tq, gqw), lambda b, rho, g, i: (b, rho, i, q_off // gqw + g)),
        pl.BlockSpec((None, 1, sub, gkw), lambda b, rho, g, i: (b, rho, 0, k_off // gkw + g)),
        pl.BlockSpec((None, 1, sub, gkw), lambda b, rho, g, i: (b, rho, 0, v_off // gkw + g)),
        pl.BlockSpec((span, n_grp * rows), lambda b, rho, g, i: (0, g)),
    ]
    args = [q_arr, kv_arr, kv_arr, bias]
    if sink_row is not None:
        in_specs.append(pl.BlockSpec((1, n_grp * rows), lambda b, rho, g, i: (0, g)))
        args.append(sink_row)
    out_specs = [pl.BlockSpec((None, 1, tq, gqw), lambda b, rho, g, i: (b, rho, i, g))]
    out_shape = [jax.ShapeDtypeStruct((batch, dil, sub, ATTN_WIDTH), out_dtype)]
    if want_lse:
        out_specs.append(pl.BlockSpec((None, n_grp, 1, tq, LANES), lambda b, rho, g, i: (b, g, rho, i, 0)))
        out_shape.append(jax.ShapeDtypeStruct((batch, n_kv, dil, sub, LANES), F32))
    kern = functools.partial(_banded_kernel, n_rep=n_rep, n_grp=n_grp, tq=tq, n_prev=n_prev, seq=sub,
                             has_sink=sink_row is not None, want_lse=want_lse)
    outs = pl.pallas_call(
        kern,
        grid=(batch, dil, n_kv // n_grp, n_blk),
        in_specs=in_specs,
        out_specs=out_specs,
        out_shape=out_shape,
        scratch_shapes=[pltpu.VMEM((n_prev * tq + sub, gkw), BF16),
                        pltpu.VMEM((n_prev * tq + sub, gkw), BF16)],
        compiler_params=_params(("parallel", "parallel", "parallel", "arbitrary")),
        name="banded_attention",
    )(*args)
    return tuple(outs) if want_lse else outs[0]


def _softmax_pv(lt, v, o_ref, n_heads, tq, col0=0):
    m = jnp.max(lt, axis=0, keepdims=True)
    p = jnp.exp2(lt - m)
    s = jnp.maximum(jnp.sum(p, axis=0, keepdims=True), TINY)
    _store_heads(o_ref, _dot_tn(v, p.astype(BF16)) / s, n_heads, tq, col0)


def _first_rank(score, n_cand):
    idx = lax.broadcasted_iota(jnp.int32, score.shape, 0)
    rank = jnp.zeros(score.shape, F32)
    for jp in range(n_cand):
        other = score[jp:jp + 1, :]
        ahead = jnp.where(other > score, 1.0, jnp.where(other == score, jnp.where(idx > jp, 1.0, 0.0), 0.0))
        rank = rank + ahead
    return rank


def _nsa_cmp_kernel(x_ref, pos_ref, w1k_ref, w2k_ref, w1v_ref, w2v_ref, ko_ref, vo_ref):
    width = 2 * NSA_KV_HEADS * HEAD_DIM
    for kv, (w1_ref, w2_ref, o_ref) in enumerate(((w1k_ref, w2k_ref, ko_ref), (w1v_ref, w2v_ref, vo_ref))):
        for g in range(NSA_KV_HEADS):
            off = kv * NSA_KV_HEADS * HEAD_DIM + g * HEAD_DIM
            chunk = jnp.concatenate(
                [x_ref[0, :, l * width + off:l * width + off + HEAD_DIM] for l in range(NSA_CMP_STRIDE)], axis=1)
            first = jnp.dot((chunk + pos_ref[0:1, :]).astype(BF16), w1_ref[0], preferred_element_type=F32)
            second = jnp.dot((chunk + pos_ref[1:2, :]).astype(BF16), w1_ref[1], preferred_element_type=F32)
            hidden = jax.nn.gelu(first + pltpu.roll(second, second.shape[0] - 1, axis=0))
            o_ref[0, g] = jnp.dot(hidden.astype(BF16), w2_ref[...], preferred_element_type=F32).astype(o_ref.dtype)


def nsa_compress(kcvc, cmp_pos, k_w1, k_w2, v_w1, v_w2, batch, seqlen):
    n_chunk = seqlen // NSA_CMP_STRIDE
    width = 2 * NSA_KV_HEADS * HEAD_DIM
    half = NSA_CMP_STRIDE * HEAD_DIM
    x = kcvc.reshape(batch, n_chunk, NSA_CMP_STRIDE * width)
    out = jax.ShapeDtypeStruct((batch, NSA_KV_HEADS, n_chunk, HEAD_DIM), BF16)
    full = lambda shape: pl.BlockSpec(shape, lambda b: (0,) * len(shape))
    return pl.pallas_call(
        _nsa_cmp_kernel,
        grid=(batch,),
        in_specs=[pl.BlockSpec((1, n_chunk, NSA_CMP_STRIDE * width), lambda b: (b, 0, 0)),
                  full((2, half)), full((2, half, HEAD_DIM)), full((HEAD_DIM, HEAD_DIM)),
                  full((2, half, HEAD_DIM)), full((HEAD_DIM, HEAD_DIM))],
        out_specs=[pl.BlockSpec((1, NSA_KV_HEADS, n_chunk, HEAD_DIM), lambda b: (b, 0, 0, 0))] * 2,
        out_shape=[out, out],
        compiler_params=_params(("parallel",)),
        name="nsa_compress",
    )(x, cmp_pos.reshape(2, half), k_w1.reshape(2, half, HEAD_DIM).astype(BF16), k_w2.astype(BF16),
      v_w1.reshape(2, half, HEAD_DIM).astype(BF16), v_w2.astype(BF16))


def _nsa_cmp_attn_kernel(q_ref, kc_ref, vc_ref, bias_ref, c2s_ref, o_ref, sel_ref, *, n_rep, tq, n_sel_blk):
    i = pl.program_id(2)
    qs = _stack_heads(q_ref[0], n_rep)
    bias = bias_ref[0]
    valid = bias > MASKED_BELOW
    lt = jnp.where(valid, _dot_nt(kc_ref[0, 0], qs) * SCORE_SCALE +bias, NEG_INF)
    m = jnp.max(lt, axis=0, keepdims=True)
    p = jnp.where(valid, jnp.exp2(lt - m), 0.0)
    s = jnp.sum(p, axis=0, keepdims=True)
    p_cmp = p / jnp.maximum(s, TINY)
    _store_heads(o_ref, _dot_tn(vc_ref[0, 0], p_cmp.astype(BF16)), n_rep, tq)

    p_sum = p_cmp[:, 0:tq]
    for r in range(1, n_rep):
        p_sum = p_sum + p_cmp[:, r * tq:(r + 1) * tq]
    imp = jnp.dot(c2s_ref[...], p_sum.astype(BF16), preferred_element_type=F32)[0:n_sel_blk]
    blk = lax.broadcasted_iota(jnp.int32, (n_sel_blk, tq), 0)
    tpos = i * tq + lax.broadcasted_iota(jnp.int32, (n_sel_blk, tq), 1)
    cur = tpos // NSA_SEL_LEN
    forced = (blk == 0) | (blk == cur) | (blk == cur - 1)
    score = jnp.where(forced, FORCED_SCORE, jnp.where(blk * NSA_SEL_LEN <= tpos, imp, NEG_INF))
    sel_ref[0, 0] = jnp.where(_first_rank(score, n_sel_blk) < min(NSA_SEL_TOPN, n_sel_blk), 1.0, 0.0)


def nsa_cmp_attention(q, kcmp, vcmp, bias, batch, seqlen):
    n_rep = N_HEADS // NSA_KV_HEADS
    tq = SEL_TILE
    qw = n_rep * HEAD_DIM
    n_sel_blk = seqlen // NSA_SEL_LEN
    n_cmp = (seqlen - NSA_CMP_LEN) // NSA_CMP_STRIDE + 1
    a, b = NSA_SEL_LEN // NSA_CMP_STRIDE, NSA_CMP_LEN // NSA_CMP_STRIDE
    w = np.zeros((LANES, LANES), np.float32)
    j = np.arange(n_sel_blk)
    for mm in range(a):
        for nn in range(b):
            ii = a * j + mm + nn - (b - 1)
            ok = (ii >= 0) & (ii < n_cmp)
            np.add.at(w, (j[ok], ii[ok]), 1.0)
    kern = functools.partial(_nsa_cmp_attn_kernel, n_rep=n_rep, tq=tq, n_sel_blk=n_sel_blk)
    return pl.pallas_call(
        kern,
        grid=(batch, NSA_KV_HEADS, seqlen // tq),
        in_specs=[pl.BlockSpec((1, tq, qw), lambda b_, g, i: (b_, i, g)),
                  pl.BlockSpec((1, 1, LANES, HEAD_DIM), lambda b_, g, i: (b_, g, 0, 0)),
                  pl.BlockSpec((1, 1, LANES, HEAD_DIM), lambda b_, g, i: (b_, g, 0, 0)),
                  pl.BlockSpec((1, LANES, n_rep * tq), lambda b_, g, i: (i, 0, g)),
                  pl.BlockSpec((LANES, LANES), lambda b_, g, i: (0, 0))],
        out_specs=[pl.BlockSpec((1, tq, qw), lambda b_, g, i: (b_, i, g)),
                   pl.BlockSpec((1, 1, n_sel_blk, tq), lambda b_, g, i: (b_, g, 0, i))],
        out_shape=[jax.ShapeDtypeStruct((batch, seqlen, ATTN_WIDTH), F32),
                   jax.ShapeDtypeStruct((batch, NSA_KV_HEADS, n_sel_blk, seqlen), F32)],
        compiler_params=_params(("parallel", "parallel", "arbitrary")),
        name="nsa_cmp_attention",
    )(q.reshape(batch, seqlen, ATTN_WIDTH), kcmp, vcmp, bias, jnp.asarray(w, BF16))


def _nsa_sel_kernel(q_ref, k_ref, v_ref, sel_ref, bias_ref, o_ref, *, n_rep, tq, n_blk, n_pass):
    i = pl.program_id(2)
    per = tq // NSA_SEL_LEN
    hpp = n_rep // n_pass
    for k in range(n_blk):
        @pl.when(i == k)
        def _(k=k):
            n_keys = (k + 1) * tq
            keys = k_ref[0, 0:n_keys, :]
            vals = v_ref[0, 0:n_keys, :]
            for part in range(n_pass):
                qs = _stack_heads(q_ref[0, :, part * hpp * HEAD_DIM:(part + 1) * hpp * HEAD_DIM], hpp)
                cols = slice(part * hpp * tq, (part + 1) * hpp * tq)
                bias = jnp.concatenate([bias_ref[k - c, :, cols] for c in range(k + 1)], axis=0)
                lt = _dot_nt(keys, qs) * SCORE_SCALE + bias
                slabs = []
                for b in range((k + 1) * per):
                    on = jnp.concatenate([sel_ref[0, 0, b:b + 1, :]] * hpp, axis=1)
                    slabs.append(jnp.where(on > 0.5, lt[b * NSA_SEL_LEN:(b + 1) * NSA_SEL_LEN], NEG_INF))
                _softmax_pv(jnp.concatenate(slabs, axis=0), vals, o_ref, hpp, tq, part * hpp * HEAD_DIM)


def nsa_selected_attention(q, kv, k_off, v_off, sel, causal_bias, batch, seqlen):
    n_rep = N_HEADS // NSA_KV_HEADS
    tq = SEL_TILE
    qw = n_rep * HEAD_DIM
    n_blk = seqlen // tq
    n_sel_blk = seqlen // NSA_SEL_LEN
    ckv = kv.shape[1]
    kv3 = kv.reshape(batch, seqlen, ckv)
    kern = functools.partial(_nsa_sel_kernel, n_rep=n_rep, tq=tq, n_blk=n_blk, n_pass=2)
    rows = n_rep * tq
    return pl.pallas_call(
        kern,
        grid=(batch, NSA_KV_HEADS, n_blk),
        in_specs=[pl.BlockSpec((1, tq, qw), lambda b, g, i: (b, i, g)),
                  pl.BlockSpec((1, seqlen, HEAD_DIM), lambda b, g, i: (b, 0, k_off // HEAD_DIM + g)),
                  pl.BlockSpec((1, seqlen, HEAD_DIM), lambda b, g, i: (b, 0, v_off // HEAD_DIM + g)),
                  pl.BlockSpec((1, 1, n_sel_blk, tq), lambda b, g, i: (b, g, 0, i)),
                  pl.BlockSpec((n_blk, tq, rows), lambda b, g, i: (0, 0, g), pipeline_mode=pl.Buffered(1))],
        out_specs=pl.BlockSpec((1, tq, qw), lambda b, g, i: (b, i, g)),
        out_shape=jax.ShapeDtypeStruct((batch, seqlen, ATTN_WIDTH), F32),
        compiler_params=_params(("parallel", "parallel", "arbitrary")),
        name="nsa_selected_attention",
    )(q.reshape(batch, seqlen, ATTN_WIDTH), kv3, kv3, sel, causal_bias)


def _nsa_gate_kernel(g_ref, oc_ref, os_ref, ow_ref, o_ref):
    gate = jax.nn.sigmoid(g_ref[...])
    for h in range(N_HEADS):
        cols = slice(h * HEAD_DIM, (h + 1) * HEAD_DIM)
        mix = (gate[:, 3 * h:3 * h + 1] * oc_ref[:, cols] + gate[:, 3 * h + 1:3 * h + 2] * os_ref[:, cols]
               + gate[:, 3 * h + 2:3 * h + 3] * ow_ref[:, cols])
        o_ref[:, cols] = mix.astype(o_ref.dtype)


def nsa_gate_combine(gates, o_cmp, o_slc, o_win):
    m = gates.shape[0]
    tm = 128
    wide = pl.BlockSpec((tm, ATTN_WIDTH), lambda i: (i, 0))
    return pl.pallas_call(
        _nsa_gate_kernel,
        grid=(m // tm,),
        in_specs=[pl.BlockSpec((tm, gates.shape[1]), lambda i: (i, 0)), wide, wide, wide],
        out_specs=wide,
        out_shape=jax.ShapeDtypeStruct((m, ATTN_WIDTH), BF16),
        compiler_params=_params(("parallel",)),
        name="nsa_gate_combine",
    )(gates, o_cmp, o_slc, o_win)


def _moba_kernel(q_ref, k_ref, v_ref, bias_ref, o_ref, kb_ref, vb_ref, km_ref, *, n_rep, tq, n_blk):
    i = pl.program_id(2)
    rows = n_rep * tq

    @pl.when(i == 0)
    def _():
        k = k_ref[0]
        kb_ref[...] = k.astype(BF16)
        vb_ref[...] = v_ref[0].astype(BF16)
        slot = lax.broadcasted_iota(jnp.int32, (BF16_ROWS, HEAD_DIM), 0)
        means = jnp.zeros((BF16_ROWS, HEAD_DIM), F32)
        for j in range(n_blk):
            means = jnp.where(slot == j, jnp.mean(k[j * tq:(j + 1) * tq], axis=0, keepdims=True), means)
        km_ref[...] = means.astype(BF16)

    qs = _stack_heads(q_ref[0], n_rep)
    gate = _dot_nt(km_ref[...], qs)[0:n_blk]
    blk = lax.broadcasted_iota(jnp.int32, (n_blk, rows), 0)
    past = blk < i
    rank = _first_rank(jnp.where(past, gate, NEG_INF), n_blk)
    chosen = jnp.where(past, jnp.where(rank < min(MOBA_TOPK, max(n_blk - 1, 1)), 1.0, 0.0), 0.0)

    for k in range(n_blk):
        @pl.when(i == k)
        def _(k=k):
            n_keys = (k + 1) * tq
            bias = jnp.concatenate([bias_ref[k - c] for c in range(k + 1)], axis=0)
            lt = _dot_nt(kb_ref[0:n_keys, :], qs) * SCORE_SCALE + bias
            parts = [jnp.where(chosen[c:c + 1, :] > 0.5, lt[c * tq:(c + 1) * tq], NEG_INF) for c in range(k)]
            parts.append(lt[k * tq:n_keys])
            _softmax_pv(jnp.concatenate(parts, axis=0), vb_ref[0:n_keys, :], o_ref, n_rep, tq)


def moba_attention(q, kv, causal_bias, batch, seqlen):
    n_rep = N_HEADS // MOBA_KV_HEADS
    tq = MOBA_BLOCK
    qw = n_rep * HEAD_DIM
    n_blk = seqlen // tq
    assert n_blk <= SUBLANES
    rows = n_rep * tq
    kv3 = kv.reshape(batch, seqlen, kv.shape[1])
    kern = functools.partial(_moba_kernel, n_rep=n_rep, tq=tq, n_blk=n_blk)
    return pl.pallas_call(
        kern,
        grid=(batch, MOBA_KV_HEADS, n_blk),
        in_specs=[pl.BlockSpec((1, tq, qw), lambda b, g, i: (b, i, g)),
                  pl.BlockSpec((1, seqlen, HEAD_DIM), lambda b, g, i: (b, 0, g)),
                  pl.BlockSpec((1, seqlen, HEAD_DIM), lambda b, g, i: (b, 0, MOBA_KV_HEADS + g)),
                  pl.BlockSpec((n_blk, tq, rows), lambda b, g, i: (0, 0, g), pipeline_mode=pl.Buffered(1))],
        out_specs=pl.BlockSpec((1, tq, qw), lambda b, g, i: (b, i, g)),
        out_shape=jax.ShapeDtypeStruct((batch, seqlen, ATTN_WIDTH), BF16),
        scratch_shapes=[pltpu.VMEM((seqlen, HEAD_DIM), BF16), pltpu.VMEM((seqlen, HEAD_DIM), BF16),
                        pltpu.VMEM((BF16_ROWS, HEAD_DIM), BF16)],
        compiler_params=_params(("parallel", "parallel", "arbitrary")),
        name="moba_attention",
    )(q.reshape(batch, seqlen, ATTN_WIDTH), kv3, kv3, causal_bias)


def _dil_combine_kernel(*refs, n_rep, dils, tile):
    n_grp = len(dils)
    o_refs, l_refs, o_ref = refs[:n_grp], refs[n_grp:2 * n_grp], refs[2 * n_grp]
    scratch = list(refs[2 * n_grp + 1:])
    outs, lses = [], []
    for o_g, l_g, dil in zip(o_refs, l_refs, dils):
        if dil == 1:
            outs.append([o_g[0, :, r * HEAD_DIM:(r + 1) * HEAD_DIM] for r in range(n_rep)])
            lses.append(l_g[0])
            continue
        nat_o, nat_l = scratch.pop(0), scratch.pop(0)
        per = tile // dil
        for rho in range(dil):
            for r in range(n_rep):
                nat_o[r, pl.ds(rho, per, stride=dil), :] = o_g[rho, :, r * HEAD_DIM:(r + 1) * HEAD_DIM]
            nat_l[pl.ds(rho, per, stride=dil), :] = l_g[rho]
        outs.append([nat_o[r] for r in range(n_rep)])
        lses.append(nat_l[...])
    top = functools.reduce(jnp.maximum, lses)
    weights = [jnp.exp(l - top) for l in lses]
    den = functools.reduce(lambda x, y: x + y, weights)
    weights = [w / den for w in weights]
    for r in range(n_rep):
        mix = weights[0][:, r:r + 1] * outs[0][r]
        for w, o in zip(weights[1:], outs[1:]):
            mix = mix + w[:, r:r + 1] * o[r]
        o_ref[0, :, r * HEAD_DIM:(r + 1) * HEAD_DIM] = mix.astype(o_ref.dtype)


def dilated_combine(outs, lses, dils, batch, seqlen):
    n_rep = N_HEADS // DIL_KV_HEADS
    tile = COMBINE_TILE
    qw = n_rep * HEAD_DIM
    in_specs, scratch = [], []
    for dil in dils:
        in_specs.append(pl.BlockSpec((None, dil, tile // dil, qw), lambda b, g, i: (b, 0, i, g)))
    for dil in dils:
        in_specs.append(pl.BlockSpec((None, None, dil, tile // dil, LANES), lambda b, g, i: (b, g, 0, i, 0)))
        if dil > 1:
            scratch += [pltpu.VMEM((n_rep, tile, HEAD_DIM), F32), pltpu.VMEM((tile, LANES), F32)]
    return pl.pallas_call(
        functools.partial(_dil_combine_kernel, n_rep=n_rep, dils=tuple(dils), tile=tile),
        grid=(batch, DIL_KV_HEADS, seqlen // tile),
        in_specs=in_specs,
        out_specs=pl.BlockSpec((1, tile, qw), lambda b, g, i: (b, i, g)),
        out_shape=jax.ShapeDtypeStruct((batch, seqlen, ATTN_WIDTH), BF16),
        scratch_shapes=scratch,
        compiler_params=_params(("parallel", "parallel", "parallel")),
        name="dilated_combine",
    )(*outs, *lses)


def nsa_mixer(act, h, w_in, j, cmp_pos, k_w1, k_w2, v_w1, v_w2, w_out, next_gain, tables, batch, seqlen):
    kvw = NSA_KV_HEADS * HEAD_DIM
    c0 = ATTN_WIDTH
    tokens = batch * seqlen
    q = matmul(act,w_in, j, 0, c0, BF16)
    kcvc = matmul(act,w_in, j, c0, 2 * kvw, F32)
    kvsw = matmul(act,w_in, j, c0 + 2 * kvw, 4 * kvw, BF16)
    gates = matmul(act,w_in[:, :, c0 + 6 * kvw:], j, 0, 3 * N_HEADS, F32)
    kcmp, vcmp = nsa_compress(kcvc, cmp_pos, k_w1, k_w2, v_w1, v_w2, batch, seqlen)
    o_cmp, sel = nsa_cmp_attention(q, kcmp, vcmp, tables["cmp"], batch, seqlen)
    o_slc = nsa_selected_attention(q, kvsw, 0, kvw, sel, tables["causal"], batch, seqlen)
    o_win = banded_attention(q.reshape(batch, 1, seqlen, c0), 0, kvsw.reshape(batch, 1, seqlen, 4 * kvw),
                             2 * kvw, 3 * kvw, tables["nsa_win"], batch=batch, seqlen=seqlen,
                             n_kv=NSA_KV_HEADS, dil=1, max_dist=NSA_WINDOW - 1)
    o = nsa_gate_combine(gates, o_cmp.reshape(tokens, ATTN_WIDTH), o_slc.reshape(tokens, ATTN_WIDTH),
                         o_win.reshape(tokens, ATTN_WIDTH))
    return matmul_residual(o, w_out, j, h, 1.0, next_gain)


def dilated_mixer(act, h, w_in, j, w_out, next_gain, tables, batch, seqlen):
    kvw = DIL_KV_HEADS * HEAD_DIM
    group = ATTN_WIDTH + 2 * kvw
    outs, lses, dils = [], [], []
    for gi, (window, dil) in enumerate(DIL_PAIRS):
        proj = matmul(act,w_in, j, gi * group, group, BF16, dil=dil, batch=batch)
        proj = proj.reshape(batch, dil, seqlen // dil, group)
        o, lse = banded_attention(proj, 0, proj, ATTN_WIDTH, ATTN_WIDTH + kvw, tables["dil%d" % dil],
                                  batch=batch, seqlen=seqlen, n_kv=DIL_KV_HEADS, dil=dil,
                                  max_dist=window // dil, want_lse=True)
        outs.append(o)
        lses.append(lse)
        dils.append(dil)
    o = dilated_combine(outs, lses, dils, batch, seqlen)
    return matmul_residual(o.reshape(batch * seqlen, ATTN_WIDTH), w_out, j, h, 1.0, next_gain)


def moba_mixer(act, h, w_in, j, w_out, next_gain, tables, batch, seqlen):
    q = matmul(act,w_in, j, 0, ATTN_WIDTH, BF16)
    kv = matmul(act,w_in, j, ATTN_WIDTH, 2 * MOBA_KV_HEADS * HEAD_DIM, F32)
    o = moba_attention(q, kv, tables["causal"], batch, seqlen)
    return matmul_residual(o.reshape(batch * seqlen, ATTN_WIDTH), w_out, j, h, 1.0, next_gain)


def swa_mixer(act, h, w_in, j, sinks, w_out, next_gain, tables, batch, seqlen):
    kvw = SWA_KV_HEADS * HEAD_DIM
    width = ATTN_WIDTH + 2 * kvw
    proj = matmul(act,w_in, j, 0, width, BF16).reshape(batch, 1, seqlen, width)
    sink_row = jnp.repeat(sinks, math.gcd(seqlen, BAND_BLOCK))[None, :]
    o = banded_attention(proj, 0, proj, ATTN_WIDTH, ATTN_WIDTH + kvw, tables["swa"], batch=batch,
                         seqlen=seqlen, n_kv=SWA_KV_HEADS, dil=1, max_dist=SWA_WINDOW - 1, sink_row=sink_row,
                         out_dtype=BF16)
    return matmul_residual(o.reshape(batch * seqlen, ATTN_WIDTH), w_out, j, h, 1.0, next_gain)


def _band_table_for(rel_table, seqlen, dil, max_dist):
    sub = seqlen // dil
    tq = math.gcd(sub, BAND_BLOCK)
    n_prev = min(-(-max_dist // tq), sub // tq - 1)
    return band_bias_table(rel_table, tq, (n_prev + 1) * tq, n_prev * tq, max_dist, dil)


def kernel(x, rel_table, ffn1_norm, ffn1_w_gate, ffn1_w_up, ffn1_w_down, mix_norm, ffn2_norm, ffn2_w_gate, ffn2_w_up, ffn2_w_down, final_norm, nsa_w_in, nsa_cmp_pos, nsa_cmp_k_w1, nsa_cmp_k_w2, nsa_cmp_v_w1, nsa_cmp_v_w2, nsa_w_out, dil_w_in, dil_w_out, moba_w_in, moba_w_out, swa_w_in, swa_sinks, swa_w_out):
    batch, seqlen, d_model = x.shape
    depth = ffn1_norm.shape[0]
    n_mixers = 4
    h = x.reshape(batch * seqlen, d_model)

    tables = {
        "causal": causal_bias_table(rel_table, SEL_TILE, seqlen // SEL_TILE),
        "cmp": cmp_bias_table(rel_table, seqlen, (seqlen - NSA_CMP_LEN) // NSA_CMP_STRIDE + 1),
        "nsa_win": _band_table_for(rel_table, seqlen, 1, NSA_WINDOW - 1),
        "swa": _band_table_for(rel_table, seqlen, 1, SWA_WINDOW - 1),
    }
    for window, dil in DIL_PAIRS:
        tables["dil%d" % dil] = _band_table_for(rel_table, seqlen, dil, window // dil)

    act = norm_prep(h, ffn1_norm[0])
    for i in range(depth):
        h, act = ffn_half_step(h, act, ffn1_w_gate, ffn1_w_up, ffn1_w_down, i, mix_norm[i])
        m, j = i % n_mixers, i // n_mixers
        if m == 0:
            h, act = nsa_mixer(act, h, nsa_w_in, j, nsa_cmp_pos[j], nsa_cmp_k_w1[j], nsa_cmp_k_w2[j],
                               nsa_cmp_v_w1[j], nsa_cmp_v_w2[j], nsa_w_out, ffn2_norm[i], tables, batch, seqlen)
        elif m == 1:
            h, act = dilated_mixer(act, h, dil_w_in, j, dil_w_out, ffn2_norm[i], tables, batch, seqlen)
        elif m == 2:
            h, act = moba_mixer(act, h, moba_w_in, j, moba_w_out, ffn2_norm[i], tables, batch, seqlen)
        else:
            h, act = swa_mixer(act, h, swa_w_in, j, swa_sinks[j], swa_w_out, ffn2_norm[i], tables, batch, seqlen)
        next_gain = ffn1_norm[i + 1] if i + 1 < depth else None
        h, act = ffn_half_step(h, act, ffn2_w_gate, ffn2_w_up, ffn2_w_down, i, next_gain)
    return rms_norm(h, final_norm, x.dtype).reshape(batch, seqlen, d_model)
```

```python
import functools
import math

import numpy as np
import jax
import jax.numpy as jnp
from jax import lax
from jax.experimental import pallas as pl
from jax.experimental.pallas import tpu as pltpu

HEAD_DIM = 128
N_HEADS = 32
ATTN_WIDTH = N_HEADS * HEAD_DIM
RMS_EPS = 1e-6
REL_BUCKETS = 32
REL_MAX_DIST = 2048
BAND_BLOCK = 128
NSA_KV_HEADS = 4
NSA_CMP_LEN = 32
NSA_CMP_STRIDE = 16
NSA_SEL_LEN = 64
NSA_SEL_TOPN = 16
NSA_WINDOW = 512
DIL_PAIRS = ((128, 1), (512, 4), (2048, 16))
DIL_KV_HEADS = 8
MOBA_BLOCK = 256
MOBA_TOPK = 3
MOBA_KV_HEADS = 8
SWA_WINDOW = 128
SWA_KV_HEADS = 4
ATTN_SCALE = HEAD_DIM ** -0.5
LOG2E = 1.0 / math.log(2.0)
LN2 = math.log(2.0)
SCORE_SCALE = ATTN_SCALE * LOG2E
NEG_INF = -1e30
MASKED_BELOW = -5e29
TINY = 1e-20
FORCED_SCORE = 1e9

LANES = 128
SUBLANES = 8
BF16_ROWS = 16
SEL_TILE = 256
COMBINE_TILE = 512
BANDED_ROWS_PER_STEP = 2048
TABLE_HEADS_PER_STEP = 8
VMEM_LIMIT = 56 * 1024 * 1024

F32 = jnp.float32
BF16 = jnp.bfloat16


def _params(semantics):
    return pltpu.CompilerParams(dimension_semantics=semantics, vmem_limit_bytes=VMEM_LIMIT)


def _dot_nt(a, b):
    return lax.dot_general(a, b, (((1,), (1,)), ((), ())), preferred_element_type=F32)


def _dot_tn(a, b):
    return lax.dot_general(a, b, (((0,), (0,)), ((), ())), preferred_element_type=F32)


def _stack_heads(q, n_heads):
    return jnp.concatenate([q[:, r * HEAD_DIM:(r + 1) * HEAD_DIM] for r in range(n_heads)], axis=0)


def _store_heads(o_ref, o_t, n_heads, tq, col0=0, mix=None):
    for r in range(n_heads):
        cols = slice(col0 + r * HEAD_DIM, col0 + (r + 1) * HEAD_DIM)
        blk = o_t[:, r * tq:(r + 1) * tq]
        if mix is not None:
            gate, branch, prev_ref = mix
            row = 3 * (col0 // HEAD_DIM + r) + branch
            blk = blk * gate[row:row + 1, :]
        val = blk.T
        if mix is not None and mix[2] is not None:
            val = mix[2][0, :, cols] + val
        o_ref[0, :, cols] = val.astype(o_ref.dtype)


def _rms_kernel(x_ref, g_ref, o_ref):
    x = x_ref[...]
    y = x * lax.rsqrt(jnp.mean(x * x, axis=-1, keepdims=True) + RMS_EPS)
    o_ref[...] = (y * g_ref[...]).astype(o_ref.dtype)


def rms_norm(x, gain, out_dtype):
    m, d = x.shape
    tm = 256
    return pl.pallas_call(
        _rms_kernel,
        grid=(m // tm,),
        in_specs=[pl.BlockSpec((tm, d), lambda i: (i, 0)), pl.BlockSpec((1, d), lambda i: (0, 0))],
        out_specs=pl.BlockSpec((tm, d), lambda i: (i, 0)),
        out_shape=jax.ShapeDtypeStruct((m, d), out_dtype),
        compiler_params=_params(("parallel",)),
        name="rms_norm",
    )(x, gain.reshape(1, d))


def _norm_prep_kernel(x_ref, g_ref, a_ref, ssq_ref):
    x = x_ref[...]
    a_ref[...] = (x * g_ref[...]).astype(a_ref.dtype)
    ssq_ref[...] = jnp.broadcast_to(jnp.sum(x * x, axis=-1, keepdims=True), ssq_ref.shape)


def norm_prep(x, gain):
    m, d = x.shape
    tm = 256
    return pl.pallas_call(
        _norm_prep_kernel,
        grid=(m // tm,),
        in_specs=[pl.BlockSpec((tm, d), lambda i: (i, 0)), pl.BlockSpec((1, d), lambda i: (0, 0))],
        out_specs=[pl.BlockSpec((tm, d), lambda i: (i, 0)), pl.BlockSpec((tm, LANES), lambda i: (i, 0))],
        out_shape=[jax.ShapeDtypeStruct((m, d), BF16), jax.ShapeDtypeStruct((m, LANES), F32)],
        compiler_params=_params(("parallel",)),
        name="norm_prep",
    )(x, gain.reshape(1, d))


def _row_scale(ssq_ref, d):
    return lax.rsqrt(ssq_ref[:, 0:1] * (1.0 / d) + RMS_EPS)


MM_VMEM_BUDGET = 50 * 1024 * 1024


def _mm_kernel(a_ref, ssq_ref, b_ref, o_ref, *scratch, dil, transposed):
    res = jnp.dot(a_ref[...], b_ref[...].astype(BF16), preferred_element_type=F32)
    res = res * _row_scale(ssq_ref, a_ref.shape[1])
    if transposed:
        o_ref[...] = res.T.astype(o_ref.dtype)
        return
    if dil == 1:
        o_ref[...] = res.astype(o_ref.dtype)
        return
    scr_ref, = scratch
    tm, tn = res.shape
    for s in range(tn // LANES):
        scr_ref[s] = res[:, s * LANES:(s + 1) * LANES]
    for rho in range(dil):
        for s in range(tn // LANES):
            o_ref[rho, :, s * LANES:(s + 1) * LANES] = (
                scr_ref[s, pl.ds(rho, tm // dil, stride=dil), :].astype(o_ref.dtype))


def _mm_res_kernel(a_ref, b_ref, r_ref, *rest, scale, emit_next):
    if emit_next:
        g_ref, o_ref, an_ref, ssq_ref = rest

        @pl.when(pl.program_id(1) == 0)
        def _():
            ssq_ref[...] = jnp.zeros(ssq_ref.shape, F32)
    else:
        o_ref, = rest
    h = r_ref[...] + scale * jnp.dot(a_ref[...], b_ref[...].astype(BF16), preferred_element_type=F32)
    o_ref[...] = h
    if emit_next:
        an_ref[...] = (h * g_ref[...]).astype(an_ref.dtype)
        ssq_ref[...] += jnp.sum(h * h, axis=-1, keepdims=True)


def _mm_swiglu_kernel(a_ref, ssq_ref, bg_ref, bu_ref, o_ref):
    a = a_ref[...]
    r = _row_scale(ssq_ref, a_ref.shape[1])
    gate = jnp.dot(a, bg_ref[...].astype(BF16), preferred_element_type=F32) * r
    up = jnp.dot(a, bu_ref[...].astype(BF16), preferred_element_type=F32) * r
    o_ref[...] = (jax.nn.silu(gate) * up).astype(o_ref.dtype)


def _mm_tiles(m, k, n, n_weights, io_bytes):
    tm = min(m, 1024)
    if n < LANES:
        return tm, n
    for tn in (512, 256, 128):
        need = (2 * tm * k * 2 + n_weights * (2 * k * tn * 4 + k * tn * 2) + 2 * tm * tn * io_bytes
                + n_weights * tm * tn * 4)
        if n % tn == 0 and need <= MM_VMEM_BUDGET:
            return tm, tn
    raise ValueError("no matmul tile fits VMEM")


def _weight_spec(w, layer, col_off, tn):
    assert col_off % tn == 0
    return pl.BlockSpec((None, w.shape[1], tn), lambda i, j: (layer, 0, col_off // tn + j))


def matmul(act, w, layer, col_off, n, out_dtype, *, dil=1, batch=1, transposed=False):
    a, ssq = act
    m, k = a.shape
    tm, tn = _mm_tiles(m, k, n, 1, jnp.dtype(out_dtype).itemsize)
    in_specs = [pl.BlockSpec((tm, k), lambda i, j: (i, 0)), pl.BlockSpec((tm, LANES), lambda i, j: (i, 0)),
                _weight_spec(w, layer, col_off, tn)]
    if transposed:
        assert dil == 1 and tn % LANES == 0
        out_specs = pl.BlockSpec((tn, tm), lambda i, j: (j, i))
        out_shape = jax.ShapeDtypeStruct((n, m), out_dtype)
        scratch = []
    elif dil == 1:
        out_specs = pl.BlockSpec((tm, tn), lambda i, j: (i, j))
        out_shape = jax.ShapeDtypeStruct((m, n), out_dtype)
        scratch = []
    else:
        per_batch = m // batch // tm
        assert m % (batch * tm) == 0 and tm % (dil * BF16_ROWS) == 0 and tn % LANES == 0
        out_specs = pl.BlockSpec((None, dil, tm // dil, tn), lambda i, j: (i // per_batch, 0, i % per_batch, j))
        out_shape = jax.ShapeDtypeStruct((batch, dil, m // batch // dil, n), out_dtype)
        scratch = [pltpu.VMEM((tn // LANES, tm, LANES), F32)]
    return pl.pallas_call(
        functools.partial(_mm_kernel, dil=dil, transposed=transposed),
        grid=(m // tm, n // tn),
        in_specs=in_specs,
        out_specs=out_specs,
        out_shape=out_shape,
        scratch_shapes=scratch,
        compiler_params=_params(("parallel", "arbitrary")),
        name="matmul",
    )(a, ssq, w)


def matmul_residual(a, w, layer, res, scale, next_gain=None):
    m, k = a.shape
    n = w.shape[2]
    emit_next = next_gain is not None
    tm, tn = _mm_tiles(m, k, n, 1, 10 if emit_next else 8)
    tile = pl.BlockSpec((tm, tn), lambda i, j: (i, j))
    in_specs = [pl.BlockSpec((tm, k), lambda i, j: (i, 0)), _weight_spec(w, layer, 0, tn), tile]
    args = [a, w, res]
    out_specs, out_shape = tile, jax.ShapeDtypeStruct((m, n), F32)
    if emit_next:
        in_specs.append(pl.BlockSpec((1, tn), lambda i, j: (0, j)))
        args.append(next_gain.reshape(1, n))
        out_specs = [tile, tile, pl.BlockSpec((tm, LANES), lambda i, j: (i, 0))]
        out_shape = [out_shape, jax.ShapeDtypeStruct((m, n), BF16), jax.ShapeDtypeStruct((m, LANES), F32)]
    outs = pl.pallas_call(
        functools.partial(_mm_res_kernel, scale=scale, emit_next=emit_next),
        grid=(m // tm, n // tn),
        in_specs=in_specs,
        out_specs=out_specs,
        out_shape=out_shape,
        compiler_params=_params(("parallel", "arbitrary")),
        name="matmul_residual",
    )(*args)
    return (outs[0], (outs[1], outs[2])) if emit_next else (outs, None)


def matmul_swiglu(act, wg, wu, layer):
    a, ssq = act
    m, k = a.shape
    n = wg.shape[2]
    tm, tn = _mm_tiles(m, k, n, 2, 2)
    return pl.pallas_call(
        _mm_swiglu_kernel,
        grid=(m // tm, n // tn),
        in_specs=[pl.BlockSpec((tm, k), lambda i, j: (i, 0)),
                  pl.BlockSpec((tm, LANES), lambda i, j: (i, 0)),
                  _weight_spec(wg, layer, 0, tn),
                  _weight_spec(wu, layer, 0, tn)],
        out_specs=pl.BlockSpec((tm, tn), lambda i, j: (i, j)),
        out_shape=jax.ShapeDtypeStruct((m, n), BF16),
        compiler_params=_params(("parallel", "arbitrary")),
        name="matmul_swiglu",
    )(a, ssq, wg, wu)


def ffn_half_step(h, act, w_gate, w_up, w_down, layer, next_gain):
    hidden = matmul_swiglu(act, w_gate, w_up, layer)
    return matmul_residual(hidden, w_down, layer, h, 0.5, next_gain)


def _t5_bucket(dist):
    n = jnp.maximum(dist, 0)
    exact = REL_BUCKETS // 2
    nf = jnp.maximum(n, 1).astype(F32)
    large = exact + (jnp.log(nf / exact) * ((REL_BUCKETS - exact) / math.log(REL_MAX_DIST / exact))).astype(jnp.int32)
    return jnp.where(n < exact, n, jnp.minimum(large, REL_BUCKETS - 1))


def _bias_table_kernel(tab_ref, o_ref, bucket_ref, *, base0, base_step, key_stride, max_dist, n_valid_keys,
                       dist_scale, heads_per_step):
    blk = pl.program_id(0)
    hg = pl.program_id(1)
    n_qry = bucket_ref.shape[1]

    @pl.when(hg == 0)
    def _():
        n_keys, n_qry = bucket_ref.shape
        key = lax.broadcasted_iota(jnp.int32, (n_keys, n_qry), 0)
        qry = lax.broadcasted_iota(jnp.int32, (n_keys, n_qry), 1)
        dist = base0 + blk * base_step + qry - key * key_stride
        valid = (dist >= 0) & (dist <= max_dist) & (key < n_valid_keys)
        bucket_ref[...] = jnp.where(valid, _t5_bucket(dist * dist_scale), -1)

    bucket = bucket_ref[...]
    bits = [(bucket & (1 << t)) != 0 for t in range(REL_BUCKETS.bit_length() - 1)]
    for hh in range(heads_per_step):
        h = hg * heads_per_step + hh
        level = [tab_ref[b, h] * LOG2E for b in range(REL_BUCKETS)]
        for odd in bits:
            level = [jnp.where(odd, level[2 * t + 1], level[2 * t]) for t in range(len(level) // 2)]
        tile = jnp.where(bucket >= 0, level[0], NEG_INF)
        if len(o_ref.shape) == 2:
            o_ref[:, hh * n_qry:(hh + 1) * n_qry] = tile
        else:
            o_ref[0, :, hh * n_qry:(hh + 1) * n_qry] = tile


def band_bias_table(rel_table, tq, span, pad, max_dist, dist_scale):
    kern = functools.partial(_bias_table_kernel, base0=pad, base_step=0, key_stride=1, max_dist=max_dist,
                             n_valid_keys=span, dist_scale=dist_scale, heads_per_step=TABLE_HEADS_PER_STEP)
    return pl.pallas_call(
        kern,
        grid=(1, N_HEADS // TABLE_HEADS_PER_STEP),
        in_specs=[pl.BlockSpec(memory_space=pltpu.SMEM)],
        out_specs=pl.BlockSpec((span, TABLE_HEADS_PER_STEP * tq), lambda j, h: (0, h)),
        out_shape=jax.ShapeDtypeStruct((span, N_HEADS * tq), F32),
        scratch_shapes=[pltpu.VMEM((span, tq), jnp.int32)],
        compiler_params=_params(("parallel", "arbitrary")),
        name="band_bias_table",
    )(rel_table)


def causal_bias_table(rel_table, tile, n_cls):
    kern = functools.partial(_bias_table_kernel, base0=0, base_step=tile, key_stride=1, max_dist=2 ** 30,
                             n_valid_keys=tile, dist_scale=1, heads_per_step=1)
    return pl.pallas_call(
        kern,
        grid=(n_cls, N_HEADS),
        in_specs=[pl.BlockSpec(memory_space=pltpu.SMEM)],
        out_specs=pl.BlockSpec((1, tile, tile), lambda c, h: (c, 0, h)),
        out_shape=jax.ShapeDtypeStruct((n_cls, tile, N_HEADS * tile), F32),
        scratch_shapes=[pltpu.VMEM((tile, tile), jnp.int32)],
        compiler_params=_params(("parallel", "arbitrary")),
        name="causal_bias_table",
    )(rel_table)


def cmp_bias_table(rel_table, seqlen, n_cmp):
    tq = SEL_TILE
    kern = functools.partial(_bias_table_kernel, base0=-(NSA_CMP_LEN - 1), base_step=tq,
                             key_stride=NSA_CMP_STRIDE, max_dist=2 ** 30, n_valid_keys=n_cmp, dist_scale=1,
                             heads_per_step=TABLE_HEADS_PER_STEP)
    return pl.pallas_call(
        kern,
        grid=(seqlen // tq, N_HEADS // TABLE_HEADS_PER_STEP),
        in_specs=[pl.BlockSpec(memory_space=pltpu.SMEM)],
        out_specs=pl.BlockSpec((1, LANES, TABLE_HEADS_PER_STEP * tq), lambda i, h: (i, 0, h)),
        out_shape=jax.ShapeDtypeStruct((seqlen // tq, LANES, N_HEADS * tq), F32),
        scratch_shapes=[pltpu.VMEM((LANES, tq), jnp.int32)],
        compiler_params=_params(("parallel", "arbitrary")),
        name="cmp_bias_table",
    )(rel_table)


def _banded_kernel(*refs, n_rep, n_grp, tq, n_prev, seq, has_sink, want_lse, gate_branch):
    q_ref, k_ref, v_ref, bias_ref = refs[:4]
    pos = 4
    sink_ref = None
    if has_sink:
        sink_ref = refs[pos]
        pos += 1
    mix = None
    if gate_branch is not None:
        mix = (jax.nn.sigmoid(refs[pos][...]), gate_branch, refs[pos + 1])
        pos += 2
    o_ref = refs[pos]
    pos += 1
    lse_ref = None
    if want_lse:
        lse_ref = refs[pos]
        pos += 1
    kpad_ref, vpad_ref = refs[pos:pos + 2]

    i = pl.program_id(3)
    pad = n_prev * tq
    span = pad + tq
    rows = n_rep * tq
    qw = n_rep * HEAD_DIM

    @pl.when(i == 0)
    def _():
        if pad:
            kpad_ref[0:pad, :] = jnp.zeros((pad, n_grp * HEAD_DIM), BF16)
            vpad_ref[0:pad, :] = jnp.zeros((pad, n_grp * HEAD_DIM), BF16)
        kpad_ref[pad:pad + seq, :] = k_ref[0]
        vpad_ref[pad:pad + seq, :] = v_ref[0]

    start = pl.multiple_of(i * tq, tq)

    def attend(gg, span_has_padding):
        kv_cols = slice(gg * HEAD_DIM, (gg + 1) * HEAD_DIM)
        row_cols = slice(gg * rows, (gg + 1) * rows)
        ks = kpad_ref[pl.ds(start, span), kv_cols]
        vs = vpad_ref[pl.ds(start, span), kv_cols]
        qs = _stack_heads(q_ref[0, :, gg * qw:(gg + 1) * qw], n_rep)
        lt = _dot_nt(ks, qs) * SCORE_SCALE + bias_ref[:, row_cols]
        if span_has_padding:
            key = lax.broadcasted_iota(jnp.int32, (span, rows), 0)
            lt = jnp.where(key >= pad - i * tq, lt, NEG_INF)
        m = jnp.max(lt, axis=0, keepdims=True)
        sink = None
        if has_sink:
            sink = sink_ref[:, row_cols] * LOG2E
            m = jnp.maximum(m, sink)
        p = jnp.exp2(lt - m)
        s = jnp.sum(p, axis=0, keepdims=True)
        if has_sink:
            s = s + jnp.exp2(sink - m)
        o_t = _dot_tn(vs, p.astype(BF16))
        s = jnp.maximum(s, TINY)
        _store_heads(o_ref, o_t / s, n_rep, tq, gg * qw, mix)
        if want_lse:
            lse = (m + jnp.log2(s)) * LN2
            head = lax.broadcasted_iota(jnp.int32, (LANES, tq), 0)
            tile = jnp.zeros((LANES, tq), F32)
            for r in range(n_rep):
                tile = jnp.where(head == r, lse[:, r * tq:(r + 1) * tq], tile)
            lse_ref[gg, 0] = tile.T

    if pad:
        @pl.when(i < n_prev)
        def _():
            for gg in range(n_grp):
                attend(gg, True)

        @pl.when(i >= n_prev)
        def _():
            for gg in range(n_grp):
                attend(gg, False)
    else:
        for gg in range(n_grp):
            attend(gg, False)


def banded_attention(q_arr, q_off, kv_arr, k_off, v_off, bias, *, batch, seqlen, n_kv, dil,
                     max_dist, sink_row=None, want_lse=False, out_dtype=F32, gated=None):
    n_rep = N_HEADS // n_kv
    sub = seqlen // dil
    tq = math.gcd(sub, BAND_BLOCK)
    n_blk = sub // tq
    n_prev = min(-(-max_dist // tq), n_blk - 1)
    span = (n_prev + 1) * tq
    qw = n_rep * HEAD_DIM
    rows = n_rep * tq
    n_grp = max(c for c in (4, 2, 1) if c * rows <= BANDED_ROWS_PER_STEP and n_kv % c == 0)
    gqw, gkw = n_grp * qw, n_grp * HEAD_DIM
    assert q_arr.shape[:3] == kv_arr.shape[:3] == (batch, dil, sub)
    assert q_off % gqw == 0 and k_off % gkw == 0 and v_off % gkw == 0 and bias.shape == (span, N_HEADS * tq)

    in_specs = [
        pl.BlockSpec((None, 1, tq, gqw), lambda b, rho, g, i: (b, rho, i, q_off // gqw + g)),
        pl.BlockSpec((None, 1, sub, gkw), lambda b, rho, g, i: (b, rho, 0, k_off // gkw + g)),
        pl.BlockSpec((None, 1, sub, gkw), lambda b, rho, g, i: (b, rho, 0, v_off // gkw + g)),
        pl.BlockSpec((span, n_grp * rows), lambda b, rho, g, i: (0, g)),
    ]
    args = [q_arr, kv_arr, kv_arr, bias]
    if sink_row is not None:
        in_specs.append(pl.BlockSpec((1, n_grp * rows), lambda b, rho, g, i: (0, g)))
        args.append(sink_row)
    if gated is not None:
        gates_t, gate_branch, prev = gated
        assert dil == 1
        in_specs.append(pl.BlockSpec((3 * n_grp * n_rep, tq), lambda b, rho, g, i: (g, b * n_blk + i)))
        in_specs.append(pl.BlockSpec((None, 1, tq, gqw), lambda b, rho, g, i: (b, rho, i, g)))
        args += [gates_t, prev]
    out_specs = [pl.BlockSpec((None, 1, tq, gqw), lambda b, rho, g, i: (b, rho, i, g))]
    out_shape = [jax.ShapeDtypeStruct((batch, dil, sub, ATTN_WIDTH), out_dtype)]
    if want_lse:
        out_specs.append(pl.BlockSpec((None, n_grp, 1, tq, LANES), lambda b, rho, g, i: (b, g, rho, i, 0)))
        out_shape.append(jax.ShapeDtypeStruct((batch, n_kv, dil, sub, LANES), F32))
    kern = functools.partial(_banded_kernel, n_rep=n_rep, n_grp=n_grp, tq=tq, n_prev=n_prev, seq=sub,
                             has_sink=sink_row is not None, want_lse=want_lse,
                             gate_branch=None if gated is None else gated[1])
    outs = pl.pallas_call(
        kern,
        grid=(batch, dil, n_kv // n_grp, n_blk),
        in_specs=in_specs,
        out_specs=out_specs,
        out_shape=out_shape,
        scratch_shapes=[pltpu.VMEM((n_prev * tq + sub, gkw), BF16),
                        pltpu.VMEM((n_prev * tq + sub, gkw), BF16)],
        compiler_params=_params(("parallel", "parallel", "parallel", "arbitrary")),
        name="banded_attention",
    )(*args)
    return tuple(outs) if want_lse else outs[0]


def _softmax_pv(lt, v, o_ref, n_heads, tq, col0=0, mix=None):
    m = jnp.max(lt, axis=0, keepdims=True)
    p = jnp.exp2(lt - m)
    s = jnp.maximum(jnp.sum(p, axis=0, keepdims=True), TINY)
    _store_heads(o_ref, _dot_tn(v, p.astype(BF16)) / s, n_heads, tq, col0, mix)


def _first_rank(score, n_cand):
    idx = lax.broadcasted_iota(jnp.int32, score.shape, 0)
    rank = jnp.zeros(score.shape, F32)
    for jp in range(n_cand):
        other = score[jp:jp + 1, :]
        ahead = jnp.where(other > score, 1.0, jnp.where(other == score, jnp.where(idx > jp, 1.0, 0.0), 0.0))
        rank = rank + ahead
    return rank


def _nsa_cmp_kernel(x_ref, pos_ref, w1k_ref, w2k_ref, w1v_ref, w2v_ref, ko_ref, vo_ref):
    width = 2 * NSA_KV_HEADS * HEAD_DIM
    for kv, (w1_ref, w2_ref, o_ref) in enumerate(((w1k_ref, w2k_ref, ko_ref), (w1v_ref, w2v_ref, vo_ref))):
        for g in range(NSA_KV_HEADS):
            off = kv * NSA_KV_HEADS * HEAD_DIM + g * HEAD_DIM
            chunk = jnp.concatenate(
                [x_ref[0, :, l * width + off:l * width + off + HEAD_DIM] for l in range(NSA_CMP_STRIDE)], axis=1)
            first = jnp.dot((chunk + pos_ref[0:1, :]).astype(BF16), w1_ref[0], preferred_element_type=F32)
            second = jnp.dot((chunk + pos_ref[1:2, :]).astype(BF16), w1_ref[1], preferred_element_type=F32)
            hidden = jax.nn.gelu(first + pltpu.roll(second, second.shape[0] - 1, axis=0))
            o_ref[0, g] = jnp.dot(hidden.astype(BF16), w2_ref[...], preferred_element_type=F32).astype(o_ref.dtype)


def nsa_compress(kcvc, cmp_pos, k_w1, k_w2, v_w1, v_w2, batch, seqlen):
    n_chunk = seqlen // NSA_CMP_STRIDE
    width = 2 * NSA_KV_HEADS * HEAD_DIM
    half = NSA_CMP_STRIDE * HEAD_DIM
    x = kcvc.reshape(batch, n_chunk, NSA_CMP_STRIDE * width)
    out = jax.ShapeDtypeStruct((batch, NSA_KV_HEADS, n_chunk, HEAD_DIM), BF16)
    full = lambda shape: pl.BlockSpec(shape, lambda b: (0,) * len(shape))
    return pl.pallas_call(
        _nsa_cmp_kernel,
        grid=(batch,),
        in_specs=[pl.BlockSpec((1, n_chunk, NSA_CMP_STRIDE * width), lambda b: (b, 0, 0)),
                  full((2, half)), full((2, half, HEAD_DIM)), full((HEAD_DIM, HEAD_DIM)),
                  full((2, half, HEAD_DIM)), full((HEAD_DIM, HEAD_DIM))],
        out_specs=[pl.BlockSpec((1, NSA_KV_HEADS, n_chunk, HEAD_DIM), lambda b: (b, 0, 0, 0))] * 2,
        out_shape=[out, out],
        compiler_params=_params(("parallel",)),
        name="nsa_compress",
    )(x, cmp_pos.reshape(2, half), k_w1.reshape(2, half, HEAD_DIM).astype(BF16), k_w2.astype(BF16),
      v_w1.reshape(2, half, HEAD_DIM).astype(BF16), v_w2.astype(BF16))


def _nsa_cmp_attn_kernel(q_ref, kc_ref, vc_ref, bias_ref, c2s_ref, g_ref, o_ref, sel_ref, *, n_rep, tq, n_sel_blk):
    i = pl.program_id(2)
    qs = _stack_heads(q_ref[0], n_rep)
    bias = bias_ref[0]
    valid = bias > MASKED_BELOW
    lt = jnp.where(valid, _dot_nt(kc_ref[0, 0], qs) * SCORE_SCALE +bias, NEG_INF)
    m = jnp.max(lt, axis=0, keepdims=True)
    p = jnp.where(valid, jnp.exp2(lt - m), 0.0)
    s = jnp.sum(p, axis=0, keepdims=True)
    p_cmp = p / jnp.maximum(s, TINY)
    _store_heads(o_ref, _dot_tn(vc_ref[0, 0], p_cmp.astype(BF16)), n_rep, tq,
                 mix=(jax.nn.sigmoid(g_ref[...]), 0, None))

    p_sum = p_cmp[:, 0:tq]
    for r in range(1, n_rep):
        p_sum = p_sum + p_cmp[:, r * tq:(r + 1) * tq]
    imp = jnp.dot(c2s_ref[...], p_sum.astype(BF16), preferred_element_type=F32)[0:n_sel_blk]
    blk = lax.broadcasted_iota(jnp.int32, (n_sel_blk, tq), 0)
    tpos = i * tq + lax.broadcasted_iota(jnp.int32, (n_sel_blk, tq), 1)
    cur = tpos // NSA_SEL_LEN
    forced = (blk == 0) | (blk == cur) | (blk == cur - 1)
    score = jnp.where(forced, FORCED_SCORE, jnp.where(blk * NSA_SEL_LEN <= tpos, imp, NEG_INF))
    sel_ref[0, 0] = jnp.where(_first_rank(score, n_sel_blk) < min(NSA_SEL_TOPN, n_sel_blk), 1.0, 0.0)


def nsa_cmp_attention(q, kcmp, vcmp, bias, gates_t, batch, seqlen):
    n_rep = N_HEADS // NSA_KV_HEADS
    tq = SEL_TILE
    n_tile = seqlen // tq
    qw = n_rep * HEAD_DIM
    n_sel_blk = seqlen // NSA_SEL_LEN
    n_cmp = (seqlen - NSA_CMP_LEN) // NSA_CMP_STRIDE + 1
    a, b = NSA_SEL_LEN // NSA_CMP_STRIDE, NSA_CMP_LEN // NSA_CMP_STRIDE
    w = np.zeros((LANES, LANES), np.float32)
    j = np.arange(n_sel_blk)
    for mm in range(a):
        for nn in range(b):
            ii = a * j + mm + nn - (b - 1)
            ok = (ii >= 0) & (ii < n_cmp)
            np.add.at(w, (j[ok], ii[ok]), 1.0)
    kern = functools.partial(_nsa_cmp_attn_kernel, n_rep=n_rep, tq=tq, n_sel_blk=n_sel_blk)
    return pl.pallas_call(
        kern,
        grid=(batch, NSA_KV_HEADS, seqlen // tq),
        in_specs=[pl.BlockSpec((1, tq, qw), lambda b_, g, i: (b_, i, g)),
                  pl.BlockSpec((1, 1, LANES, HEAD_DIM), lambda b_, g, i: (b_, g, 0, 0)),
                  pl.BlockSpec((1, 1, LANES, HEAD_DIM), lambda b_, g, i: (b_, g, 0, 0)),
                  pl.BlockSpec((1, LANES, n_rep * tq), lambda b_, g, i: (i, 0, g)),
                  pl.BlockSpec((LANES, LANES), lambda b_, g, i: (0, 0)),
                  pl.BlockSpec((3 * n_rep, tq), lambda b_, g, i: (g, b_ * n_tile + i))],
        out_specs=[pl.BlockSpec((1, tq, qw), lambda b_, g, i: (b_, i, g)),
                   pl.BlockSpec((1, 1, n_sel_blk, tq), lambda b_, g, i: (b_, g, 0, i))],
        out_shape=[jax.ShapeDtypeStruct((batch, seqlen, ATTN_WIDTH), F32),
                   jax.ShapeDtypeStruct((batch, NSA_KV_HEADS, n_sel_blk, seqlen), F32)],
        compiler_params=_params(("parallel", "parallel", "arbitrary")),
        name="nsa_cmp_attention",
    )(q.reshape(batch, seqlen, ATTN_WIDTH), kcmp, vcmp, bias, jnp.asarray(w, BF16), gates_t)


def _nsa_sel_kernel(q_ref, k_ref, v_ref, sel_ref, bias_ref, g_ref, prev_ref, o_ref, *, n_rep, tq, n_blk, n_pass):
    i = pl.program_id(2)
    mix = (jax.nn.sigmoid(g_ref[...]), 1, prev_ref)
    per = tq // NSA_SEL_LEN
    hpp = n_rep // n_pass
    for k in range(n_blk):
        @pl.when(i == k)
        def _(k=k):
            n_keys = (k + 1) * tq
            keys = k_ref[0, 0:n_keys, :]
            vals = v_ref[0, 0:n_keys, :]
            for part in range(n_pass):
                qs = _stack_heads(q_ref[0, :, part * hpp * HEAD_DIM:(part + 1) * hpp * HEAD_DIM], hpp)
                cols = slice(part * hpp * tq, (part + 1) * hpp * tq)
                bias = jnp.concatenate([bias_ref[k - c, :, cols] for c in range(k + 1)], axis=0)
                lt = _dot_nt(keys, qs) * SCORE_SCALE + bias
                slabs = []
                for b in range((k + 1) * per):
                    on = jnp.concatenate([sel_ref[0, 0, b:b + 1, :]] * hpp, axis=1)
                    slabs.append(jnp.where(on > 0.5, lt[b * NSA_SEL_LEN:(b + 1) * NSA_SEL_LEN], NEG_INF))
                _softmax_pv(jnp.concatenate(slabs, axis=0), vals, o_ref, hpp, tq, part * hpp * HEAD_DIM, mix)


def nsa_selected_attention(q, kv, k_off, v_off, sel, causal_bias, gates_t, prev, batch, seqlen):
    n_rep = N_HEADS // NSA_KV_HEADS
    tq = SEL_TILE
    qw = n_rep * HEAD_DIM
    n_blk = seqlen // tq
    n_sel_blk = seqlen // NSA_SEL_LEN
    ckv = kv.shape[1]
    kv3 = kv.reshape(batch, seqlen, ckv)
    kern = functools.partial(_nsa_sel_kernel, n_rep=n_rep, tq=tq, n_blk=n_blk, n_pass=2)
    rows = n_rep * tq
    return pl.pallas_call(
        kern,
        grid=(batch, NSA_KV_HEADS, n_blk),
        in_specs=[pl.BlockSpec((1, tq, qw), lambda b, g, i: (b, i, g)),
                  pl.BlockSpec((1, seqlen, HEAD_DIM), lambda b, g, i: (b, 0, k_off // HEAD_DIM + g)),
                  pl.BlockSpec((1, seqlen, HEAD_DIM), lambda b, g, i: (b, 0, v_off // HEAD_DIM + g)),
                  pl.BlockSpec((1, 1, n_sel_blk, tq), lambda b, g, i: (b, g, 0, i)),
                  pl.BlockSpec((n_blk, tq, rows), lambda b, g, i: (0, 0, g), pipeline_mode=pl.Buffered(1)),
                  pl.BlockSpec((3 * n_rep, tq), lambda b, g, i: (g, b * n_blk + i)),
                  pl.BlockSpec((1, tq, qw), lambda b, g, i: (b, i, g))],
        out_specs=pl.BlockSpec((1, tq, qw), lambda b, g, i: (b, i, g)),
        out_shape=jax.ShapeDtypeStruct((batch, seqlen, ATTN_WIDTH), F32),
        compiler_params=_params(("parallel", "parallel", "arbitrary")),
        name="nsa_selected_attention",
    )(q.reshape(batch, seqlen, ATTN_WIDTH), kv3, kv3, sel, causal_bias, gates_t, prev)


def _moba_kernel(q_ref, k_ref, v_ref, bias_ref, o_ref, kb_ref, vb_ref, km_ref, *, n_rep, tq, n_blk):
    i = pl.program_id(2)
    rows = n_rep * tq

    @pl.when(i == 0)
    def _():
        k = k_ref[0]
        kb_ref[...] = k.astype(BF16)
        vb_ref[...] = v_ref[0].astype(BF16)
        slot = lax.broadcasted_iota(jnp.int32, (BF16_ROWS, HEAD_DIM), 0)
        means = jnp.zeros((BF16_ROWS, HEAD_DIM), F32)
        for j in range(n_blk):
            means = jnp.where(slot == j, jnp.mean(k[j * tq:(j + 1) * tq], axis=0, keepdims=True), means)
        km_ref[...] = means.astype(BF16)

    qs = _stack_heads(q_ref[0], n_rep)
    gate = _dot_nt(km_ref[...], qs)[0:n_blk]
    blk = lax.broadcasted_iota(jnp.int32, (n_blk, rows), 0)
    past = blk < i
    rank = _first_rank(jnp.where(past, gate, NEG_INF), n_blk)
    chosen = jnp.where(past, jnp.where(rank < min(MOBA_TOPK, max(n_blk - 1, 1)), 1.0, 0.0), 0.0)

    for k in range(n_blk):
        @pl.when(i == k)
        def _(k=k):
            n_keys = (k + 1) * tq
            bias = jnp.concatenate([bias_ref[k - c] for c in range(k + 1)], axis=0)
            lt = _dot_nt(kb_ref[0:n_keys, :], qs) * SCORE_SCALE + bias
            parts = [jnp.where(chosen[c:c + 1, :] > 0.5, lt[c * tq:(c + 1) * tq], NEG_INF) for c in range(k)]
            parts.append(lt[k * tq:n_keys])
            _softmax_pv(jnp.concatenate(parts, axis=0), vb_ref[0:n_keys, :], o_ref, n_rep, tq)


def moba_attention(q, kv, causal_bias, batch, seqlen):
    n_rep = N_HEADS // MOBA_KV_HEADS
    tq = MOBA_BLOCK
    qw = n_rep * HEAD_DIM
    n_blk = seqlen // tq
    assert n_blk <= SUBLANES
    rows = n_rep * tq
    kv3 = kv.reshape(batch, seqlen, kv.shape[1])
    kern = functools.partial(_moba_kernel, n_rep=n_rep, tq=tq, n_blk=n_blk)
    return pl.pallas_call(
        kern,
        grid=(batch, MOBA_KV_HEADS, n_blk),
        in_specs=[pl.BlockSpec((1, tq, qw), lambda b, g, i: (b, i, g)),
                  pl.BlockSpec((1, seqlen, HEAD_DIM), lambda b, g, i: (b, 0, g)),
                  pl.BlockSpec((1, seqlen, HEAD_DIM), lambda b, g, i: (b, 0, MOBA_KV_HEADS + g)),
                  pl.BlockSpec((n_blk, tq, rows), lambda b, g, i: (0, 0, g), pipeline_mode=pl.Buffered(1))],
        out_specs=pl.BlockSpec((1, tq, qw), lambda b, g, i: (b, i, g)),
        out_shape=jax.ShapeDtypeStruct((batch, seqlen, ATTN_WIDTH), BF16),
        scratch_shapes=[pltpu.VMEM((seqlen, HEAD_DIM), BF16), pltpu.VMEM((seqlen, HEAD_DIM), BF16),
                        pltpu.VMEM((BF16_ROWS, HEAD_DIM), BF16)],
        compiler_params=_params(("parallel", "parallel", "arbitrary")),
        name="moba_attention",
    )(q.reshape(batch, seqlen, ATTN_WIDTH), kv3, kv3, causal_bias)


def _dil_combine_kernel(*refs, n_rep, dils, tile):
    n_grp = len(dils)
    o_refs, l_refs, o_ref = refs[:n_grp], refs[n_grp:2 * n_grp], refs[2 * n_grp]
    scratch = list(refs[2 * n_grp + 1:])
    outs, lses = [], []
    for o_g, l_g, dil in zip(o_refs, l_refs, dils):
        if dil == 1:
            outs.append([o_g[0, :, r * HEAD_DIM:(r + 1) * HEAD_DIM] for r in range(n_rep)])
            lses.append(l_g[0])
            continue
        nat_o, nat_l = scratch.pop(0), scratch.pop(0)
        per = tile // dil
        for rho in range(dil):
            for r in range(n_rep):
                nat_o[r, pl.ds(rho, per, stride=dil), :] = o_g[rho, :, r * HEAD_DIM:(r + 1) * HEAD_DIM]
            nat_l[pl.ds(rho, per, stride=dil), :] = l_g[rho]
        outs.append([nat_o[r] for r in range(n_rep)])
        lses.append(nat_l[...])
    top = functools.reduce(jnp.maximum, lses)
    weights = [jnp.exp(l - top) for l in lses]
    den = functools.reduce(lambda x, y: x + y, weights)
    weights = [w / den for w in weights]
    for r in range(n_rep):
        mix = weights[0][:, r:r + 1] * outs[0][r]
        for w, o in zip(weights[1:], outs[1:]):
            mix = mix + w[:, r:r + 1] * o[r]
        o_ref[0, :, r * HEAD_DIM:(r + 1) * HEAD_DIM] = mix.astype(o_ref.dtype)


def dilated_combine(outs, lses, dils, batch, seqlen):
    n_rep = N_HEADS // DIL_KV_HEADS
    tile = COMBINE_TILE
    qw = n_rep * HEAD_DIM
    in_specs, scratch = [], []
    for dil in dils:
        in_specs.append(pl.BlockSpec((None, dil, tile // dil, qw), lambda b, g, i: (b, 0, i, g)))
    for dil in dils:
        in_specs.append(pl.BlockSpec((None, None, dil, tile // dil, LANES), lambda b, g, i: (b, g, 0, i, 0)))
        if dil > 1:
            scratch += [pltpu.VMEM((n_rep, tile, HEAD_DIM), F32), pltpu.VMEM((tile, LANES), F32)]
    return pl.pallas_call(
        functools.partial(_dil_combine_kernel, n_rep=n_rep, dils=tuple(dils), tile=tile),
        grid=(batch, DIL_KV_HEADS, seqlen // tile),
        in_specs=in_specs,
        out_specs=pl.BlockSpec((1, tile, qw), lambda b, g, i: (b, i, g)),
        out_shape=jax.ShapeDtypeStruct((batch, seqlen, ATTN_WIDTH), BF16),
        scratch_shapes=scratch,
        compiler_params=_params(("parallel", "parallel", "parallel")),
        name="dilated_combine",
    )(*outs, *lses)


def nsa_mixer(act, h, w_in, j, cmp_pos, k_w1, k_w2, v_w1, v_w2, w_out, next_gain, tables, batch, seqlen):
    kvw = NSA_KV_HEADS * HEAD_DIM
    c0 = ATTN_WIDTH
    tokens = batch * seqlen
    q = matmul(act,w_in, j, 0, c0, BF16)
    kcvc = matmul(act,w_in, j, c0, 2 * kvw, F32)
    kvsw = matmul(act,w_in, j, c0 + 2 * kvw, 4 * kvw, BF16)
    w_gate = jnp.pad(w_in[:, :, c0 + 6 * kvw:], ((0, 0), (0, 0), (0, LANES - 3 * N_HEADS)))
    gates_t = matmul(act, w_gate, j, 0, LANES, F32, transposed=True)
    kcmp, vcmp = nsa_compress(kcvc, cmp_pos, k_w1, k_w2, v_w1, v_w2, batch, seqlen)
    o, sel = nsa_cmp_attention(q, kcmp, vcmp, tables["cmp"], gates_t, batch, seqlen)
    o = nsa_selected_attention(q, kvsw, 0, kvw, sel, tables["causal"], gates_t, o, batch, seqlen)
    o = banded_attention(q.reshape(batch, 1, seqlen, c0), 0, kvsw.reshape(batch, 1, seqlen, 4 * kvw),
                         2 * kvw, 3 * kvw, tables["nsa_win"], batch=batch, seqlen=seqlen,
                         n_kv=NSA_KV_HEADS, dil=1, max_dist=NSA_WINDOW - 1, out_dtype=BF16,
                         gated=(gates_t, 2, o.reshape(batch, 1, seqlen, ATTN_WIDTH)))
    return matmul_residual(o.reshape(tokens, ATTN_WIDTH), w_out, j, h, 1.0, next_gain)


def dilated_mixer(act, h, w_in, j, w_out, next_gain, tables, batch, seqlen):
    kvw = DIL_KV_HEADS * HEAD_DIM
    group = ATTN_WIDTH + 2 * kvw
    outs, lses, dils = [], [], []
    for gi, (window, dil) in enumerate(DIL_PAIRS):
        proj = matmul(act,w_in, j, gi * group, group, BF16, dil=dil, batch=batch)
        proj = proj.reshape(batch, dil, seqlen // dil, group)
        o, lse = banded_attention(proj, 0, proj, ATTN_WIDTH, ATTN_WIDTH + kvw, tables["dil%d" % dil],
                                  batch=batch, seqlen=seqlen, n_kv=DIL_KV_HEADS, dil=dil,
                                  max_dist=window // dil, want_lse=True)
        outs.append(o)
        lses.append(lse)
        dils.append(dil)
    o = dilated_combine(outs, lses, dils, batch, seqlen)
    return matmul_residual(o.reshape(batch * seqlen, ATTN_WIDTH), w_out, j, h, 1.0, next_gain)


def moba_mixer(act, h, w_in, j, w_out, next_gain, tables, batch, seqlen):
    q = matmul(act,w_in, j, 0, ATTN_WIDTH, BF16)
    kv = matmul(act,w_in, j, ATTN_WIDTH, 2 * MOBA_KV_HEADS * HEAD_DIM, F32)
    o = moba_attention(q, kv, tables["causal"], batch, seqlen)
    return matmul_residual(o.reshape(batch * seqlen, ATTN_WIDTH), w_out, j, h, 1.0, next_gain)


def swa_mixer(act, h, w_in, j, sinks, w_out, next_gain, tables, batch, seqlen):
    kvw = SWA_KV_HEADS * HEAD_DIM
    width = ATTN_WIDTH + 2 * kvw
    proj = matmul(act,w_in, j, 0, width, BF16).reshape(batch, 1, seqlen, width)
    sink_row = jnp.repeat(sinks, math.gcd(seqlen, BAND_BLOCK))[None, :]
    o = banded_attention(proj, 0, proj, ATTN_WIDTH, ATTN_WIDTH + kvw, tables["swa"], batch=batch,
                         seqlen=seqlen, n_kv=SWA_KV_HEADS, dil=1, max_dist=SWA_WINDOW - 1, sink_row=sink_row,
                         out_dtype=BF16)
    return matmul_residual(o.reshape(batch * seqlen, ATTN_WIDTH), w_out, j, h, 1.0, next_gain)


def _band_table_for(rel_table, seqlen, dil, max_dist):
    sub = seqlen // dil
    tq = math.gcd(sub, BAND_BLOCK)
    n_prev = min(-(-max_dist // tq), sub // tq - 1)
    return band_bias_table(rel_table, tq, (n_prev + 1) * tq, n_prev * tq, max_dist, dil)


def kernel(x, rel_table, ffn1_norm, ffn1_w_gate, ffn1_w_up, ffn1_w_down, mix_norm, ffn2_norm, ffn2_w_gate, ffn2_w_up, ffn2_w_down, final_norm, nsa_w_in, nsa_cmp_pos, nsa_cmp_k_w1, nsa_cmp_k_w2, nsa_cmp_v_w1, nsa_cmp_v_w2, nsa_w_out, dil_w_in, dil_w_out, moba_w_in, moba_w_out, swa_w_in, swa_sinks, swa_w_out):
    batch, seqlen, d_model = x.shape
    depth = ffn1_norm.shape[0]
    n_mixers = 4
    h = x.reshape(batch * seqlen, d_model)

    tables = {
        "causal": causal_bias_table(rel_table, SEL_TILE, seqlen // SEL_TILE),
        "cmp": cmp_bias_table(rel_table, seqlen, (seqlen - NSA_CMP_LEN) // NSA_CMP_STRIDE + 1),
        "nsa_win": _band_table_for(rel_table, seqlen, 1, NSA_WINDOW - 1),
        "swa": _band_table_for(rel_table, seqlen, 1, SWA_WINDOW - 1),
    }
    for window, dil in DIL_PAIRS:
        tables["dil%d" % dil] = _band_table_for(rel_table, seqlen, dil, window // dil)

    act = norm_prep(h, ffn1_norm[0])
    for i in range(depth):
        h, act = ffn_half_step(h, act, ffn1_w_gate, ffn1_w_up, ffn1_w_down, i, mix_norm[i])
        m, j = i % n_mixers, i // n_mixers
        if m == 0:
            h, act = nsa_mixer(act, h, nsa_w_in, j, nsa_cmp_pos[j], nsa_cmp_k_w1[j], nsa_cmp_k_w2[j],
                               nsa_cmp_v_w1[j], nsa_cmp_v_w2[j], nsa_w_out, ffn2_norm[i], tables, batch, seqlen)
        elif m == 1:
            h, act = dilated_mixer(act, h, dil_w_in, j, dil_w_out, ffn2_norm[i], tables, batch, seqlen)
        elif m == 2:
            h, act = moba_mixer(act, h, moba_w_in, j, moba_w_out, ffn2_norm[i], tables, batch, seqlen)
        else:
            h, act = swa_mixer(act, h, swa_w_in, j, swa_sinks[j], swa_w_out, ffn2_norm[i], tables, batch, seqlen)
        next_gain = ffn1_norm[i + 1] if i + 1 < depth else None
        h, act = ffn_half_step(h, act, ffn2_w_gate, ffn2_w_up, ffn2_w_down, i, next_gain)
    return rms_norm(h, final_norm, x.dtype).reshape(batch, seqlen, d_model)
```

```python
import functools
import math

import numpy as np
import jax
import jax.numpy as jnp
from jax import lax
from jax.experimental import pallas as pl
from jax.experimental.pallas import tpu as pltpu

HEAD_DIM = 128
N_HEADS = 32
ATTN_WIDTH = N_HEADS * HEAD_DIM
RMS_EPS = 1e-6
REL_BUCKETS = 32
REL_MAX_DIST = 2048
BAND_BLOCK = 128
NSA_KV_HEADS = 4
NSA_CMP_LEN = 32
NSA_CMP_STRIDE = 16
NSA_SEL_LEN = 64
NSA_SEL_TOPN = 16
NSA_WINDOW = 512
DIL_PAIRS = ((128, 1), (512, 4), (2048, 16))
DIL_KV_HEADS = 8
MOBA_BLOCK = 256
MOBA_TOPK = 3
MOBA_KV_HEADS = 8
SWA_WINDOW = 128
SWA_KV_HEADS = 4
ATTN_SCALE = HEAD_DIM ** -0.5
LOG2E = 1.0 / math.log(2.0)
LN2 = math.log(2.0)
SCORE_SCALE = ATTN_SCALE * LOG2E
NEG_INF = -1e30
MASKED_BELOW = -5e29
TINY = 1e-20
FORCED_SCORE = 1e9

LANES = 128
SUBLANES = 8
BF16_ROWS = 16
SEL_TILE = 256
COMBINE_TILE = 512
BANDED_ROWS_PER_STEP = 4096
TABLE_HEADS_PER_STEP = 8
VMEM_LIMIT = 56 * 1024 * 1024

F32 = jnp.float32
BF16 = jnp.bfloat16


def _params(semantics):
    return pltpu.CompilerParams(dimension_semantics=semantics, vmem_limit_bytes=VMEM_LIMIT)


def _dot_nt(a, b):
    return lax.dot_general(a, b, (((1,), (1,)), ((), ())), preferred_element_type=F32)


def _dot_tn(a, b):
    return lax.dot_general(a, b, (((0,), (0,)), ((), ())), preferred_element_type=F32)


def _stack_heads(q, n_heads):
    return jnp.concatenate([q[:, r * HEAD_DIM:(r + 1) * HEAD_DIM] for r in range(n_heads)], axis=0)


def _store_heads(o_ref, o_t, n_heads, tq, col0=0, mix=None):
    for r in range(n_heads):
        cols = slice(col0 + r * HEAD_DIM, col0 + (r + 1) * HEAD_DIM)
        blk = o_t[:, r * tq:(r + 1) * tq]
        if mix is not None:
            gate, branch, prev_ref = mix
            row = 3 * (col0 // HEAD_DIM + r) + branch
            blk = blk * gate[row:row + 1, :]
        val = blk.T
        if mix is not None and mix[2] is not None:
            val = mix[2][0, :, cols] + val
        o_ref[0, :, cols] = val.astype(o_ref.dtype)


def _rms_kernel(x_ref, g_ref, o_ref):
    x = x_ref[...]
    y = x * lax.rsqrt(jnp.mean(x * x, axis=-1, keepdims=True) + RMS_EPS)
    o_ref[...] = (y * g_ref[...]).astype(o_ref.dtype)


def rms_norm(x, gain, out_dtype):
    m, d = x.shape
    tm = 256
    return pl.pallas_call(
        _rms_kernel,
        grid=(m // tm,),
        in_specs=[pl.BlockSpec((tm, d), lambda i: (i, 0)), pl.BlockSpec((1, d), lambda i: (0, 0))],
        out_specs=pl.BlockSpec((tm, d), lambda i: (i, 0)),
        out_shape=jax.ShapeDtypeStruct((m, d), out_dtype),
        compiler_params=_params(("parallel",)),
        name="rms_norm",
    )(x, gain.reshape(1, d))


def _norm_prep_kernel(x_ref, g_ref, a_ref, ssq_ref):
    x = x_ref[...]
    a_ref[...] = (x * g_ref[...]).astype(a_ref.dtype)
    ssq_ref[...] = jnp.broadcast_to(jnp.sum(x * x, axis=-1, keepdims=True), ssq_ref.shape)


def norm_prep(x, gain):
    m, d = x.shape
    tm = 256
    return pl.pallas_call(
        _norm_prep_kernel,
        grid=(m // tm,),
        in_specs=[pl.BlockSpec((tm, d), lambda i: (i, 0)), pl.BlockSpec((1, d), lambda i: (0, 0))],
        out_specs=[pl.BlockSpec((tm, d), lambda i: (i, 0)), pl.BlockSpec((tm, LANES), lambda i: (i, 0))],
        out_shape=[jax.ShapeDtypeStruct((m, d), BF16), jax.ShapeDtypeStruct((m, LANES), F32)],
        compiler_params=_params(("parallel",)),
        name="norm_prep",
    )(x, gain.reshape(1, d))


def _row_scale(ssq_ref, d):
    return lax.rsqrt(ssq_ref[:, 0:1] * (1.0 / d) + RMS_EPS)


MM_VMEM_BUDGET = 50 * 1024 * 1024


def _mm_kernel(a_ref, ssq_ref, b_ref, o_ref, *scratch, dil, transposed):
    res = jnp.dot(a_ref[...], b_ref[...].astype(BF16), preferred_element_type=F32)
    res = res * _row_scale(ssq_ref, a_ref.shape[1])
    if transposed:
        o_ref[...] = res.T.astype(o_ref.dtype)
        return
    if dil == 1:
        o_ref[...] = res.astype(o_ref.dtype)
        return
    scr_ref, = scratch
    tm, tn = res.shape
    for s in range(tn // LANES):
        scr_ref[s] = res[:, s * LANES:(s + 1) * LANES]
    for rho in range(dil):
        for s in range(tn // LANES):
            o_ref[rho, :, s * LANES:(s + 1) * LANES] = (
                scr_ref[s, pl.ds(rho, tm // dil, stride=dil), :].astype(o_ref.dtype))


def _mm_res_kernel(a_ref, b_ref, r_ref, *rest, scale, emit_next):
    if emit_next:
        g_ref, o_ref, an_ref, ssq_ref = rest

        @pl.when(pl.program_id(1) == 0)
        def _():
            ssq_ref[...] = jnp.zeros(ssq_ref.shape, F32)
    else:
        o_ref, = rest
    h = r_ref[...] + scale * jnp.dot(a_ref[...], b_ref[...].astype(BF16), preferred_element_type=F32)
    o_ref[...] = h
    if emit_next:
        an_ref[...] = (h * g_ref[...]).astype(an_ref.dtype)
        ssq_ref[...] += jnp.sum(h * h, axis=-1, keepdims=True)


def _mm_swiglu_kernel(a_ref, ssq_ref, bg_ref, bu_ref, o_ref):
    a = a_ref[...]
    r = _row_scale(ssq_ref, a_ref.shape[1])
    gate = jnp.dot(a, bg_ref[...].astype(BF16), preferred_element_type=F32) * r
    up = jnp.dot(a, bu_ref[...].astype(BF16), preferred_element_type=F32) * r
    o_ref[...] = (jax.nn.silu(gate) * up).astype(o_ref.dtype)


def _mm_tiles(m, k, n, n_weights, io_bytes):
    tm = min(m, 1024)
    if n < LANES:
        return tm, n
    for tn in (512, 256, 128):
        need = (2 * tm * k * 2 + n_weights * (2 * k * tn * 4 + k * tn * 2) + 2 * tm * tn * io_bytes
                + n_weights * tm * tn * 4)
        if n % tn == 0 and need <= MM_VMEM_BUDGET:
            return tm, tn
    raise ValueError("no matmul tile fits VMEM")


def _weight_spec(w, layer, col_off, tn):
    assert col_off % tn == 0
    return pl.BlockSpec((None, w.shape[1], tn), lambda i, j: (layer, 0, col_off // tn + j))


def matmul(act, w, layer, col_off, n, out_dtype, *, dil=1, batch=1, transposed=False):
    a, ssq = act
    m, k = a.shape
    tm, tn = _mm_tiles(m, k, n, 1, jnp.dtype(out_dtype).itemsize)
    in_specs = [pl.BlockSpec((tm, k), lambda i, j: (i, 0)), pl.BlockSpec((tm, LANES), lambda i, j: (i, 0)),
                _weight_spec(w, layer, col_off, tn)]
    if transposed:
        assert dil == 1 and tn % LANES == 0
        out_specs = pl.BlockSpec((tn, tm), lambda i, j: (j, i))
        out_shape = jax.ShapeDtypeStruct((n, m), out_dtype)
        scratch = []
    elif dil == 1:
        out_specs = pl.BlockSpec((tm, tn), lambda i, j: (i, j))
        out_shape = jax.ShapeDtypeStruct((m, n), out_dtype)
        scratch = []
    else:
        per_batch = m // batch // tm
        assert m % (batch * tm) == 0 and tm % (dil * BF16_ROWS) == 0 and tn % LANES == 0
        out_specs = pl.BlockSpec((None, dil, tm // dil, tn), lambda i, j: (i // per_batch, 0, i % per_batch, j))
        out_shape = jax.ShapeDtypeStruct((batch, dil, m // batch // dil, n), out_dtype)
        scratch = [pltpu.VMEM((tn // LANES, tm, LANES), F32)]
    return pl.pallas_call(
        functools.partial(_mm_kernel, dil=dil, transposed=transposed),
        grid=(m // tm, n // tn),
        in_specs=in_specs,
        out_specs=out_specs,
        out_shape=out_shape,
        scratch_shapes=scratch,
        compiler_params=_params(("parallel", "arbitrary")),
        name="matmul",
    )(a, ssq, w)


def matmul_residual(a, w, layer, res, scale, next_gain=None):
    m, k = a.shape
    n = w.shape[2]
    emit_next = next_gain is not None
    tm, tn = _mm_tiles(m, k, n, 1, 10 if emit_next else 8)
    tile = pl.BlockSpec((tm, tn), lambda i, j: (i, j))
    in_specs = [pl.BlockSpec((tm, k), lambda i, j: (i, 0)), _weight_spec(w, layer, 0, tn), tile]
    args = [a, w, res]
    out_specs, out_shape = tile, jax.ShapeDtypeStruct((m, n), F32)
    if emit_next:
        in_specs.append(pl.BlockSpec((1, tn), lambda i, j: (0, j)))
        args.append(next_gain.reshape(1, n))
        out_specs = [tile, tile, pl.BlockSpec((tm, LANES), lambda i, j: (i, 0))]
        out_shape = [out_shape, jax.ShapeDtypeStruct((m, n), BF16), jax.ShapeDtypeStruct((m, LANES), F32)]
    outs = pl.pallas_call(
        functools.partial(_mm_res_kernel, scale=scale, emit_next=emit_next),
        grid=(m // tm, n // tn),
        in_specs=in_specs,
        out_specs=out_specs,
        out_shape=out_shape,
        compiler_params=_params(("parallel", "arbitrary")),
        name="matmul_residual",
    )(*args)
    return (outs[0], (outs[1], outs[2])) if emit_next else (outs, None)


def matmul_swiglu(act, wg, wu, layer):
    a, ssq = act
    m, k = a.shape
    n = wg.shape[2]
    tm, tn = _mm_tiles(m, k, n, 2, 2)
    return pl.pallas_call(
        _mm_swiglu_kernel,
        grid=(m // tm, n // tn),
        in_specs=[pl.BlockSpec((tm, k), lambda i, j: (i, 0)),
                  pl.BlockSpec((tm, LANES), lambda i, j: (i, 0)),
                  _weight_spec(wg, layer, 0, tn),
                  _weight_spec(wu, layer, 0, tn)],
        out_specs=pl.BlockSpec((tm, tn), lambda i, j: (i, j)),
        out_shape=jax.ShapeDtypeStruct((m, n), BF16),
        compiler_params=_params(("parallel", "arbitrary")),
        name="matmul_swiglu",
    )(a, ssq, wg, wu)


def ffn_half_step(h, act, w_gate, w_up, w_down, layer, next_gain):
    hidden = matmul_swiglu(act, w_gate, w_up, layer)
    return matmul_residual(hidden, w_down, layer, h, 0.5, next_gain)


def _t5_bucket(dist):
    n = jnp.maximum(dist, 0)
    exact = REL_BUCKETS // 2
    nf = jnp.maximum(n, 1).astype(F32)
    large = exact + (jnp.log(nf / exact) * ((REL_BUCKETS - exact) / math.log(REL_MAX_DIST / exact))).astype(jnp.int32)
    return jnp.where(n < exact, n, jnp.minimum(large, REL_BUCKETS - 1))


def _bias_table_kernel(tab_ref, o_ref, bucket_ref, *, base0, base_step, key_stride, max_dist, n_valid_keys,
                       dist_scale, heads_per_step):
    blk = pl.program_id(0)
    hg = pl.program_id(1)
    n_qry = bucket_ref.shape[1]

    @pl.when(hg == 0)
    def _():
        n_keys, n_qry = bucket_ref.shape
        key = lax.broadcasted_iota(jnp.int32, (n_keys, n_qry), 0)
        qry = lax.broadcasted_iota(jnp.int32, (n_keys, n_qry), 1)
        dist = base0 + blk * base_step + qry - key * key_stride
        valid = (dist >= 0) & (dist <= max_dist) & (key < n_valid_keys)
        bucket_ref[...] = jnp.where(valid, _t5_bucket(dist * dist_scale), -1)

    bucket = bucket_ref[...]
    bits = [(bucket & (1 << t)) != 0 for t in range(REL_BUCKETS.bit_length() - 1)]
    for hh in range(heads_per_step):
        h = hg * heads_per_step + hh
        level = [tab_ref[b, h] * LOG2E for b in range(REL_BUCKETS)]
        for odd in bits:
            level = [jnp.where(odd, level[2 * t + 1], level[2 * t]) for t in range(len(level) // 2)]
        tile = jnp.where(bucket >= 0, level[0], NEG_INF)
        if len(o_ref.shape) == 2:
            o_ref[:, hh * n_qry:(hh + 1) * n_qry] = tile
        else:
            o_ref[0, :, hh * n_qry:(hh + 1) * n_qry] = tile


def band_bias_table(rel_table, tq, span, pad, max_dist, dist_scale):
    kern = functools.partial(_bias_table_kernel, base0=pad, base_step=0, key_stride=1, max_dist=max_dist,
                             n_valid_keys=span, dist_scale=dist_scale, heads_per_step=TABLE_HEADS_PER_STEP)
    return pl.pallas_call(
        kern,
        grid=(1, N_HEADS // TABLE_HEADS_PER_STEP),
        in_specs=[pl.BlockSpec(memory_space=pltpu.SMEM)],
        out_specs=pl.BlockSpec((span, TABLE_HEADS_PER_STEP * tq), lambda j, h: (0, h)),
        out_shape=jax.ShapeDtypeStruct((span, N_HEADS * tq), F32),
        scratch_shapes=[pltpu.VMEM((span, tq), jnp.int32)],
        compiler_params=_params(("parallel", "arbitrary")),
        name="band_bias_table",
    )(rel_table)


def causal_bias_table(rel_table, tile, n_cls):
    kern = functools.partial(_bias_table_kernel, base0=0, base_step=tile, key_stride=1, max_dist=2 ** 30,
                             n_valid_keys=tile, dist_scale=1, heads_per_step=1)
    return pl.pallas_call(
        kern,
        grid=(n_cls, N_HEADS),
        in_specs=[pl.BlockSpec(memory_space=pltpu.SMEM)],
        out_specs=pl.BlockSpec((1, tile, tile), lambda c, h: (c, 0, h)),
        out_shape=jax.ShapeDtypeStruct((n_cls, tile, N_HEADS * tile), F32),
        scratch_shapes=[pltpu.VMEM((tile, tile), jnp.int32)],
        compiler_params=_params(("parallel", "arbitrary")),
        name="causal_bias_table",
    )(rel_table)


def cmp_bias_table(rel_table, seqlen, n_cmp):
    tq = SEL_TILE
    kern = functools.partial(_bias_table_kernel, base0=-(NSA_CMP_LEN - 1), base_step=tq,
                             key_stride=NSA_CMP_STRIDE, max_dist=2 ** 30, n_valid_keys=n_cmp, dist_scale=1,
                             heads_per_step=TABLE_HEADS_PER_STEP)
    return pl.pallas_call(
        kern,
        grid=(seqlen // tq, N_HEADS // TABLE_HEADS_PER_STEP),
        in_specs=[pl.BlockSpec(memory_space=pltpu.SMEM)],
        out_specs=pl.BlockSpec((1, LANES, TABLE_HEADS_PER_STEP * tq), lambda i, h: (i, 0, h)),
        out_shape=jax.ShapeDtypeStruct((seqlen // tq, LANES, N_HEADS * tq), F32),
        scratch_shapes=[pltpu.VMEM((LANES, tq), jnp.int32)],
        compiler_params=_params(("parallel", "arbitrary")),
        name="cmp_bias_table",
    )(rel_table)


def _banded_kernel(*refs, n_rep, n_grp, tq, n_prev, seq, has_sink, want_lse, gate_branch):
    q_ref, k_ref, v_ref, bias_ref = refs[:4]
    pos = 4
    sink_ref = None
    if has_sink:
        sink_ref = refs[pos]
        pos += 1
    mix = None
    if gate_branch is not None:
        mix = (jax.nn.sigmoid(refs[pos][...]), gate_branch, refs[pos + 1])
        pos += 2
    o_ref = refs[pos]
    pos += 1
    lse_ref = None
    if want_lse:
        lse_ref = refs[pos]
        pos += 1
    kpad_ref, vpad_ref = refs[pos:pos + 2]

    i = pl.program_id(3)
    pad = n_prev * tq
    span = pad + tq
    rows = n_rep * tq
    qw = n_rep * HEAD_DIM

    @pl.when(i == 0)
    def _():
        if pad:
            kpad_ref[0:pad, :] = jnp.zeros((pad, n_grp * HEAD_DIM), BF16)
            vpad_ref[0:pad, :] = jnp.zeros((pad, n_grp * HEAD_DIM), BF16)
        kpad_ref[pad:pad + seq, :] = k_ref[0]
        vpad_ref[pad:pad + seq, :] = v_ref[0]

    start = pl.multiple_of(i * tq, tq)

    def attend(gg, span_has_padding):
        kv_cols = slice(gg * HEAD_DIM, (gg + 1) * HEAD_DIM)
        row_cols = slice(gg * rows, (gg + 1) * rows)
        ks = kpad_ref[pl.ds(start, span), kv_cols]
        vs = vpad_ref[pl.ds(start, span), kv_cols]
        qs = _stack_heads(q_ref[0, :, gg * qw:(gg + 1) * qw], n_rep)
        lt = _dot_nt(ks, qs) * SCORE_SCALE + bias_ref[:, row_cols]
        if span_has_padding:
            key = lax.broadcasted_iota(jnp.int32, (span, rows), 0)
            lt = jnp.where(key >= pad - i * tq, lt, NEG_INF)
        m = jnp.max(lt, axis=0, keepdims=True)
        sink = None
        if has_sink:
            sink = sink_ref[:, row_cols] * LOG2E
            m = jnp.maximum(m, sink)
        p = jnp.exp2(lt - m)
        s = jnp.sum(p, axis=0, keepdims=True)
        if has_sink:
            s = s + jnp.exp2(sink - m)
        o_t = _dot_tn(vs, p.astype(BF16))
        s = jnp.maximum(s, TINY)
        _store_heads(o_ref, o_t / s, n_rep, tq, gg * qw, mix)
        if want_lse:
            lse = (m + jnp.log2(s)) * LN2
            head = lax.broadcasted_iota(jnp.int32, (LANES, tq), 0)
            tile = jnp.zeros((LANES, tq), F32)
            for r in range(n_rep):
                tile = jnp.where(head == r, lse[:, r * tq:(r + 1) * tq], tile)
            lse_ref[gg, 0] = tile.T

    if pad:
        @pl.when(i < n_prev)
        def _():
            for gg in range(n_grp):
                attend(gg, True)

        @pl.when(i >= n_prev)
        def _():
            for gg in range(n_grp):
                attend(gg, False)
    else:
        for gg in range(n_grp):
            attend(gg, False)


def banded_attention(q_arr, q_off, kv_arr, k_off, v_off, bias, *, batch, seqlen, n_kv, dil,
                     max_dist, sink_row=None, want_lse=False, out_dtype=F32, gated=None):
    n_rep = N_HEADS // n_kv
    sub = seqlen // dil
    tq = math.gcd(sub, BAND_BLOCK)
    n_blk = sub // tq
    n_prev = min(-(-max_dist // tq), n_blk - 1)
    span = (n_prev + 1) * tq
    qw = n_rep * HEAD_DIM
    rows = n_rep * tq
    n_grp = max(c for c in (8, 4, 2, 1) if c * rows <= BANDED_ROWS_PER_STEP and n_kv % c == 0)
    gqw, gkw = n_grp * qw, n_grp * HEAD_DIM
    assert q_arr.shape[:3] == kv_arr.shape[:3] == (batch, dil, sub)
    assert q_off % gqw == 0 and k_off % gkw == 0 and v_off % gkw == 0 and bias.shape == (span, N_HEADS * tq)

    in_specs = [
        pl.BlockSpec((None, 1, tq, gqw), lambda b, rho, g, i: (b, rho, i, q_off // gqw + g)),
        pl.BlockSpec((None, 1, sub, gkw), lambda b, rho, g, i: (b, rho, 0, k_off // gkw + g)),
        pl.BlockSpec((None, 1, sub, gkw), lambda b, rho, g, i: (b, rho, 0, v_off // gkw + g)),
        pl.BlockSpec((span, n_grp * rows), lambda b, rho, g, i: (0, g)),
    ]
    args = [q_arr, kv_arr, kv_arr, bias]
    if sink_row is not None:
        in_specs.append(pl.BlockSpec((1, n_grp * rows), lambda b, rho, g, i: (0, g)))
        args.append(sink_row)
    if gated is not None:
        gates_t, gate_branch, prev = gated
        assert dil == 1
        in_specs.append(pl.BlockSpec((3 * n_grp * n_rep, tq), lambda b, rho, g, i: (g, b * n_blk + i)))
        in_specs.append(pl.BlockSpec((None, 1, tq, gqw), lambda b, rho, g, i: (b, rho, i, g)))
        args += [gates_t, prev]
    out_specs = [pl.BlockSpec((None, 1, tq, gqw), lambda b, rho, g, i: (b, rho, i, g))]
    out_shape = [jax.ShapeDtypeStruct((batch, dil, sub, ATTN_WIDTH), out_dtype)]
    if want_lse:
        out_specs.append(pl.BlockSpec((None, n_grp, 1, tq, LANES), lambda b, rho, g, i: (b, g, rho, i, 0)))
        out_shape.append(jax.ShapeDtypeStruct((batch, n_kv, dil, sub, LANES), F32))
    kern = functools.partial(_banded_kernel, n_rep=n_rep, n_grp=n_grp, tq=tq, n_prev=n_prev, seq=sub,
                             has_sink=sink_row is not None, want_lse=want_lse,
                             gate_branch=None if gated is None else gated[1])
    outs = pl.pallas_call(
        kern,
        grid=(batch, dil, n_kv // n_grp, n_blk),
        in_specs=in_specs,
        out_specs=out_specs,
        out_shape=out_shape,
        scratch_shapes=[pltpu.VMEM((n_prev * tq + sub, gkw), BF16),
                        pltpu.VMEM((n_prev * tq + sub, gkw), BF16)],
        compiler_params=_params(("parallel", "parallel", "parallel", "arbitrary")),
        name="banded_attention",
    )(*args)
    return tuple(outs) if want_lse else outs[0]


def _softmax_pv(lt, v, o_ref, n_heads, tq, col0=0, mix=None):
    m = jnp.max(lt, axis=0, keepdims=True)
    p = jnp.exp2(lt - m)
    s = jnp.maximum(jnp.sum(p, axis=0, keepdims=True), TINY)
    _store_heads(o_ref, _dot_tn(v, p.astype(BF16)) / s, n_heads, tq, col0, mix)


def _first_rank(score, n_cand):
    idx = lax.broadcasted_iota(jnp.int32, score.shape, 0)
    rank = jnp.zeros(score.shape, F32)
    for jp in range(n_cand):
        other = score[jp:jp + 1, :]
        ahead = jnp.where(other > score, 1.0, jnp.where(other == score, jnp.where(idx > jp, 1.0, 0.0), 0.0))
        rank = rank + ahead
    return rank


def _nsa_cmp_kernel(x_ref, pos_ref, w1k_ref, w2k_ref, w1v_ref, w2v_ref, ko_ref, vo_ref):
    width = 2 * NSA_KV_HEADS * HEAD_DIM
    for kv, (w1_ref, w2_ref, o_ref) in enumerate(((w1k_ref, w2k_ref, ko_ref), (w1v_ref, w2v_ref, vo_ref))):
        for g in range(NSA_KV_HEADS):
            off = kv * NSA_KV_HEADS * HEAD_DIM + g * HEAD_DIM
            chunk = jnp.concatenate(
                [x_ref[0, :, l * width + off:l * width + off + HEAD_DIM] for l in range(NSA_CMP_STRIDE)], axis=1)
            first = jnp.dot((chunk + pos_ref[0:1, :]).astype(BF16), w1_ref[0], preferred_element_type=F32)
            second = jnp.dot((chunk + pos_ref[1:2, :]).astype(BF16), w1_ref[1], preferred_element_type=F32)
            hidden = jax.nn.gelu(first + pltpu.roll(second, second.shape[0] - 1, axis=0))
            o_ref[0, g] = jnp.dot(hidden.astype(BF16), w2_ref[...], preferred_element_type=F32).astype(o_ref.dtype)


def nsa_compress(kcvc, cmp_pos, k_w1, k_w2, v_w1, v_w2, batch, seqlen):
    n_chunk = seqlen // NSA_CMP_STRIDE
    width = 2 * NSA_KV_HEADS * HEAD_DIM
    half = NSA_CMP_STRIDE * HEAD_DIM
    x = kcvc.reshape(batch, n_chunk, NSA_CMP_STRIDE * width)
    out = jax.ShapeDtypeStruct((batch, NSA_KV_HEADS, n_chunk, HEAD_DIM), BF16)
    full = lambda shape: pl.BlockSpec(shape, lambda b: (0,) * len(shape))
    return pl.pallas_call(
        _nsa_cmp_kernel,
        grid=(batch,),
        in_specs=[pl.BlockSpec((1, n_chunk, NSA_CMP_STRIDE * width), lambda b: (b, 0, 0)),
                  full((2, half)), full((2, half, HEAD_DIM)), full((HEAD_DIM, HEAD_DIM)),
                  full((2, half, HEAD_DIM)), full((HEAD_DIM, HEAD_DIM))],
        out_specs=[pl.BlockSpec((1, NSA_KV_HEADS, n_chunk, HEAD_DIM), lambda b: (b, 0, 0, 0))] * 2,
        out_shape=[out, out],
        compiler_params=_params(("parallel",)),
        name="nsa_compress",
    )(x, cmp_pos.reshape(2, half), k_w1.reshape(2, half, HEAD_DIM).astype(BF16), k_w2.astype(BF16),
      v_w1.reshape(2, half, HEAD_DIM).astype(BF16), v_w2.astype(BF16))


def _nsa_cmp_attn_kernel(q_ref, kc_ref, vc_ref, bias_ref, c2s_ref, g_ref, o_ref, sel_ref, *, n_rep, tq, n_sel_blk):
    i = pl.program_id(2)
    qs = _stack_heads(q_ref[0], n_rep)
    bias = bias_ref[0]
    valid = bias > MASKED_BELOW
    lt = jnp.where(valid, _dot_nt(kc_ref[0, 0], qs) * SCORE_SCALE +bias, NEG_INF)
    m = jnp.max(lt, axis=0, keepdims=True)
    p = jnp.where(valid, jnp.exp2(lt - m), 0.0)
    s = jnp.sum(p, axis=0, keepdims=True)
    p_cmp = p / jnp.maximum(s, TINY)
    _store_heads(o_ref, _dot_tn(vc_ref[0, 0], p_cmp.astype(BF16)), n_rep, tq,
                 mix=(jax.nn.sigmoid(g_ref[...]), 0, None))

    p_sum = p_cmp[:, 0:tq]
    for r in range(1, n_rep):
        p_sum = p_sum + p_cmp[:, r * tq:(r + 1) * tq]
    imp = jnp.dot(c2s_ref[...], p_sum.astype(BF16), preferred_element_type=F32)[0:n_sel_blk]
    blk = lax.broadcasted_iota(jnp.int32, (n_sel_blk, tq), 0)
    tpos = i * tq + lax.broadcasted_iota(jnp.int32, (n_sel_blk, tq), 1)
    cur = tpos // NSA_SEL_LEN
    forced = (blk == 0) | (blk == cur) | (blk == cur - 1)
    score = jnp.where(forced, FORCED_SCORE, jnp.where(blk * NSA_SEL_LEN <= tpos, imp, NEG_INF))
    sel_ref[0, 0] = jnp.where(_first_rank(score, n_sel_blk) < min(NSA_SEL_TOPN, n_sel_blk), 1.0, 0.0)


def nsa_cmp_attention(q, kcmp, vcmp, bias, gates_t, batch, seqlen):
    n_rep = N_HEADS // NSA_KV_HEADS
    tq = SEL_TILE
    n_tile = seqlen // tq
    qw = n_rep * HEAD_DIM
    n_sel_blk = seqlen // NSA_SEL_LEN
    n_cmp = (seqlen - NSA_CMP_LEN) // NSA_CMP_STRIDE + 1
    a, b = NSA_SEL_LEN // NSA_CMP_STRIDE, NSA_CMP_LEN // NSA_CMP_STRIDE
    w = np.zeros((LANES, LANES), np.float32)
    j = np.arange(n_sel_blk)
    for mm in range(a):
        for nn in range(b):
            ii = a * j + mm + nn - (b - 1)
            ok = (ii >= 0) & (ii < n_cmp)
            np.add.at(w, (j[ok], ii[ok]), 1.0)
    kern = functools.partial(_nsa_cmp_attn_kernel, n_rep=n_rep, tq=tq, n_sel_blk=n_sel_blk)
    return pl.pallas_call(
        kern,
        grid=(batch, NSA_KV_HEADS, seqlen // tq),
        in_specs=[pl.BlockSpec((1, tq, qw), lambda b_, g, i: (b_, i, g)),
                  pl.BlockSpec((1, 1, LANES, HEAD_DIM), lambda b_, g, i: (b_, g, 0, 0)),
                  pl.BlockSpec((1, 1, LANES, HEAD_DIM), lambda b_, g, i: (b_, g, 0, 0)),
                  pl.BlockSpec((1, LANES, n_rep * tq), lambda b_, g, i: (i, 0, g)),
                  pl.BlockSpec((LANES, LANES), lambda b_, g, i: (0, 0)),
                  pl.BlockSpec((3 * n_rep, tq), lambda b_, g, i: (g, b_ * n_tile + i))],
        out_specs=[pl.BlockSpec((1, tq, qw), lambda b_, g, i: (b_, i, g)),
                   pl.BlockSpec((1, 1, n_sel_blk, tq), lambda b_, g, i: (b_, g, 0, i))],
        out_shape=[jax.ShapeDtypeStruct((batch, seqlen, ATTN_WIDTH), F32),
                   jax.ShapeDtypeStruct((batch, NSA_KV_HEADS, n_sel_blk, seqlen), F32)],
        compiler_params=_params(("parallel", "parallel", "arbitrary")),
        name="nsa_cmp_attention",
    )(q.reshape(batch, seqlen, ATTN_WIDTH), kcmp, vcmp, bias, jnp.asarray(w, BF16), gates_t)


def _nsa_sel_kernel(q_ref, k_ref, v_ref, sel_ref, bias_ref, g_ref, prev_ref, o_ref, *, n_rep, tq, n_blk, n_pass):
    i = pl.program_id(2)
    mix = (jax.nn.sigmoid(g_ref[...]), 1, prev_ref)
    per = tq // NSA_SEL_LEN
    hpp = n_rep // n_pass
    for k in range(n_blk):
        @pl.when(i == k)
        def _(k=k):
            n_keys = (k + 1) * tq
            keys = k_ref[0, 0:n_keys, :]
            vals = v_ref[0, 0:n_keys, :]
            for part in range(n_pass):
                qs = _stack_heads(q_ref[0, :, part * hpp * HEAD_DIM:(part + 1) * hpp * HEAD_DIM], hpp)
                cols = slice(part * hpp * tq, (part + 1) * hpp * tq)
                bias = jnp.concatenate([bias_ref[k - c, :, cols] for c in range(k + 1)], axis=0)
                lt = _dot_nt(keys, qs) * SCORE_SCALE + bias
                slabs = []
                for b in range((k + 1) * per):
                    on = jnp.concatenate([sel_ref[0, 0, b:b + 1, :]] * hpp, axis=1)
                    slabs.append(jnp.where(on > 0.5, lt[b * NSA_SEL_LEN:(b + 1) * NSA_SEL_LEN], NEG_INF))
                _softmax_pv(jnp.concatenate(slabs, axis=0), vals, o_ref, hpp, tq, part * hpp * HEAD_DIM, mix)


def nsa_selected_attention(q, kv, k_off, v_off, sel, causal_bias, gates_t, prev, batch, seqlen):
    n_rep = N_HEADS // NSA_KV_HEADS
    tq = SEL_TILE
    qw = n_rep * HEAD_DIM
    n_blk = seqlen // tq
    n_sel_blk = seqlen // NSA_SEL_LEN
    ckv = kv.shape[1]
    kv3 = kv.reshape(batch, seqlen, ckv)
    kern = functools.partial(_nsa_sel_kernel, n_rep=n_rep, tq=tq, n_blk=n_blk, n_pass=2)
    rows = n_rep * tq
    return pl.pallas_call(
        kern,
        grid=(batch, NSA_KV_HEADS, n_blk),
        in_specs=[pl.BlockSpec((1, tq, qw), lambda b, g, i: (b, i, g)),
                  pl.BlockSpec((1, seqlen, HEAD_DIM), lambda b, g, i: (b, 0, k_off // HEAD_DIM + g)),
                  pl.BlockSpec((1, seqlen, HEAD_DIM), lambda b, g, i: (b, 0, v_off // HEAD_DIM + g)),
                  pl.BlockSpec((1, 1, n_sel_blk, tq), lambda b, g, i: (b, g, 0, i)),
                  pl.BlockSpec((n_blk, tq, rows), lambda b, g, i: (0, 0, g), pipeline_mode=pl.Buffered(1)),
                  pl.BlockSpec((3 * n_rep, tq), lambda b, g, i: (g, b * n_blk + i)),
                  pl.BlockSpec((1, tq, qw), lambda b, g, i: (b, i, g))],
        out_specs=pl.BlockSpec((1, tq, qw), lambda b, g, i: (b, i, g)),
        out_shape=jax.ShapeDtypeStruct((batch, seqlen, ATTN_WIDTH), F32),
        compiler_params=_params(("parallel", "parallel", "arbitrary")),
        name="nsa_selected_attention",
    )(q.reshape(batch, seqlen, ATTN_WIDTH), kv3, kv3, sel, causal_bias, gates_t, prev)


def _moba_kernel(q_ref, k_ref, v_ref, bias_ref, o_ref, kb_ref, vb_ref, km_ref, *, n_rep, tq, n_blk):
    i = pl.program_id(2)
    rows = n_rep * tq

    @pl.when(i == 0)
    def _():
        k = k_ref[0]
        kb_ref[...] = k.astype(BF16)
        vb_ref[...] = v_ref[0].astype(BF16)
        slot = lax.broadcasted_iota(jnp.int32, (BF16_ROWS, HEAD_DIM), 0)
        means = jnp.zeros((BF16_ROWS, HEAD_DIM), F32)
        for j in range(n_blk):
            means = jnp.where(slot == j, jnp.mean(k[j * tq:(j + 1) * tq], axis=0, keepdims=True), means)
        km_ref[...] = means.astype(BF16)

    qs = _stack_heads(q_ref[0], n_rep)
    gate = _dot_nt(km_ref[...], qs)[0:n_blk]
    blk = lax.broadcasted_iota(jnp.int32, (n_blk, rows), 0)
    past = blk < i
    rank = _first_rank(jnp.where(past, gate, NEG_INF), n_blk)
    chosen = jnp.where(past, jnp.where(rank < min(MOBA_TOPK, max(n_blk - 1, 1)), 1.0, 0.0), 0.0)

    for k in range(n_blk):
        @pl.when(i == k)
        def _(k=k):
            n_keys = (k + 1) * tq
            bias = jnp.concatenate([bias_ref[k - c] for c in range(k + 1)], axis=0)
            lt = _dot_nt(kb_ref[0:n_keys, :], qs) * SCORE_SCALE + bias
            parts = [jnp.where(chosen[c:c + 1, :] > 0.5, lt[c * tq:(c + 1) * tq], NEG_INF) for c in range(k)]
            parts.append(lt[k * tq:n_keys])
            _softmax_pv(jnp.concatenate(parts, axis=0), vb_ref[0:n_keys, :], o_ref, n_rep, tq)


def moba_attention(q, kv, causal_bias, batch, seqlen):
    n_rep = N_HEADS // MOBA_KV_HEADS
    tq = MOBA_BLOCK
    qw = n_rep * HEAD_DIM
    n_blk = seqlen // tq
    assert n_blk <= SUBLANES
    rows = n_rep * tq
    kv3 = kv.reshape(batch, seqlen, kv.shape[1])
    kern = functools.partial(_moba_kernel, n_rep=n_rep, tq=tq, n_blk=n_blk)
    return pl.pallas_call(
        kern,
        grid=(batch, MOBA_KV_HEADS, n_blk),
        in_specs=[pl.BlockSpec((1, tq, qw), lambda b, g, i: (b, i, g)),
                  pl.BlockSpec((1, seqlen, HEAD_DIM), lambda b, g, i: (b, 0, g)),
                  pl.BlockSpec((1, seqlen, HEAD_DIM), lambda b, g, i: (b, 0, MOBA_KV_HEADS + g)),
                  pl.BlockSpec((n_blk, tq, rows), lambda b, g, i: (0, 0, g), pipeline_mode=pl.Buffered(1))],
        out_specs=pl.BlockSpec((1, tq, qw), lambda b, g, i: (b, i, g)),
        out_shape=jax.ShapeDtypeStruct((batch, seqlen, ATTN_WIDTH), BF16),
        scratch_shapes=[pltpu.VMEM((seqlen, HEAD_DIM), BF16), pltpu.VMEM((seqlen, HEAD_DIM), BF16),
                        pltpu.VMEM((BF16_ROWS, HEAD_DIM), BF16)],
        compiler_params=_params(("parallel", "parallel", "arbitrary")),
        name="moba_attention",
    )(q.reshape(batch, seqlen, ATTN_WIDTH), kv3, kv3, causal_bias)


def _dil_combine_kernel(*refs, n_rep, dils, tile):
    n_grp = len(dils)
    o_refs, l_refs, o_ref = refs[:n_grp], refs[n_grp:2 * n_grp], refs[2 * n_grp]
    scratch = list(refs[2 * n_grp + 1:])
    outs, lses = [], []
    for o_g, l_g, dil in zip(o_refs, l_refs, dils):
        if dil == 1:
            outs.append([o_g[0, :, r * HEAD_DIM:(r + 1) * HEAD_DIM] for r in range(n_rep)])
            lses.append(l_g[0])
            continue
        nat_o, nat_l = scratch.pop(0), scratch.pop(0)
        per = tile // dil
        for rho in range(dil):
            for r in range(n_rep):
                nat_o[r, pl.ds(rho, per, stride=dil), :] = o_g[rho, :, r * HEAD_DIM:(r + 1) * HEAD_DIM]
            nat_l[pl.ds(rho, per, stride=dil), :] = l_g[rho]
        outs.append([nat_o[r] for r in range(n_rep)])
        lses.append(nat_l[...])
    top = functools.reduce(jnp.maximum, lses)
    weights = [jnp.exp(l - top) for l in lses]
    den = functools.reduce(lambda x, y: x + y, weights)
    weights = [w / den for w in weights]
    for r in range(n_rep):
        mix = weights[0][:, r:r + 1] * outs[0][r]
        for w, o in zip(weights[1:], outs[1:]):
            mix = mix + w[:, r:r + 1] * o[r]
        o_ref[0, :, r * HEAD_DIM:(r + 1) * HEAD_DIM] = mix.astype(o_ref.dtype)


def dilated_combine(outs, lses, dils, batch, seqlen):
    n_rep = N_HEADS // DIL_KV_HEADS
    tile = COMBINE_TILE
    qw = n_rep * HEAD_DIM
    in_specs, scratch = [], []
    for dil in dils:
        in_specs.append(pl.BlockSpec((None, dil, tile // dil, qw), lambda b, g, i: (b, 0, i, g)))
    for dil in dils:
        in_specs.append(pl.BlockSpec((None, None, dil, tile // dil, LANES), lambda b, g, i: (b, g, 0, i, 0)))
        if dil > 1:
            scratch += [pltpu.VMEM((n_rep, tile, HEAD_DIM), F32), pltpu.VMEM((tile, LANES), F32)]
    return pl.pallas_call(
        functools.partial(_dil_combine_kernel, n_rep=n_rep, dils=tuple(dils), tile=tile),
        grid=(batch, DIL_KV_HEADS, seqlen // tile),
        in_specs=in_specs,
        out_specs=pl.BlockSpec((1, tile, qw), lambda b, g, i: (b, i, g)),
        out_shape=jax.ShapeDtypeStruct((batch, seqlen, ATTN_WIDTH), BF16),
        scratch_shapes=scratch,
        compiler_params=_params(("parallel", "parallel", "parallel")),
        name="dilated_combine",
    )(*outs, *lses)


def nsa_mixer(act, h, w_in, j, cmp_pos, k_w1, k_w2, v_w1, v_w2, w_out, next_gain, tables, batch, seqlen):
    kvw = NSA_KV_HEADS * HEAD_DIM
    c0 = ATTN_WIDTH
    tokens = batch * seqlen
    q = matmul(act,w_in, j, 0, c0, BF16)
    kcvc = matmul(act,w_in, j, c0, 2 * kvw, F32)
    kvsw = matmul(act,w_in, j, c0 + 2 * kvw, 4 * kvw, BF16)
    w_gate = jnp.pad(w_in[:, :, c0 + 6 * kvw:], ((0, 0), (0, 0), (0, LANES - 3 * N_HEADS)))
    gates_t = matmul(act, w_gate, j, 0, LANES, F32, transposed=True)
    kcmp, vcmp = nsa_compress(kcvc, cmp_pos, k_w1, k_w2, v_w1, v_w2, batch, seqlen)
    o, sel = nsa_cmp_attention(q, kcmp, vcmp, tables["cmp"], gates_t, batch, seqlen)
    o = nsa_selected_attention(q, kvsw, 0, kvw, sel, tables["causal"], gates_t, o, batch, seqlen)
    o = banded_attention(q.reshape(batch, 1, seqlen, c0), 0, kvsw.reshape(batch, 1, seqlen, 4 * kvw),
                         2 * kvw, 3 * kvw, tables["nsa_win"], batch=batch, seqlen=seqlen,
                         n_kv=NSA_KV_HEADS, dil=1, max_dist=NSA_WINDOW - 1, out_dtype=BF16,
                         gated=(gates_t, 2, o.reshape(batch, 1, seqlen, ATTN_WIDTH)))
    return matmul_residual(o.reshape(tokens, ATTN_WIDTH), w_out, j, h, 1.0, next_gain)


def dilated_mixer(act, h, w_in, j, w_out, next_gain, tables, batch, seqlen):
    kvw = DIL_KV_HEADS * HEAD_DIM
    group = ATTN_WIDTH + 2 * kvw
    outs, lses, dils = [], [], []
    for gi, (window, dil) in enumerate(DIL_PAIRS):
        proj = matmul(act,w_in, j, gi * group, group, BF16, dil=dil, batch=batch)
        proj = proj.reshape(batch, dil, seqlen // dil, group)
        o, lse = banded_attention(proj, 0, proj, ATTN_WIDTH, ATTN_WIDTH + kvw, tables["dil%d" % dil],
                                  batch=batch, seqlen=seqlen, n_kv=DIL_KV_HEADS, dil=dil,
                                  max_dist=window // dil, want_lse=True)
        outs.append(o)
        lses.append(lse)
        dils.append(dil)
    o = dilated_combine(outs, lses, dils, batch, seqlen)
    return matmul_residual(o.reshape(batch * seqlen, ATTN_WIDTH), w_out, j, h, 1.0, next_gain)


def moba_mixer(act, h, w_in, j, w_out, next_gain, tables, batch, seqlen):
    q = matmul(act,w_in, j, 0, ATTN_WIDTH, BF16)
    kv = matmul(act,w_in, j, ATTN_WIDTH, 2 * MOBA_KV_HEADS * HEAD_DIM, F32)
    o = moba_attention(q, kv, tables["causal"], batch, seqlen)
    return matmul_residual(o.reshape(batch * seqlen, ATTN_WIDTH), w_out, j, h, 1.0, next_gain)


def swa_mixer(act, h, w_in, j, sinks, w_out, next_gain, tables, batch, seqlen):
    kvw = SWA_KV_HEADS * HEAD_DIM
    width = ATTN_WIDTH + 2 * kvw
    proj = matmul(act,w_in, j, 0, width, BF16).reshape(batch, 1, seqlen, width)
    sink_row = jnp.repeat(sinks, math.gcd(seqlen, BAND_BLOCK))[None, :]
    o = banded_attention(proj, 0, proj, ATTN_WIDTH, ATTN_WIDTH + kvw, tables["swa"], batch=batch,
                         seqlen=seqlen, n_kv=SWA_KV_HEADS, dil=1, max_dist=SWA_WINDOW - 1, sink_row=sink_row,
                         out_dtype=BF16)
    return matmul_residual(o.reshape(batch * seqlen, ATTN_WIDTH), w_out, j, h, 1.0, next_gain)


def _band_table_for(rel_table, seqlen, dil, max_dist):
    sub = seqlen // dil
    tq = math.gcd(sub, BAND_BLOCK)
    n_prev = min(-(-max_dist // tq), sub // tq - 1)
    return band_bias_table(rel_table, tq, (n_prev + 1) * tq, n_prev * tq, max_dist, dil)


def kernel(x, rel_table, ffn1_norm, ffn1_w_gate, ffn1_w_up, ffn1_w_down, mix_norm, ffn2_norm, ffn2_w_gate, ffn2_w_up, ffn2_w_down, final_norm, nsa_w_in, nsa_cmp_pos, nsa_cmp_k_w1, nsa_cmp_k_w2, nsa_cmp_v_w1, nsa_cmp_v_w2, nsa_w_out, dil_w_in, dil_w_out, moba_w_in, moba_w_out, swa_w_in, swa_sinks, swa_w_out):
    batch, seqlen, d_model = x.shape
    depth = ffn1_norm.shape[0]
    n_mixers = 4
    h = x.reshape(batch * seqlen, d_model)

    tables = {
        "causal": causal_bias_table(rel_table, SEL_TILE, seqlen // SEL_TILE),
        "cmp": cmp_bias_table(rel_table, seqlen, (seqlen - NSA_CMP_LEN) // NSA_CMP_STRIDE + 1),
        "nsa_win": _band_table_for(rel_table, seqlen, 1, NSA_WINDOW - 1),
        "swa": _band_table_for(rel_table, seqlen, 1, SWA_WINDOW - 1),
    }
    for window, dil in DIL_PAIRS:
        tables["dil%d" % dil] = _band_table_for(rel_table, seqlen, dil, window // dil)

    act = norm_prep(h, ffn1_norm[0])
    for i in range(depth):
        h, act = ffn_half_step(h, act, ffn1_w_gate, ffn1_w_up, ffn1_w_down, i, mix_norm[i])
        m, j = i % n_mixers, i // n_mixers
        if m == 0:
            h, act = nsa_mixer(act, h, nsa_w_in, j, nsa_cmp_pos[j], nsa_cmp_k_w1[j], nsa_cmp_k_w2[j],
                               nsa_cmp_v_w1[j], nsa_cmp_v_w2[j], nsa_w_out, ffn2_norm[i], tables, batch, seqlen)
        elif m == 1:
            h, act = dilated_mixer(act, h, dil_w_in, j, dil_w_out, ffn2_norm[i], tables, batch, seqlen)
        elif m == 2:
            h, act = moba_mixer(act, h, moba_w_in, j, moba_w_out, ffn2_norm[i], tables, batch, seqlen)
        else:
            h, act = swa_mixer(act, h, swa_w_in, j, swa_sinks[j], swa_w_out, ffn2_norm[i], tables, batch, seqlen)
        next_gain = ffn1_norm[i + 1] if i + 1 < depth else None
        h, act = ffn_half_step(h, act, ffn2_w_gate, ffn2_w_up, ffn2_w_down, i, next_gain)
    return rms_norm(h, final_norm, x.dtype).reshape(batch, seqlen, d_model)
```

```python
import functools
import math

import numpy as np
import jax
import jax.numpy as jnp
from jax import lax
from jax.experimental import pallas as pl
from jax.experimental.pallas import tpu as pltpu

HEAD_DIM = 128
N_HEADS = 32
ATTN_WIDTH = N_HEADS * HEAD_DIM
RMS_EPS = 1e-6
REL_BUCKETS = 32
REL_MAX_DIST = 2048
BAND_BLOCK = 128
NSA_KV_HEADS = 4
NSA_CMP_LEN = 32
NSA_CMP_STRIDE = 16
NSA_SEL_LEN = 64
NSA_SEL_TOPN = 16
NSA_WINDOW = 512
DIL_PAIRS = ((128, 1), (512, 4), (2048, 16))
DIL_KV_HEADS = 8
MOBA_BLOCK = 256
MOBA_TOPK = 3
MOBA_KV_HEADS = 8
SWA_WINDOW = 128
SWA_KV_HEADS = 4
ATTN_SCALE = HEAD_DIM ** -0.5
LOG2E = 1.0 / math.log(2.0)
LN2 = math.log(2.0)
SCORE_SCALE = ATTN_SCALE * LOG2E
NEG_INF = -1e30
MASKED_BELOW = -5e29
TINY = 1e-20
FORCED_SCORE = 1e9

LANES = 128
SUBLANES = 8
BF16_ROWS = 16
SEL_TILE = 256
COMBINE_TILE = 1024
BANDED_ROWS_PER_STEP = 4096
TABLE_HEADS_PER_STEP = 8
VMEM_LIMIT = 56 * 1024 * 1024

F32 = jnp.float32
BF16 = jnp.bfloat16


def _params(semantics):
    return pltpu.CompilerParams(dimension_semantics=semantics, vmem_limit_bytes=VMEM_LIMIT)


def _dot_nt(a, b):
    return lax.dot_general(a, b, (((1,), (1,)), ((), ())), preferred_element_type=F32)


def _dot_tn(a, b):
    return lax.dot_general(a, b, (((0,), (0,)), ((), ())), preferred_element_type=F32)


def _stack_heads(q, n_heads):
    return jnp.concatenate([q[:, r * HEAD_DIM:(r + 1) * HEAD_DIM] for r in range(n_heads)], axis=0)


def _store_heads(o_ref, o_t, n_heads, tq, col0=0, mix=None):
    for r in range(n_heads):
        cols = slice(col0 + r * HEAD_DIM, col0 + (r + 1) * HEAD_DIM)
        blk = o_t[:, r * tq:(r + 1) * tq]
        if mix is not None:
            gate, branch, prev_ref = mix
            row = 3 * (col0 // HEAD_DIM + r) + branch
            blk = blk * gate[row:row + 1, :]
        val = blk.T
        if mix is not None and mix[2] is not None:
            val = mix[2][0, :, cols] + val
        o_ref[0, :, cols] = val.astype(o_ref.dtype)


def _rms_kernel(x_ref, g_ref, o_ref):
    x = x_ref[...]
    y = x * lax.rsqrt(jnp.mean(x * x, axis=-1, keepdims=True) + RMS_EPS)
    o_ref[...] = (y * g_ref[...]).astype(o_ref.dtype)


def rms_norm(x, gain, out_dtype):
    m, d = x.shape
    tm = 512
    return pl.pallas_call(
        _rms_kernel,
        grid=(m // tm,),
        in_specs=[pl.BlockSpec((tm, d), lambda i: (i, 0)), pl.BlockSpec((1, d), lambda i: (0, 0))],
        out_specs=pl.BlockSpec((tm, d), lambda i: (i, 0)),
        out_shape=jax.ShapeDtypeStruct((m, d), out_dtype),
        compiler_params=_params(("parallel",)),
        name="rms_norm",
    )(x, gain.reshape(1, d))


def _norm_prep_kernel(x_ref, g_ref, a_ref, ssq_ref):
    x = x_ref[...]
    a_ref[...] = (x * g_ref[...]).astype(a_ref.dtype)
    ssq_ref[...] = jnp.broadcast_to(jnp.sum(x * x, axis=-1, keepdims=True), ssq_ref.shape)


def norm_prep(x, gain):
    m, d = x.shape
    tm = 512
    return pl.pallas_call(
        _norm_prep_kernel,
        grid=(m // tm,),
        in_specs=[pl.BlockSpec((tm, d), lambda i: (i, 0)), pl.BlockSpec((1, d), lambda i: (0, 0))],
        out_specs=[pl.BlockSpec((tm, d), lambda i: (i, 0)), pl.BlockSpec((tm, LANES), lambda i: (i, 0))],
        out_shape=[jax.ShapeDtypeStruct((m, d), BF16), jax.ShapeDtypeStruct((m, LANES), F32)],
        compiler_params=_params(("parallel",)),
        name="norm_prep",
    )(x, gain.reshape(1, d))


def _row_scale(ssq_ref, d):
    return lax.rsqrt(ssq_ref[:, 0:1] * (1.0 / d) + RMS_EPS)


MM_VMEM_BUDGET = 50 * 1024 * 1024


def _mm_kernel(a_ref, ssq_ref, b_ref, o_ref, *scratch, dil, transposed):
    res = jnp.dot(a_ref[...], b_ref[...].astype(BF16), preferred_element_type=F32)
    res = res * _row_scale(ssq_ref, a_ref.shape[1])
    if transposed:
        o_ref[...] = res.T.astype(o_ref.dtype)
        return
    if dil == 1:
        o_ref[...] = res.astype(o_ref.dtype)
        return
    scr_ref, = scratch
    tm, tn = res.shape
    for s in range(tn // LANES):
        scr_ref[s] = res[:, s * LANES:(s + 1) * LANES]
    for rho in range(dil):
        for s in range(tn // LANES):
            o_ref[rho, :, s * LANES:(s + 1) * LANES] = (
                scr_ref[s, pl.ds(rho, tm // dil, stride=dil), :].astype(o_ref.dtype))


def _mm_res_kernel(a_ref, b_ref, r_ref, *rest, scale, emit_next):
    if emit_next:
        g_ref, o_ref, an_ref, ssq_ref = rest

        @pl.when(pl.program_id(1) == 0)
        def _():
            ssq_ref[...] = jnp.zeros(ssq_ref.shape, F32)
    else:
        o_ref, = rest
    h = r_ref[...] + scale * jnp.dot(a_ref[...], b_ref[...].astype(BF16), preferred_element_type=F32)
    o_ref[...] = h
    if emit_next:
        an_ref[...] = (h * g_ref[...]).astype(an_ref.dtype)
        ssq_ref[...] += jnp.sum(h * h, axis=-1, keepdims=True)


def _mm_swiglu_kernel(a_ref, ssq_ref, bg_ref, bu_ref, o_ref):
    a = a_ref[...]
    r = _row_scale(ssq_ref, a_ref.shape[1])
    gate = jnp.dot(a, bg_ref[...].astype(BF16), preferred_element_type=F32) * r
    up = jnp.dot(a, bu_ref[...].astype(BF16), preferred_element_type=F32) * r
    o_ref[...] = (jax.nn.silu(gate) * up).astype(o_ref.dtype)


def _mm_tiles(m, k, n, n_weights, io_bytes):
    tm = min(m, 1024)
    if n < LANES:
        return tm, n
    for tn in (512, 256, 128):
        need = (2 * tm * k * 2 + n_weights * (2 * k * tn * 4 + k * tn * 2) + 2 * tm * tn * io_bytes
                + n_weights * tm * tn * 4)
        if n % tn == 0 and need <= MM_VMEM_BUDGET:
            return tm, tn
    raise ValueError("no matmul tile fits VMEM")


def _weight_spec(w, layer, col_off, tn):
    assert col_off % tn == 0
    return pl.BlockSpec((None, w.shape[1], tn), lambda i, j: (layer, 0, col_off // tn + j))


def matmul(act, w, layer, col_off, n, out_dtype, *, dil=1, batch=1, transposed=False):
    a, ssq = act
    m, k = a.shape
    tm, tn = _mm_tiles(m, k, n, 1, jnp.dtype(out_dtype).itemsize)
    in_specs = [pl.BlockSpec((tm, k), lambda i, j: (i, 0)), pl.BlockSpec((tm, LANES), lambda i, j: (i, 0)),
                _weight_spec(w, layer, col_off, tn)]
    if transposed:
        assert dil == 1 and tn % LANES == 0
        out_specs = pl.BlockSpec((tn, tm), lambda i, j: (j, i))
        out_shape = jax.ShapeDtypeStruct((n, m), out_dtype)
        scratch = []
    elif dil == 1:
        out_specs = pl.BlockSpec((tm, tn), lambda i, j: (i, j))
        out_shape = jax.ShapeDtypeStruct((m, n), out_dtype)
        scratch = []
    else:
        per_batch = m // batch // tm
        assert m % (batch * tm) == 0 and tm % (dil * BF16_ROWS) == 0 and tn % LANES == 0
        out_specs = pl.BlockSpec((None, dil, tm // dil, tn), lambda i, j: (i // per_batch, 0, i % per_batch, j))
        out_shape = jax.ShapeDtypeStruct((batch, dil, m // batch // dil, n), out_dtype)
        scratch = [pltpu.VMEM((tn // LANES, tm, LANES), F32)]
    return pl.pallas_call(
        functools.partial(_mm_kernel, dil=dil, transposed=transposed),
        grid=(m // tm, n // tn),
        in_specs=in_specs,
        out_specs=out_specs,
        out_shape=out_shape,
        scratch_shapes=scratch,
        compiler_params=_params(("parallel", "arbitrary")),
        name="matmul",
    )(a, ssq, w)


def matmul_residual(a, w, layer, res, scale, next_gain=None):
    m, k = a.shape
    n = w.shape[2]
    emit_next = next_gain is not None
    tm, tn = _mm_tiles(m, k, n, 1, 10 if emit_next else 8)
    tile = pl.BlockSpec((tm, tn), lambda i, j: (i, j))
    in_specs = [pl.BlockSpec((tm, k), lambda i, j: (i, 0)), _weight_spec(w, layer, 0, tn), tile]
    args = [a, w, res]
    out_specs, out_shape = tile, jax.ShapeDtypeStruct((m, n), F32)
    if emit_next:
        in_specs.append(pl.BlockSpec((1, tn), lambda i, j: (0, j)))
        args.append(next_gain.reshape(1, n))
        out_specs = [tile, tile, pl.BlockSpec((tm, LANES), lambda i, j: (i, 0))]
        out_shape = [out_shape, jax.ShapeDtypeStruct((m, n), BF16), jax.ShapeDtypeStruct((m, LANES), F32)]
    outs = pl.pallas_call(
        functools.partial(_mm_res_kernel, scale=scale, emit_next=emit_next),
        grid=(m // tm, n // tn),
        in_specs=in_specs,
        out_specs=out_specs,
        out_shape=out_shape,
        compiler_params=_params(("parallel", "arbitrary")),
        name="matmul_residual",
    )(*args)
    return (outs[0], (outs[1], outs[2])) if emit_next else (outs, None)


def matmul_swiglu(act, wg, wu, layer):
    a, ssq = act
    m, k = a.shape
    n = wg.shape[2]
    tm, tn = _mm_tiles(m, k, n, 2, 2)
    return pl.pallas_call(
        _mm_swiglu_kernel,
        grid=(m // tm, n // tn),
        in_specs=[pl.BlockSpec((tm, k), lambda i, j: (i, 0)),
                  pl.BlockSpec((tm, LANES), lambda i, j: (i, 0)),
                  _weight_spec(wg, layer, 0, tn),
                  _weight_spec(wu, layer, 0, tn)],
        out_specs=pl.BlockSpec((tm, tn), lambda i, j: (i, j)),
        out_shape=jax.ShapeDtypeStruct((m, n), BF16),
        compiler_params=_params(("parallel", "arbitrary")),
        name="matmul_swiglu",
    )(a, ssq, wg, wu)


def ffn_half_step(h, act, w_gate, w_up, w_down, layer, next_gain):
    hidden = matmul_swiglu(act, w_gate, w_up, layer)
    return matmul_residual(hidden, w_down, layer, h, 0.5, next_gain)


def _t5_bucket(dist):
    n = jnp.maximum(dist, 0)
    exact = REL_BUCKETS // 2
    nf = jnp.maximum(n, 1).astype(F32)
    large = exact + (jnp.log(nf / exact) * ((REL_BUCKETS - exact) / math.log(REL_MAX_DIST / exact))).astype(jnp.int32)
    return jnp.where(n < exact, n, jnp.minimum(large, REL_BUCKETS - 1))


def _bias_table_kernel(tab_ref, o_ref, bucket_ref, *, base0, base_step, key_stride, max_dist, n_valid_keys,
                       dist_scale, heads_per_step):
    blk = pl.program_id(0)
    hg = pl.program_id(1)
    n_qry = bucket_ref.shape[1]

    @pl.when(hg == 0)
    def _():
        n_keys, n_qry = bucket_ref.shape
        key = lax.broadcasted_iota(jnp.int32, (n_keys, n_qry), 0)
        qry = lax.broadcasted_iota(jnp.int32, (n_keys, n_qry), 1)
        dist = base0 + blk * base_step + qry - key * key_stride
        valid = (dist >= 0) & (dist <= max_dist) & (key < n_valid_keys)
        bucket_ref[...] = jnp.where(valid, _t5_bucket(dist * dist_scale), -1)

    bucket = bucket_ref[...]
    bits = [(bucket & (1 << t)) != 0 for t in range(REL_BUCKETS.bit_length() - 1)]
    for hh in range(heads_per_step):
        h = hg * heads_per_step + hh
        level = [tab_ref[b, h] * LOG2E for b in range(REL_BUCKETS)]
        for odd in bits:
            level = [jnp.where(odd, level[2 * t + 1], level[2 * t]) for t in range(len(level) // 2)]
        tile = jnp.where(bucket >= 0, level[0], NEG_INF)
        if len(o_ref.shape) == 2:
            o_ref[:, hh * n_qry:(hh + 1) * n_qry] = tile
        else:
            o_ref[0, :, hh * n_qry:(hh + 1) * n_qry] = tile


def band_bias_table(rel_table, tq, span, pad, max_dist, dist_scale):
    kern = functools.partial(_bias_table_kernel, base0=pad, base_step=0, key_stride=1, max_dist=max_dist,
                             n_valid_keys=span, dist_scale=dist_scale, heads_per_step=TABLE_HEADS_PER_STEP)
    return pl.pallas_call(
        kern,
        grid=(1, N_HEADS // TABLE_HEADS_PER_STEP),
        in_specs=[pl.BlockSpec(memory_space=pltpu.SMEM)],
        out_specs=pl.BlockSpec((span, TABLE_HEADS_PER_STEP * tq), lambda j, h: (0, h)),
        out_shape=jax.ShapeDtypeStruct((span, N_HEADS * tq), F32),
        scratch_shapes=[pltpu.VMEM((span, tq), jnp.int32)],
        compiler_params=_params(("parallel", "arbitrary")),
        name="band_bias_table",
    )(rel_table)


def causal_bias_table(rel_table, tile, n_cls):
    kern = functools.partial(_bias_table_kernel, base0=0, base_step=tile, key_stride=1, max_dist=2 ** 30,
                             n_valid_keys=tile, dist_scale=1, heads_per_step=1)
    return pl.pallas_call(
        kern,
        grid=(n_cls, N_HEADS),
        in_specs=[pl.BlockSpec(memory_space=pltpu.SMEM)],
        out_specs=pl.BlockSpec((1, tile, tile), lambda c, h: (c, 0, h)),
        out_shape=jax.ShapeDtypeStruct((n_cls, tile, N_HEADS * tile), F32),
        scratch_shapes=[pltpu.VMEM((tile, tile), jnp.int32)],
        compiler_params=_params(("parallel", "arbitrary")),
        name="causal_bias_table",
    )(rel_table)


def cmp_bias_table(rel_table, seqlen, n_cmp):
    tq = SEL_TILE
    kern = functools.partial(_bias_table_kernel, base0=-(NSA_CMP_LEN - 1), base_step=tq,
                             key_stride=NSA_CMP_STRIDE, max_dist=2 ** 30, n_valid_keys=n_cmp, dist_scale=1,
                             heads_per_step=TABLE_HEADS_PER_STEP)
    return pl.pallas_call(
        kern,
        grid=(seqlen // tq, N_HEADS // TABLE_HEADS_PER_STEP),
        in_specs=[pl.BlockSpec(memory_space=pltpu.SMEM)],
        out_specs=pl.BlockSpec((1, LANES, TABLE_HEADS_PER_STEP * tq), lambda i, h: (i, 0, h)),
        out_shape=jax.ShapeDtypeStruct((seqlen // tq, LANES, N_HEADS * tq), F32),
        scratch_shapes=[pltpu.VMEM((LANES, tq), jnp.int32)],
        compiler_params=_params(("parallel", "arbitrary")),
        name="cmp_bias_table",
    )(rel_table)


def _banded_kernel(*refs, n_rep, n_grp, tq, n_prev, seq, has_sink, want_lse, gate_branch):
    q_ref, k_ref, v_ref, bias_ref = refs[:4]
    pos = 4
    sink_ref = None
    if has_sink:
        sink_ref = refs[pos]
        pos += 1
    mix = None
    if gate_branch is not None:
        mix = (jax.nn.sigmoid(refs[pos][...]), gate_branch, refs[pos + 1])
        pos += 2
    o_ref = refs[pos]
    pos += 1
    lse_ref = None
    if want_lse:
        lse_ref = refs[pos]
        pos += 1
    kpad_ref, vpad_ref = refs[pos:pos + 2]

    i = pl.program_id(3)
    pad = n_prev * tq
    span = pad + tq
    rows = n_rep * tq
    qw = n_rep * HEAD_DIM

    @pl.when(i == 0)
    def _():
        if pad:
            kpad_ref[0:pad, :] = jnp.zeros((pad, n_grp * HEAD_DIM), BF16)
            vpad_ref[0:pad, :] = jnp.zeros((pad, n_grp * HEAD_DIM), BF16)
        kpad_ref[pad:pad + seq, :] = k_ref[0]
        vpad_ref[pad:pad + seq, :] = v_ref[0]

    start = pl.multiple_of(i * tq, tq)

    def attend(gg, span_has_padding):
        kv_cols = slice(gg * HEAD_DIM, (gg + 1) * HEAD_DIM)
        row_cols = slice(gg * rows, (gg + 1) * rows)
        ks = kpad_ref[pl.ds(start, span), kv_cols]
        vs = vpad_ref[pl.ds(start, span), kv_cols]
        qs = _stack_heads(q_ref[0, :, gg * qw:(gg + 1) * qw], n_rep)
        lt = _dot_nt(ks, qs) * SCORE_SCALE + bias_ref[:, row_cols]
        if span_has_padding:
            key = lax.broadcasted_iota(jnp.int32, (span, rows), 0)
            lt = jnp.where(key >= pad - i * tq, lt, NEG_INF)
        m = jnp.max(lt, axis=0, keepdims=True)
        sink = None
        if has_sink:
            sink = sink_ref[:, row_cols] * LOG2E
            m = jnp.maximum(m, sink)
        p = jnp.exp2(lt - m)
        s = jnp.sum(p, axis=0, keepdims=True)
        if has_sink:
            s = s + jnp.exp2(sink - m)
        o_t = _dot_tn(vs, p.astype(BF16))
        s = jnp.maximum(s, TINY)
        _store_heads(o_ref, o_t / s, n_rep, tq, gg * qw, mix)
        if want_lse:
            lse = (m + jnp.log2(s)) * LN2
            head = lax.broadcasted_iota(jnp.int32, (LANES, tq), 0)
            tile = jnp.zeros((LANES, tq), F32)
            for r in range(n_rep):
                tile = jnp.where(head == r, lse[:, r * tq:(r + 1) * tq], tile)
            lse_ref[gg, 0] = tile.T

    if pad:
        @pl.when(i < n_prev)
        def _():
            for gg in range(n_grp):
                attend(gg, True)

        @pl.when(i >= n_prev)
        def _():
            for gg in range(n_grp):
                attend(gg, False)
    else:
        for gg in range(n_grp):
            attend(gg, False)


def banded_attention(q_arr, q_off, kv_arr, k_off, v_off, bias, *, batch, seqlen, n_kv, dil,
                     max_dist, sink_row=None, want_lse=False, out_dtype=F32, gated=None):
    n_rep = N_HEADS // n_kv
    sub = seqlen // dil
    tq = math.gcd(sub, BAND_BLOCK)
    n_blk = sub // tq
    n_prev = min(-(-max_dist // tq), n_blk - 1)
    span = (n_prev + 1) * tq
    qw = n_rep * HEAD_DIM
    rows = n_rep * tq
    n_grp = max(c for c in (8, 4, 2, 1) if c * rows <= BANDED_ROWS_PER_STEP and n_kv % c == 0)
    gqw, gkw = n_grp * qw, n_grp * HEAD_DIM
    assert q_arr.shape[:3] == kv_arr.shape[:3] == (batch, dil, sub)
    assert q_off % gqw == 0 and k_off % gkw == 0 and v_off % gkw == 0 and bias.shape == (span, N_HEADS * tq)

    in_specs = [
        pl.BlockSpec((None, 1, tq, gqw), lambda b, rho, g, i: (b, rho, i, q_off // gqw + g)),
        pl.BlockSpec((None, 1, sub, gkw), lambda b, rho, g, i: (b, rho, 0, k_off // gkw + g)),
        pl.BlockSpec((None, 1, sub, gkw), lambda b, rho, g, i: (b, rho, 0, v_off // gkw + g)),
        pl.BlockSpec((span, n_grp * rows), lambda b, rho, g, i: (0, g)),
    ]
    args = [q_arr, kv_arr, kv_arr, bias]
    if sink_row is not None:
        in_specs.append(pl.BlockSpec((1, n_grp * rows), lambda b, rho, g, i: (0, g)))
        args.append(sink_row)
    if gated is not None:
        gates_t, gate_branch, prev = gated
        assert dil == 1
        in_specs.append(pl.BlockSpec((3 * n_grp * n_rep, tq), lambda b, rho, g, i: (g, b * n_blk + i)))
        in_specs.append(pl.BlockSpec((None, 1, tq, gqw), lambda b, rho, g, i: (b, rho, i, g)))
        args += [gates_t, prev]
    out_specs = [pl.BlockSpec((None, 1, tq, gqw), lambda b, rho, g, i: (b, rho, i, g))]
    out_shape = [jax.ShapeDtypeStruct((batch, dil, sub, ATTN_WIDTH), out_dtype)]
    if want_lse:
        out_specs.append(pl.BlockSpec((None, n_grp, 1, tq, LANES), lambda b, rho, g, i: (b, g, rho, i, 0)))
        out_shape.append(jax.ShapeDtypeStruct((batch, n_kv, dil, sub, LANES), F32))
    kern = functools.partial(_banded_kernel, n_rep=n_rep, n_grp=n_grp, tq=tq, n_prev=n_prev, seq=sub,
                             has_sink=sink_row is not None, want_lse=want_lse,
                             gate_branch=None if gated is None else gated[1])
    outs = pl.pallas_call(
        kern,
        grid=(batch, dil, n_kv // n_grp, n_blk),
        in_specs=in_specs,
        out_specs=out_specs,
        out_shape=out_shape,
        scratch_shapes=[pltpu.VMEM((n_prev * tq + sub, gkw), BF16),
                        pltpu.VMEM((n_prev * tq + sub, gkw), BF16)],
        compiler_params=_params(("parallel", "parallel", "parallel", "arbitrary")),
        name="banded_attention",
    )(*args)
    return tuple(outs) if want_lse else outs[0]


def _softmax_pv(lt, v, o_ref, n_heads, tq, col0=0, mix=None):
    m = jnp.max(lt, axis=0, keepdims=True)
    p = jnp.exp2(lt - m)
    s = jnp.maximum(jnp.sum(p, axis=0, keepdims=True), TINY)
    _store_heads(o_ref, _dot_tn(v, p.astype(BF16)) / s, n_heads, tq, col0, mix)


def _first_rank(score, n_cand):
    idx = lax.broadcasted_iota(jnp.int32, score.shape, 0)
    rank = jnp.zeros(score.shape, F32)
    for jp in range(n_cand):
        other = score[jp:jp + 1, :]
        ahead = jnp.where(other > score, 1.0, jnp.where(other == score, jnp.where(idx > jp, 1.0, 0.0), 0.0))
        rank = rank + ahead
    return rank


def _nsa_cmp_kernel(x_ref, pos_ref, w1k_ref, w2k_ref, w1v_ref, w2v_ref, ko_ref, vo_ref):
    width = 2 * NSA_KV_HEADS * HEAD_DIM
    for kv, (w1_ref, w2_ref, o_ref) in enumerate(((w1k_ref, w2k_ref, ko_ref), (w1v_ref, w2v_ref, vo_ref))):
        for g in range(NSA_KV_HEADS):
            off = kv * NSA_KV_HEADS * HEAD_DIM + g * HEAD_DIM
            chunk = jnp.concatenate(
                [x_ref[0, :, l * width + off:l * width + off + HEAD_DIM] for l in range(NSA_CMP_STRIDE)], axis=1)
            first = jnp.dot((chunk + pos_ref[0:1, :]).astype(BF16), w1_ref[0], preferred_element_type=F32)
            second = jnp.dot((chunk + pos_ref[1:2, :]).astype(BF16), w1_ref[1], preferred_element_type=F32)
            hidden = jax.nn.gelu(first + pltpu.roll(second, second.shape[0] - 1, axis=0))
            o_ref[0, g] = jnp.dot(hidden.astype(BF16), w2_ref[...], preferred_element_type=F32).astype(o_ref.dtype)


def nsa_compress(kcvc, cmp_pos, k_w1, k_w2, v_w1, v_w2, batch, seqlen):
    n_chunk = seqlen // NSA_CMP_STRIDE
    width = 2 * NSA_KV_HEADS * HEAD_DIM
    half = NSA_CMP_STRIDE * HEAD_DIM
    x = kcvc.reshape(batch, n_chunk, NSA_CMP_STRIDE * width)
    out = jax.ShapeDtypeStruct((batch, NSA_KV_HEADS, n_chunk, HEAD_DIM), BF16)
    full = lambda shape: pl.BlockSpec(shape, lambda b: (0,) * len(shape))
    return pl.pallas_call(
        _nsa_cmp_kernel,
        grid=(batch,),
        in_specs=[pl.BlockSpec((1, n_chunk, NSA_CMP_STRIDE * width), lambda b: (b, 0, 0)),
                  full((2, half)), full((2, half, HEAD_DIM)), full((HEAD_DIM, HEAD_DIM)),
                  full((2, half, HEAD_DIM)), full((HEAD_DIM, HEAD_DIM))],
        out_specs=[pl.BlockSpec((1, NSA_KV_HEADS, n_chunk, HEAD_DIM), lambda b: (b, 0, 0, 0))] * 2,
        out_shape=[out, out],
        compiler_params=_params(("parallel",)),
        name="nsa_compress",
    )(x, cmp_pos.reshape(2, half), k_w1.reshape(2, half, HEAD_DIM).astype(BF16), k_w2.astype(BF16),
      v_w1.reshape(2, half, HEAD_DIM).astype(BF16), v_w2.astype(BF16))


def _nsa_cmp_attn_kernel(q_ref, kc_ref, vc_ref, bias_ref, c2s_ref, g_ref, o_ref, sel_ref, *, n_rep, tq, n_sel_blk):
    i = pl.program_id(2)
    qs = _stack_heads(q_ref[0], n_rep)
    bias = bias_ref[0]
    valid = bias > MASKED_BELOW
    lt = jnp.where(valid, _dot_nt(kc_ref[0, 0], qs) * SCORE_SCALE +bias, NEG_INF)
    m = jnp.max(lt, axis=0, keepdims=True)
    p = jnp.where(valid, jnp.exp2(lt - m), 0.0)
    s = jnp.sum(p, axis=0, keepdims=True)
    p_cmp = p / jnp.maximum(s, TINY)
    _store_heads(o_ref, _dot_tn(vc_ref[0, 0], p_cmp.astype(BF16)), n_rep, tq,
                 mix=(jax.nn.sigmoid(g_ref[...]), 0, None))

    p_sum = p_cmp[:, 0:tq]
    for r in range(1, n_rep):
        p_sum = p_sum + p_cmp[:, r * tq:(r + 1) * tq]
    imp = jnp.dot(c2s_ref[...], p_sum.astype(BF16), preferred_element_type=F32)[0:n_sel_blk]
    blk = lax.broadcasted_iota(jnp.int32, (n_sel_blk, tq), 0)
    tpos = i * tq + lax.broadcasted_iota(jnp.int32, (n_sel_blk, tq), 1)
    cur = tpos // NSA_SEL_LEN
    forced = (blk == 0) | (blk == cur) | (blk == cur - 1)
    score = jnp.where(forced, FORCED_SCORE, jnp.where(blk * NSA_SEL_LEN <= tpos, imp, NEG_INF))
    sel_ref[0, 0] = jnp.where(_first_rank(score, n_sel_blk) < min(NSA_SEL_TOPN, n_sel_blk), 1.0, 0.0)


def nsa_cmp_attention(q, kcmp, vcmp, bias, gates_t, batch, seqlen):
    n_rep = N_HEADS // NSA_KV_HEADS
    tq = SEL_TILE
    n_tile = seqlen // tq
    qw = n_rep * HEAD_DIM
    n_sel_blk = seqlen // NSA_SEL_LEN
    n_cmp = (seqlen - NSA_CMP_LEN) // NSA_CMP_STRIDE + 1
    a, b = NSA_SEL_LEN // NSA_CMP_STRIDE, NSA_CMP_LEN // NSA_CMP_STRIDE
    w = np.zeros((LANES, LANES), np.float32)
    j = np.arange(n_sel_blk)
    for mm in range(a):
        for nn in range(b):
            ii = a * j + mm + nn - (b - 1)
            ok = (ii >= 0) & (ii < n_cmp)
            np.add.at(w, (j[ok], ii[ok]), 1.0)
    kern = functools.partial(_nsa_cmp_attn_kernel, n_rep=n_rep, tq=tq, n_sel_blk=n_sel_blk)
    return pl.pallas_call(
        kern,
        grid=(batch, NSA_KV_HEADS, seqlen // tq),
        in_specs=[pl.BlockSpec((1, tq, qw), lambda b_, g, i: (b_, i, g)),
                  pl.BlockSpec((1, 1, LANES, HEAD_DIM), lambda b_, g, i: (b_, g, 0, 0)),
                  pl.BlockSpec((1, 1, LANES, HEAD_DIM), lambda b_, g, i: (b_, g, 0, 0)),
                  pl.BlockSpec((1, LANES, n_rep * tq), lambda b_, g, i: (i, 0, g)),
                  pl.BlockSpec((LANES, LANES), lambda b_, g, i: (0, 0)),
                  pl.BlockSpec((3 * n_rep, tq), lambda b_, g, i: (g, b_ * n_tile + i))],
        out_specs=[pl.BlockSpec((1, tq, qw), lambda b_, g, i: (b_, i, g)),
                   pl.BlockSpec((1, 1, n_sel_blk, tq), lambda b_, g, i: (b_, g, 0, i))],
        out_shape=[jax.ShapeDtypeStruct((batch, seqlen, ATTN_WIDTH), F32),
                   jax.ShapeDtypeStruct((batch, NSA_KV_HEADS, n_sel_blk, seqlen), F32)],
        compiler_params=_params(("parallel", "parallel", "arbitrary")),
        name="nsa_cmp_attention",
    )(q.reshape(batch, seqlen, ATTN_WIDTH), kcmp, vcmp, bias, jnp.asarray(w, BF16), gates_t)


def _nsa_sel_kernel(q_ref, k_ref, v_ref, sel_ref, bias_ref, g_ref, prev_ref, o_ref, *, n_rep, tq, n_blk, n_pass):
    i = pl.program_id(2)
    mix = (jax.nn.sigmoid(g_ref[...]), 1, prev_ref)
    per = tq // NSA_SEL_LEN
    hpp = n_rep // n_pass
    for k in range(n_blk):
        @pl.when(i == k)
        def _(k=k):
            n_keys = (k + 1) * tq
            keys = k_ref[0, 0:n_keys, :]
            vals = v_ref[0, 0:n_keys, :]
            for part in range(n_pass):
                qs = _stack_heads(q_ref[0, :, part * hpp * HEAD_DIM:(part + 1) * hpp * HEAD_DIM], hpp)
                cols = slice(part * hpp * tq, (part + 1) * hpp * tq)
                bias = jnp.concatenate([bias_ref[k - c, :, cols] for c in range(k + 1)], axis=0)
                lt = _dot_nt(keys, qs) * SCORE_SCALE + bias
                slabs = []
                for b in range((k + 1) * per):
                    on = jnp.concatenate([sel_ref[0, 0, b:b + 1, :]] * hpp, axis=1)
                    slabs.append(jnp.where(on > 0.5, lt[b * NSA_SEL_LEN:(b + 1) * NSA_SEL_LEN], NEG_INF))
                _softmax_pv(jnp.concatenate(slabs, axis=0), vals, o_ref, hpp, tq, part * hpp * HEAD_DIM, mix)


def nsa_selected_attention(q, kv, k_off, v_off, sel, causal_bias, gates_t, prev, batch, seqlen):
    n_rep = N_HEADS // NSA_KV_HEADS
    tq = SEL_TILE
    qw = n_rep * HEAD_DIM
    n_blk = seqlen // tq
    n_sel_blk = seqlen // NSA_SEL_LEN
    ckv = kv.shape[1]
    kv3 = kv.reshape(batch, seqlen, ckv)
    kern = functools.partial(_nsa_sel_kernel, n_rep=n_rep, tq=tq, n_blk=n_blk, n_pass=2)
    rows = n_rep * tq
    return pl.pallas_call(
        kern,
        grid=(batch, NSA_KV_HEADS, n_blk),
        in_specs=[pl.BlockSpec((1, tq, qw), lambda b, g, i: (b, i, g)),
                  pl.BlockSpec((1, seqlen, HEAD_DIM), lambda b, g, i: (b, 0, k_off // HEAD_DIM + g)),
                  pl.BlockSpec((1, seqlen, HEAD_DIM), lambda b, g, i: (b, 0, v_off // HEAD_DIM + g)),
                  pl.BlockSpec((1, 1, n_sel_blk, tq), lambda b, g, i: (b, g, 0, i)),
                  pl.BlockSpec((n_blk, tq, rows), lambda b, g, i: (0, 0, g), pipeline_mode=pl.Buffered(1)),
                  pl.BlockSpec((3 * n_rep, tq), lambda b, g, i: (g, b * n_blk + i)),
                  pl.BlockSpec((1, tq, qw), lambda b, g, i: (b, i, g))],
        out_specs=pl.BlockSpec((1, tq, qw), lambda b, g, i: (b, i, g)),
        out_shape=jax.ShapeDtypeStruct((batch, seqlen, ATTN_WIDTH), F32),
        compiler_params=_params(("parallel", "parallel", "arbitrary")),
        name="nsa_selected_attention",
    )(q.reshape(batch, seqlen, ATTN_WIDTH), kv3, kv3, sel, causal_bias, gates_t, prev)


def _moba_kernel(q_ref, k_ref, v_ref, bias_ref, o_ref, kb_ref, vb_ref, km_ref, *, n_rep, tq, n_blk):
    i = pl.program_id(2)
    rows = n_rep * tq

    @pl.when(i == 0)
    def _():
        k = k_ref[0]
        kb_ref[...] = k.astype(BF16)
        vb_ref[...] = v_ref[0].astype(BF16)
        slot = lax.broadcasted_iota(jnp.int32, (BF16_ROWS, HEAD_DIM), 0)
        means = jnp.zeros((BF16_ROWS, HEAD_DIM), F32)
        for j in range(n_blk):
            means = jnp.where(slot == j, jnp.mean(k[j * tq:(j + 1) * tq], axis=0, keepdims=True), means)
        km_ref[...] = means.astype(BF16)

    qs = _stack_heads(q_ref[0], n_rep)
    gate = _dot_nt(km_ref[...], qs)[0:n_blk]
    blk = lax.broadcasted_iota(jnp.int32, (n_blk, rows), 0)
    past = blk < i
    rank = _first_rank(jnp.where(past, gate, NEG_INF), n_blk)
    chosen = jnp.where(past, jnp.where(rank < min(MOBA_TOPK, max(n_blk - 1, 1)), 1.0, 0.0), 0.0)

    for k in range(n_blk):
        @pl.when(i == k)
        def _(k=k):
            n_keys = (k + 1) * tq
            bias = jnp.concatenate([bias_ref[k - c] for c in range(k + 1)], axis=0)
            lt = _dot_nt(kb_ref[0:n_keys, :], qs) * SCORE_SCALE + bias
            parts = [jnp.where(chosen[c:c + 1, :] > 0.5, lt[c * tq:(c + 1) * tq], NEG_INF) for c in range(k)]
            parts.append(lt[k * tq:n_keys])
            _softmax_pv(jnp.concatenate(parts, axis=0), vb_ref[0:n_keys, :], o_ref, n_rep, tq)


def moba_attention(q, kv, causal_bias, batch, seqlen):
    n_rep = N_HEADS // MOBA_KV_HEADS
    tq = MOBA_BLOCK
    qw = n_rep * HEAD_DIM
    n_blk = seqlen // tq
    assert n_blk <= SUBLANES
    rows = n_rep * tq
    kv3 = kv.reshape(batch, seqlen, kv.shape[1])
    kern = functools.partial(_moba_kernel, n_rep=n_rep, tq=tq, n_blk=n_blk)
    return pl.pallas_call(
        kern,
        grid=(batch, MOBA_KV_HEADS, n_blk),
        in_specs=[pl.BlockSpec((1, tq, qw), lambda b, g, i: (b, i, g)),
                  pl.BlockSpec((1, seqlen, HEAD_DIM), lambda b, g, i: (b, 0, g)),
                  pl.BlockSpec((1, seqlen, HEAD_DIM), lambda b, g, i: (b, 0, MOBA_KV_HEADS + g)),
                  pl.BlockSpec((n_blk, tq, rows), lambda b, g, i: (0, 0, g), pipeline_mode=pl.Buffered(1))],
        out_specs=pl.BlockSpec((1, tq, qw), lambda b, g, i: (b, i, g)),
        out_shape=jax.ShapeDtypeStruct((batch, seqlen, ATTN_WIDTH), BF16),
        scratch_shapes=[pltpu.VMEM((seqlen, HEAD_DIM), BF16), pltpu.VMEM((seqlen, HEAD_DIM), BF16),
                        pltpu.VMEM((BF16_ROWS, HEAD_DIM), BF16)],
        compiler_params=_params(("parallel", "parallel", "arbitrary")),
        name="moba_attention",
    )(q.reshape(batch, seqlen, ATTN_WIDTH), kv3, kv3, causal_bias)


def _dil_combine_kernel(*refs, n_rep, dils, tile):
    n_grp = len(dils)
    o_refs, l_refs, o_ref = refs[:n_grp], refs[n_grp:2 * n_grp], refs[2 * n_grp]
    scratch = list(refs[2 * n_grp + 1:])
    outs, lses = [], []
    for o_g, l_g, dil in zip(o_refs, l_refs, dils):
        if dil == 1:
            outs.append([o_g[0, :, r * HEAD_DIM:(r + 1) * HEAD_DIM] for r in range(n_rep)])
            lses.append(l_g[0])
            continue
        nat_o, nat_l = scratch.pop(0), scratch.pop(0)
        per = tile // dil
        for rho in range(dil):
            for r in range(n_rep):
                nat_o[r, pl.ds(rho, per, stride=dil), :] = o_g[rho, :, r * HEAD_DIM:(r + 1) * HEAD_DIM]
            nat_l[pl.ds(rho, per, stride=dil), :] = l_g[rho]
        outs.append([nat_o[r] for r in range(n_rep)])
        lses.append(nat_l[...])
    top = functools.reduce(jnp.maximum, lses)
    weights = [jnp.exp(l - top) for l in lses]
    den = functools.reduce(lambda x, y: x + y, weights)
    weights = [w / den for w in weights]
    for r in range(n_rep):
        mix = weights[0][:, r:r + 1] * outs[0][r]
        for w, o in zip(weights[1:], outs[1:]):
            mix = mix + w[:, r:r + 1] * o[r]
        o_ref[0, :, r * HEAD_DIM:(r + 1) * HEAD_DIM] = mix.astype(o_ref.dtype)


def dilated_combine(outs, lses, dils, batch, seqlen):
    n_rep = N_HEADS // DIL_KV_HEADS
    tile = COMBINE_TILE
    qw = n_rep * HEAD_DIM
    in_specs, scratch = [], []
    for dil in dils:
        in_specs.append(pl.BlockSpec((None, dil, tile // dil, qw), lambda b, g, i: (b, 0, i, g)))
    for dil in dils:
        in_specs.append(pl.BlockSpec((None, None, dil, tile // dil, LANES), lambda b, g, i: (b, g, 0, i, 0)))
        if dil > 1:
            scratch += [pltpu.VMEM((n_rep, tile, HEAD_DIM), F32), pltpu.VMEM((tile, LANES), F32)]
    return pl.pallas_call(
        functools.partial(_dil_combine_kernel, n_rep=n_rep, dils=tuple(dils), tile=tile),
        grid=(batch, DIL_KV_HEADS, seqlen // tile),
        in_specs=in_specs,
        out_specs=pl.BlockSpec((1, tile, qw), lambda b, g, i: (b, i, g)),
        out_shape=jax.ShapeDtypeStruct((batch, seqlen, ATTN_WIDTH), BF16),
        scratch_shapes=scratch,
        compiler_params=_params(("parallel", "parallel", "parallel")),
        name="dilated_combine",
    )(*outs, *lses)


def nsa_mixer(act, h, w_in, j, cmp_pos, k_w1, k_w2, v_w1, v_w2, w_out, next_gain, tables, batch, seqlen):
    kvw = NSA_KV_HEADS * HEAD_DIM
    c0 = ATTN_WIDTH
    tokens = batch * seqlen
    q = matmul(act,w_in, j, 0, c0, BF16)
    kcvc = matmul(act,w_in, j, c0, 2 * kvw, F32)
    kvsw = matmul(act,w_in, j, c0 + 2 * kvw, 4 * kvw, BF16)
    w_gate = jnp.pad(w_in[:, :, c0 + 6 * kvw:], ((0, 0), (0, 0), (0, LANES - 3 * N_HEADS)))
    gates_t = matmul(act, w_gate, j, 0, LANES, F32, transposed=True)
    kcmp, vcmp = nsa_compress(kcvc, cmp_pos, k_w1, k_w2, v_w1, v_w2, batch, seqlen)
    o, sel = nsa_cmp_attention(q, kcmp, vcmp, tables["cmp"], gates_t, batch, seqlen)
    o = nsa_selected_attention(q, kvsw, 0, kvw, sel, tables["causal"], gates_t, o, batch, seqlen)
    o = banded_attention(q.reshape(batch, 1, seqlen, c0), 0, kvsw.reshape(batch, 1, seqlen, 4 * kvw),
                         2 * kvw, 3 * kvw, tables["nsa_win"], batch=batch, seqlen=seqlen,
                         n_kv=NSA_KV_HEADS, dil=1, max_dist=NSA_WINDOW - 1, out_dtype=BF16,
                         gated=(gates_t, 2, o.reshape(batch, 1, seqlen, ATTN_WIDTH)))
    return matmul_residual(o.reshape(tokens, ATTN_WIDTH), w_out, j, h, 1.0, next_gain)


def dilated_mixer(act, h, w_in, j, w_out, next_gain, tables, batch, seqlen):
    kvw = DIL_KV_HEADS * HEAD_DIM
    group = ATTN_WIDTH + 2 * kvw
    outs, lses, dils = [], [], []
    for gi, (window, dil) in enumerate(DIL_PAIRS):
        proj = matmul(act,w_in, j, gi * group, group, BF16, dil=dil, batch=batch)
        proj = proj.reshape(batch, dil, seqlen // dil, group)
        o, lse = banded_attention(proj, 0, proj, ATTN_WIDTH, ATTN_WIDTH + kvw, tables["dil%d" % dil],
                                  batch=batch, seqlen=seqlen, n_kv=DIL_KV_HEADS, dil=dil,
                                  max_dist=window // dil, want_lse=True)
        outs.append(o)
        lses.append(lse)
        dils.append(dil)
    o = dilated_combine(outs, lses, dils, batch, seqlen)
    return matmul_residual(o.reshape(batch * seqlen, ATTN_WIDTH), w_out, j, h, 1.0, next_gain)


def moba_mixer(act, h, w_in, j, w_out, next_gain, tables, batch, seqlen):
    q = matmul(act,w_in, j, 0, ATTN_WIDTH, BF16)
    kv = matmul(act,w_in, j, ATTN_WIDTH, 2 * MOBA_KV_HEADS * HEAD_DIM, F32)
    o = moba_attention(q, kv, tables["causal"], batch, seqlen)
    return matmul_residual(o.reshape(batch * seqlen, ATTN_WIDTH), w_out, j, h, 1.0, next_gain)


def swa_mixer(act, h, w_in, j, sinks, w_out, next_gain, tables, batch, seqlen):
    kvw = SWA_KV_HEADS * HEAD_DIM
    width = ATTN_WIDTH + 2 * kvw
    proj = matmul(act,w_in, j, 0, width, BF16).reshape(batch, 1, seqlen, width)
    sink_row = jnp.repeat(sinks, math.gcd(seqlen, BAND_BLOCK))[None, :]
    o = banded_attention(proj, 0, proj, ATTN_WIDTH, ATTN_WIDTH + kvw, tables["swa"], batch=batch,
                         seqlen=seqlen, n_kv=SWA_KV_HEADS, dil=1, max_dist=SWA_WINDOW - 1, sink_row=sink_row,
                         out_dtype=BF16)
    return matmul_residual(o.reshape(batch * seqlen, ATTN_WIDTH), w_out, j, h, 1.0, next_gain)


def _band_table_for(rel_table, seqlen, dil, max_dist):
    sub = seqlen // dil
    tq = math.gcd(sub, BAND_BLOCK)
    n_prev = min(-(-max_dist // tq), sub // tq - 1)
    return band_bias_table(rel_table, tq, (n_prev + 1) * tq, n_prev * tq, max_dist, dil)


def kernel(x, rel_table, ffn1_norm, ffn1_w_gate, ffn1_w_up, ffn1_w_down, mix_norm, ffn2_norm, ffn2_w_gate, ffn2_w_up, ffn2_w_down, final_norm, nsa_w_in, nsa_cmp_pos, nsa_cmp_k_w1, nsa_cmp_k_w2, nsa_cmp_v_w1, nsa_cmp_v_w2, nsa_w_out, dil_w_in, dil_w_out, moba_w_in, moba_w_out, swa_w_in, swa_sinks, swa_w_out):
    batch, seqlen, d_model = x.shape
    depth = ffn1_norm.shape[0]
    n_mixers = 4
    h = x.reshape(batch * seqlen, d_model)

    tables = {
        "causal": causal_bias_table(rel_table, SEL_TILE, seqlen // SEL_TILE),
        "cmp": cmp_bias_table(rel_table, seqlen, (seqlen - NSA_CMP_LEN) // NSA_CMP_STRIDE + 1),
        "nsa_win": _band_table_for(rel_table, seqlen, 1, NSA_WINDOW - 1),
        "swa": _band_table_for(rel_table, seqlen, 1, SWA_WINDOW - 1),
    }
    for window, dil in DIL_PAIRS:
        tables["dil%d" % dil] = _band_table_for(rel_table, seqlen, dil, window // dil)

    act = norm_prep(h, ffn1_norm[0])
    for i in range(depth):
        h, act = ffn_half_step(h, act, ffn1_w_gate, ffn1_w_up, ffn1_w_down, i, mix_norm[i])
        m, j = i % n_mixers, i // n_mixers
        if m == 0:
            h, act = nsa_mixer(act, h, nsa_w_in, j, nsa_cmp_pos[j], nsa_cmp_k_w1[j], nsa_cmp_k_w2[j],
                               nsa_cmp_v_w1[j], nsa_cmp_v_w2[j], nsa_w_out, ffn2_norm[i], tables, batch, seqlen)
        elif m == 1:
            h, act = dilated_mixer(act, h, dil_w_in, j, dil_w_out, ffn2_norm[i], tables, batch, seqlen)
        elif m == 2:
            h, act = moba_mixer(act, h, moba_w_in, j, moba_w_out, ffn2_norm[i], tables, batch, seqlen)
        else:
            h, act = swa_mixer(act, h, swa_w_in, j, swa_sinks[j], swa_w_out, ffn2_norm[i], tables, batch, seqlen)
        next_gain = ffn1_norm[i + 1] if i + 1 < depth else None
        h, act = ffn_half_step(h, act, ffn2_w_gate, ffn2_w_up, ffn2_w_down, i, next_gain)
    return rms_norm(h, final_norm, x.dtype).reshape(batch, seqlen, d_model)
```
